```python
import math
import jax, jax.numpy as jnp
from jax import lax
import numpy as np

D_MODEL = 2048
BATCH = 2
SEQ = 4096
DEPTH = 2

GRID_W = 64
CTX_LEN = 256
HEAD_DIM = 128
ROT_AXIS = HEAD_DIM // 2
ROPE_THETA = 10000.0
EPS = 1e-6
Q_BLOCK = 128
NEG_INF = -1e30
N_MOD = 6

A_HEADS = 8
A_KV_HEADS = 2
WINDOW = 128
B_HEADS = 4
B_DIM = HEAD_DIM
B_VDIM = 2 * HEAD_DIM
C_HEADS = D_MODEL // HEAD_DIM
C_KV_HEADS = 4

A_Q = A_HEADS * HEAD_DIM
A_KV = A_KV_HEADS * HEAD_DIM
B_QK = B_HEADS * 2 * B_DIM
B_V = B_HEADS * B_VDIM
EVEN_IN = A_Q + 2 * A_KV + 2 * B_QK + B_V
EVEN_SPLITS = [A_Q, A_Q + A_KV, A_Q + 2 * A_KV, A_Q + 2 * A_KV + B_QK, A_Q + 2 * A_KV + 2 * B_QK]
EVEN_MIX = A_Q + B_V
C_Q = C_HEADS * HEAD_DIM
C_KV = C_KV_HEADS * HEAD_DIM
ODD_IN = C_Q + 2 * C_KV
ODD_SPLITS = [C_Q, C_Q + C_KV]
ODD_MIX = C_Q

D_FF = 5632
N_EXPERTS = 8
TOP_K = 2
D_FF_EXPERT = 7168
MOE_BLOCK = 128

N_EVEN = (DEPTH + 1) // 2
N_ODD = DEPTH // 2

kernel_name = 'hybrid_diffusion_trunk_window_diff_gqa_moe'


def rms_norm(x, g):
    xf = x.astype(jnp.float32)
    y = xf * lax.rsqrt(jnp.mean(xf * xf, axis=-1, keepdims=True) + EPS)
    return (y * g.astype(jnp.float32)).astype(x.dtype)


def modulate(h, shift, scale):
    return h * (1 + scale) + shift


def heads(t, n):
    b, s, _ = t.shape
    return t.reshape(b, s, n, -1).transpose(0, 2, 1, 3)


def merge_heads(t):
    b, h, s, d = t.shape
    return t.transpose(0, 2, 1, 3).reshape(b, s, h * d)


def axial_rope_tables(rows, dtype):
    row = jnp.repeat(jnp.arange(rows, dtype=jnp.float32), GRID_W)
    col = jnp.tile(jnp.arange(GRID_W, dtype=jnp.float32), rows)
    inv = ROPE_THETA ** (-jnp.arange(0, ROT_AXIS, 2, dtype=jnp.float32) / ROT_AXIS)
    ang = jnp.concatenate([row[:, None] * inv, col[:, None] * inv], axis=-1)
    return jnp.cos(ang).astype(dtype), jnp.sin(ang).astype(dtype)


def apply_rope(x, cos, sin):
    h = ROT_AXIS // 2
    r1, r2, c1, c2 = jnp.split(x, 4, axis=-1)
    cr, cc = cos[:, :h], cos[:, h:]
    sr, sc = sin[:, :h], sin[:, h:]
    return jnp.concatenate([r1 * cr - r2 * sr, r2 * cr + r1 * sr, c1 * cc - c2 * sc, c2 * cc + c1 * sc], axis=-1)


def sweep_query_blocks(fn, *qs):
    L = qs[0].shape[-2]
    nb = L // Q_BLOCK
    blocks = tuple(jnp.moveaxis(t.reshape(t.shape[:-2] + (nb, Q_BLOCK, t.shape[-1])), -3, 0) for t in qs)
    out = lax.map(lambda bl: fn(*bl), blocks)
    out = jnp.moveaxis(out, 0, -3)
    return out.reshape(out.shape[:-3] + (L, out.shape[-1]))


def window_sink_attention(q, k, v, kc, vc, sink):
    b, _, L, d = q.shape
    nb = L // Q_BLOCK
    g = A_HEADS // A_KV_HEADS
    scale = d ** -0.5
    qb = q.reshape(b, A_KV_HEADS, g, nb, Q_BLOCK, d)
    pad = ((0, 0), (0, 0), (Q_BLOCK, Q_BLOCK), (0, 0))

    def band(t):
        tb = jnp.pad(t, pad).reshape(b, A_KV_HEADS, nb + 2, Q_BLOCK, d)
        return jnp.concatenate([tb[:, :, :-2], tb[:, :, 1:-1], tb[:, :, 2:]], axis=3)

    kw, vw = band(k), band(v)
    s_loc = jnp.einsum('bkgnqd,bknsd->bkgnqs', qb, kw).astype(jnp.float32) * scale
    s_ctx = jnp.einsum('bkgnqd,bkcd->bkgnqc', qb, kc).astype(jnp.float32) * scale
    blk = jnp.arange(nb)[:, None, None] * Q_BLOCK
    qpos = blk + jnp.arange(Q_BLOCK)[None, :, None]
    kpos = blk - Q_BLOCK + jnp.arange(3 * Q_BLOCK)[None, None, :]
    mask = (jnp.abs(qpos - kpos) <= WINDOW) & (kpos >= 0) & (kpos < L)
    s_loc = jnp.where(mask, s_loc, NEG_INF)
    s_sink = jnp.broadcast_to(sink.astype(jnp.float32).reshape(1, A_KV_HEADS, g, 1, 1, 1), s_ctx.shape[:-1] + (1,))
    p = jax.nn.softmax(jnp.concatenate([s_loc, s_ctx, s_sink], axis=-1), axis=-1)
    nw = 3 * Q_BLOCK
    nc = kc.shape[2]
    o = (jnp.einsum('bkgnqs,bknsd->bkgnqd', p[..., :nw].astype(v.dtype), vw)
         + jnp.einsum('bkgnqc,bkcd->bkgnqd', p[..., nw:nw + nc].astype(vc.dtype), vc))
    return o.reshape(b, A_HEADS, L, d)


def context_sink_attention(qc, kc, vc, sink):
    b, _, n, d = qc.shape
    g = A_HEADS // A_KV_HEADS
    qg = qc.reshape(b, A_KV_HEADS, g, n, d)
    s = jnp.einsum('bkgqd,bkcd->bkgqc', qg, kc).astype(jnp.float32) * (d ** -0.5)
    s_sink = jnp.broadcast_to(sink.astype(jnp.float32).reshape(1, A_KV_HEADS, g, 1, 1), s.shape[:-1] + (1,))
    p = jax.nn.softmax(jnp.concatenate([s, s_sink], axis=-1), axis=-1)
    o = jnp.einsum('bkgqc,bkcd->bkgqd', p[..., :n].astype(vc.dtype), vc)
    return o.reshape(b, A_HEADS, n, d)


def diff_core(q1, q2, k1, k2, v, lam):
    scale = q1.shape[-1] ** -0.5
    a1 = jax.nn.softmax(jnp.einsum('bhqd,bhkd->bhqk', q1, k1).astype(jnp.float32) * scale, axis=-1)
    a2 = jax.nn.softmax(jnp.einsum('bhqd,bhkd->bhqk', q2, k2).astype(jnp.float32) * scale, axis=-1)
    att = a1 - lam[None, :, None, None] * a2
    return jnp.einsum('bhqk,bhkd->bhqd', att.astype(v.dtype), v)


def gqa_core(q, k, v):
    s = jnp.einsum('bkgqd,bksd->bkgqs', q, k).astype(jnp.float32) * (q.shape[-1] ** -0.5)
    p = jax.nn.softmax(s, axis=-1)
    return jnp.einsum('bkgqs,bksd->bkgqd', p.astype(v.dtype), v)


def swiglu(h, w_in, w_down):
    g_, u_ = jnp.split(h @ w_in, 2, axis=-1)
    return (jax.nn.silu(g_) * u_) @ w_down


def moe_swiglu(h, w_router, exp_w_in, exp_w_down, layer):
    bq, s, d = h.shape
    T = bq * s
    xt = h.reshape(T, d)
    logits = (xt @ w_router).astype(jnp.float32)
    top_v, top_i = lax.top_k(logits, TOP_K)
    gates = jax.nn.softmax(top_v, axis=-1)
    tk = T * TOP_K
    flat_e = top_i.reshape(tk)
    flat_tok = jnp.arange(tk, dtype=jnp.int32) // TOP_K
    flat_g = gates.reshape(tk)
    counts = jnp.zeros((N_EXPERTS,), jnp.int32).at[flat_e].add(1)
    padded = (counts + MOE_BLOCK - 1) // MOE_BLOCK * MOE_BLOCK
    pad_end = jnp.cumsum(padded)
    pad_start = pad_end - padded
    start = jnp.cumsum(counts) - counts
    order = jnp.argsort(flat_e)
    se = flat_e[order]
    dest = pad_start[se] + jnp.arange(tk, dtype=jnp.int32) - start[se]
    n_rows = (tk + MOE_BLOCK - 1) // MOE_BLOCK * MOE_BLOCK + N_EXPERTS * MOE_BLOCK
    row_tok = jnp.full((n_rows,), T, jnp.int32).at[dest].set(flat_tok[order])
    row_gate = jnp.zeros((n_rows,), jnp.float32).at[dest].set(flat_g[order])
    nblk = n_rows // MOE_BLOCK
    blk_e = jnp.minimum(jnp.sum(jnp.arange(nblk)[:, None] * MOE_BLOCK >= pad_end[None, :], axis=1), N_EXPERTS - 1)
    xs = jnp.concatenate([xt, jnp.zeros((1, d), xt.dtype)], axis=0)[row_tok].reshape(nblk, MOE_BLOCK, d)

    def expert_block(args):
        xb, e = args
        g_, u_ = jnp.split(xb @ exp_w_in[layer, e], 2, axis=-1)
        return (jax.nn.silu(g_) * u_) @ exp_w_down[layer, e]

    ys = lax.map(expert_block, (xs, blk_e)).reshape(n_rows, d)
    ys = ys * row_gate[:, None].astype(ys.dtype)
    out = jax.ops.segment_sum(ys, row_tok, num_segments=T + 1)[:T]
    return out.reshape(bq, s, d)


def modulation(c, c_ctx, w_mod, b_mod):
    m = jax.nn.silu(c) @ w_mod + b_mod
    mc = jax.nn.silu(c_ctx) @ w_mod + b_mod
    return jnp.split(m[:, None, :], N_MOD, axis=-1), jnp.split(mc, N_MOD, axis=-1)


def even_mixer(h, hc, need_ctx, layer_idx, cos, sin, w_in, w_out, a_qn, a_kn, a_sink,
               b_qn, b_kn, lq1, lk1, lq2, lk2, b_subln):
    z = jnp.split(h @ w_in, EVEN_SPLITS, axis=-1)
    zc = jnp.split(hc @ w_in, EVEN_SPLITS, axis=-1)

    def a_qkv(parts, rope):
        q = rms_norm(heads(parts[0], A_HEADS), a_qn)
        k = rms_norm(heads(parts[1], A_KV_HEADS), a_kn)
        v = heads(parts[2], A_KV_HEADS)
        if rope:
            q, k = apply_rope(q, cos, sin), apply_rope(k, cos, sin)
        return q, k, v

    def b_qkv(parts, rope):
        q1, q2 = jnp.split(heads(parts[3], B_HEADS), 2, axis=-1)
        k1, k2 = jnp.split(heads(parts[4], B_HEADS), 2, axis=-1)
        v = heads(parts[5], B_HEADS)
        q1, q2 = rms_norm(q1, b_qn), rms_norm(q2, b_qn)
        k1, k2 = rms_norm(k1, b_kn), rms_norm(k2, b_kn)
        if rope:
            q1, q2 = apply_rope(q1, cos, sin), apply_rope(q2, cos, sin)
            k1, k2 = apply_rope(k1, cos, sin), apply_rope(k2, cos, sin)
        return q1, q2, k1, k2, v

    qa, ka, va = a_qkv(z, True)
    qac, kac, vac = a_qkv(zc, False)
    oa = window_sink_attention(qa, ka, va, kac, vac, a_sink)
    q1, q2, k1, k2, vb = b_qkv(z, True)
    q1c, q2c, k1c, k2c, vbc = b_qkv(zc, False)
    lam_init = 0.8 - 0.6 * math.exp(-0.3 * layer_idx)
    f32 = jnp.float32
    lam = (jnp.exp(jnp.sum(lq1.astype(f32) * lk1.astype(f32), axis=-1))
           - jnp.exp(jnp.sum(lq2.astype(f32) * lk2.astype(f32), axis=-1)) + lam_init)
    k1a = jnp.concatenate([k1, k1c], axis=2)
    k2a = jnp.concatenate([k2, k2c], axis=2)
    va_all = jnp.concatenate([vb, vbc], axis=2)
    ob = sweep_query_blocks(lambda a1, a2: diff_core(a1, a2, k1a, k2a, va_all, lam), q1, q2)
    ob = rms_norm(ob, b_subln) * (1 - lam_init)
    y = jnp.concatenate([merge_heads(oa), merge_heads(ob)], axis=-1) @ w_out
    yc = None
    if need_ctx:
        oac = context_sink_attention(qac, kac, vac, a_sink)
        obc = rms_norm(diff_core(q1c, q2c, k1c, k2c, vbc, lam), b_subln) * (1 - lam_init)
        yc = jnp.concatenate([merge_heads(oac), merge_heads(obc)], axis=-1) @ w_out
    return y, yc


def odd_mixer(h, hc, need_ctx, cos, sin, w_in, w_out, c_qn, c_kn):
    z = jnp.split(h @ w_in, ODD_SPLITS, axis=-1)
    zc = jnp.split(hc @ w_in, ODD_SPLITS, axis=-1)
    b, L, _ = h.shape
    g = C_HEADS // C_KV_HEADS
    q = apply_rope(rms_norm(heads(z[0], C_HEADS), c_qn), cos, sin)
    k = apply_rope(rms_norm(heads(z[1], C_KV_HEADS), c_kn), cos, sin)
    v = heads(z[2], C_KV_HEADS)
    qc = rms_norm(heads(zc[0], C_HEADS), c_qn)
    kc = rms_norm(heads(zc[1], C_KV_HEADS), c_kn)
    vc = heads(zc[2], C_KV_HEADS)
    k_all = jnp.concatenate([k, kc], axis=2)
    v_all = jnp.concatenate([v, vc], axis=2)
    qg = q.reshape(b, C_KV_HEADS, g, L, HEAD_DIM)
    o = sweep_query_blocks(lambda qb: gqa_core(qb, k_all, v_all), qg).reshape(b, C_HEADS, L, HEAD_DIM)
    y = merge_heads(o) @ w_out
    yc = None
    if need_ctx:
        n = hc.shape[1]
        oc = gqa_core(qc.reshape(b, C_KV_HEADS, g, n, HEAD_DIM), kc, vc).reshape(b, C_HEADS, n, HEAD_DIM)
        yc = merge_heads(oc) @ w_out
    return y, yc


def setup_inputs(seed: int = 0) -> dict:
    key = jax.random.key(seed)
    ks = iter(jax.random.split(key, 40))

    def nrm(shape, s):
        return jax.random.normal(next(ks), shape, jnp.float32) * s

    def gain(shape):
        return 1.0 + nrm(shape, 0.02)

    D = D_MODEL
    E, O = N_EVEN, N_ODD
    return {
        'x': nrm((BATCH, SEQ, D), 1.0),
        'c': nrm((BATCH, D), 1.0),
        'ctx': nrm((BATCH, CTX_LEN, D), 1.0),
        'c_ctx': nrm((D,), 1.0),
        'e_norm1': gain((E, D)),
        'e_norm2': gain((E, D)),
        'e_w_mod': nrm((E, D, N_MOD * D), 0.5 * D ** -0.5),
        'e_b_mod': nrm((E, N_MOD * D), 0.02),
        'e_w_in': nrm((E, D, EVEN_IN), D ** -0.5),
        'e_w_out': nrm((E, EVEN_MIX, D), EVEN_MIX ** -0.5),
        'e_a_qnorm': gain((E, HEAD_DIM)),
        'e_a_knorm': gain((E, HEAD_DIM)),
        'e_a_sink': nrm((E, A_HEADS), 0.5),
        'e_b_qnorm': gain((E, HEAD_DIM)),
        'e_b_knorm': gain((E, HEAD_DIM)),
        'e_b_lam_q1': nrm((E, B_HEADS, B_DIM), 0.1),
        'e_b_lam_k1': nrm((E, B_HEADS, B_DIM), 0.1),
        'e_b_lam_q2': nrm((E, B_HEADS, B_DIM), 0.1),
        'e_b_lam_k2': nrm((E, B_HEADS, B_DIM), 0.1),
        'e_b_subln': gain((E, B_VDIM)),
        'e_ffn_w_in': nrm((E, D, 2 * D_FF), D ** -0.5),
        'e_ffn_w_down': nrm((E, D_FF, D), D_FF ** -0.5),
        'o_norm1': gain((O, D)),
        'o_norm2': gain((O, D)),
        'o_w_mod': nrm((O, D, N_MOD * D), 0.5 * D ** -0.5),
        'o_b_mod': nrm((O, N_MOD * D), 0.02),
        'o_w_in': nrm((O, D, ODD_IN), D ** -0.5),
        'o_w_out': nrm((O, ODD_MIX, D), ODD_MIX ** -0.5),
        'o_c_qnorm': gain((O, HEAD_DIM)),
        'o_c_knorm': gain((O, HEAD_DIM)),
        'o_router': nrm((O, D, N_EXPERTS), D ** -0.5),
        'o_exp_w_in': nrm((O, N_EXPERTS, D, 2 * D_FF_EXPERT), D ** -0.5),
        'o_exp_w_down': nrm((O, N_EXPERTS, D_FF_EXPERT, D), D_FF_EXPERT ** -0.5),
    }


def reference(x, c, ctx, c_ctx, e_norm1, e_norm2, e_w_mod, e_b_mod, e_w_in, e_w_out, e_a_qnorm, e_a_knorm,
              e_a_sink, e_b_qnorm, e_b_knorm, e_b_lam_q1, e_b_lam_k1, e_b_lam_q2, e_b_lam_k2, e_b_subln,
              e_ffn_w_in, e_ffn_w_down, o_norm1, o_norm2, o_w_mod, o_b_mod, o_w_in, o_w_out, o_c_qnorm,
              o_c_knorm, o_router, o_exp_w_in, o_exp_w_down):
    L = x.shape[1]
    rows = L // GRID_W
    cos, sin = axial_rope_tables(rows, x.dtype)
    xc = ctx
    for l in range(DEPTH):
        need_ctx = l < DEPTH - 1
        i = l // 2
        if l % 2 == 0:
            (sh1, sc1, g1, sh2, sc2, g2), (csh1, csc1, cg1, csh2, csc2, cg2) = modulation(c, c_ctx, e_w_mod[i], e_b_mod[i])
            h = modulate(rms_norm(x, e_norm1[i]), sh1, sc1)
            hc = modulate(rms_norm(xc, e_norm1[i]), csh1, csc1)
            y, yc = even_mixer(h, hc, need_ctx, l, cos, sin, e_w_in[i], e_w_out[i], e_a_qnorm[i], e_a_knorm[i],
                               e_a_sink[i], e_b_qnorm[i], e_b_knorm[i], e_b_lam_q1[i], e_b_lam_k1[i],
                               e_b_lam_q2[i], e_b_lam_k2[i], e_b_subln[i])
            x = x + g1 * y
            x = x + g2 * swiglu(modulate(rms_norm(x, e_norm2[i]), sh2, sc2), e_ffn_w_in[i], e_ffn_w_down[i])
            if need_ctx:
                xc = xc + cg1 * yc
                xc = xc + cg2 * swiglu(modulate(rms_norm(xc, e_norm2[i]), csh2, csc2), e_ffn_w_in[i], e_ffn_w_down[i])
        else:
            (sh1, sc1, g1, sh2, sc2, g2), (csh1, csc1, cg1, csh2, csc2, cg2) = modulation(c, c_ctx, o_w_mod[i], o_b_mod[i])
            h = modulate(rms_norm(x, o_norm1[i]), sh1, sc1)
            hc = modulate(rms_norm(xc, o_norm1[i]), csh1, csc1)
            y, yc = odd_mixer(h, hc, need_ctx, cos, sin, o_w_in[i], o_w_out[i], o_c_qnorm[i], o_c_knorm[i])
            x = x + g1 * y
            x = x + g2 * moe_swiglu(modulate(rms_norm(x, o_norm2[i]), sh2, sc2), o_router[i], o_exp_w_in, o_exp_w_down, i)
            if need_ctx:
                xc = xc + cg1 * yc
                xc = xc + cg2 * moe_swiglu(modulate(rms_norm(xc, o_norm2[i]), csh2, csc2), o_router[i], o_exp_w_in, o_exp_w_down, i)
    return x
```

```python
import functools
import math

import jax
import jax.numpy as jnp
from jax import lax
from jax.experimental import pallas as pl
from jax.experimental.pallas import tpu as pltpu

F32 = jnp.float32
BF16 = jnp.bfloat16

HEAD_DIM = 128
GRID_W = 64
ROT_AXIS = HEAD_DIM // 2
ROPE_THETA = 10000.0
EPS = 1e-6
NEG_INF = -1e30
N_MOD = 6
WINDOW = 128
A_HEADS, A_KV_HEADS = 8, 2
B_HEADS = 4
C_KV_HEADS = 4
N_EXPERTS = 8
LANES = 128
V7X_VMEM_BYTES = 64 * 1024 * 1024
VMEM_TEMP_ALLOWANCE = 16 * 1024 * 1024


def _vmem_limit(*block_bytes):
    need = 2 * sum(block_bytes) + VMEM_TEMP_ALLOWANCE
    return int(min(need, V7X_VMEM_BYTES - 6 * 1024 * 1024))


def _params(sem, *block_bytes):
    return pltpu.CompilerParams(dimension_semantics=sem, vmem_limit_bytes=_vmem_limit(*block_bytes))


def _nt_dot(a, b):
    return lax.dot_general(a, b, (((1,), (1,)), ((), ())), preferred_element_type=F32)


def _rms_mod(x, g, shift, scale):
    ms = jnp.mean(x * x, axis=-1, keepdims=True)
    y = x * lax.rsqrt(ms + EPS) * g
    return y * (1.0 + scale) + shift


def _mod_kernel(c_ref, w_ref, b_ref, o_ref):
    c = c_ref[...]
    s = c * jax.nn.sigmoid(c)
    s_hi = s.astype(BF16).astype(F32)
    top = lax.broadcasted_iota(jnp.int32, s.shape, 0) < 8
    lhs = jnp.where(top, s_hi, s - s_hi).astype(BF16)
    acc = jnp.dot(lhs, w_ref[...].astype(BF16), preferred_element_type=F32)
    o_ref[...] = acc[0:8] + acc[8:16] + b_ref[...]


def _modulation(cvec, w_mod, b_mod):
    d, n = w_mod.shape
    tn = 1024
    return pl.pallas_call(
        _mod_kernel,
        out_shape=jax.ShapeDtypeStruct((8, n), F32),
        grid=(n // tn,),
        in_specs=[pl.BlockSpec((16, d), lambda j: (0, 0)),
                  pl.BlockSpec((d, tn), lambda j: (0, j)),
                  pl.BlockSpec((1, tn), lambda j: (0, j))],
        out_specs=pl.BlockSpec((8, tn), lambda j: (0, j)),
        compiler_params=_params(("arbitrary",), d * tn * 4, d * tn * 2),
        name="modulation",
    )(cvec, w_mod, b_mod.reshape(1, n))


def _inproj_kernel(kind_ref, x_ref, g_ref, sh_ref, sc_ref, w_ref, gc_ref, cos_ref, sa_ref, sb_ref,
                   o_ref, h_ref, *, rope):
    j = pl.program_id(1)
    tn = o_ref.shape[1]
    nch = tn // LANES

    @pl.when(j == 0)
    def _():
        h_ref[...] = _rms_mod(x_ref[...], g_ref[...], sh_ref[0], sc_ref[0]).astype(BF16)

    acc = jnp.dot(h_ref[...], w_ref[...], preferred_element_type=F32)
    for c in range(nch):
        a = acc[:, c * LANES:(c + 1) * LANES]
        kind = kind_ref[j * nch + c]

        @pl.when(kind == 0)
        def _():
            o_ref[:, c * LANES:(c + 1) * LANES] = a.astype(o_ref.dtype)

        @pl.when(kind == 1)
        def _():
            ms = jnp.mean(a * a, axis=-1, keepdims=True)
            y = a * lax.rsqrt(ms + EPS) * gc_ref[:, c * LANES:(c + 1) * LANES]
            if rope:
                y = (y * cos_ref[...] + pltpu.roll(y, LANES - 32, 1) * sa_ref[...]
                     + pltpu.roll(y, 32, 1) * sb_ref[...])
            o_ref[:, c * LANES:(c + 1) * LANES] = y.astype(o_ref.dtype)


def _in_proj(x2d, norm_g, shift, scale, w_bf16, col_gain, col_kind, tables, *, tm, rows_per_mod, rope):
    rows, d = x2d.shape
    n = w_bf16.shape[1]
    tn = 512
    cos_t, sa_t, sb_t = tables
    pos_tiles = cos_t.shape[0] // tm

    grid_spec = pltpu.PrefetchScalarGridSpec(
        num_scalar_prefetch=1,
        grid=(rows // tm, n // tn),
        in_specs=[
            pl.BlockSpec((tm, d), lambda i, j, k: (i, 0)),
            pl.BlockSpec((1, d), lambda i, j, k: (0, 0)),
            pl.BlockSpec((1, 1, d), lambda i, j, k: ((i * tm) // rows_per_mod, 0, 0)),
            pl.BlockSpec((1, 1, d), lambda i, j, k: ((i * tm) // rows_per_mod, 0, 0)),
            pl.BlockSpec((d, tn), lambda i, j, k: (0, j)),
            pl.BlockSpec((1, tn), lambda i, j, k: (0, j)),
            pl.BlockSpec((tm, LANES), lambda i, j, k: (i % pos_tiles, 0)),
            pl.BlockSpec((tm, LANES), lambda i, j, k: (i % pos_tiles, 0)),
            pl.BlockSpec((tm, LANES), lambda i, j, k: (i % pos_tiles, 0)),
        ],
        out_specs=pl.BlockSpec((tm, tn), lambda i, j, k: (i, j)),
        scratch_shapes=[pltpu.VMEM((tm, d), BF16)],
    )
    return pl.pallas_call(
        functools.partial(_inproj_kernel, rope=rope),
        out_shape=jax.ShapeDtypeStruct((rows, n), BF16),
        grid_spec=grid_spec,
        compiler_params=_params(("arbitrary", "arbitrary"), tm * d * 4, d * tn * 2, tm * tn * 2,
                                3 * tm * LANES * 4, tm * d),
        name="in_proj_rope" if rope else "in_proj_ctx",
    )(col_kind, x2d, norm_g.reshape(1, d), shift, scale, w_bf16, col_gain.reshape(1, n), cos_t, sa_t, sb_t)


def _window_mask(q0, ws, tq, nk):
    qpos = q0 + lax.broadcasted_iota(jnp.int32, (tq, nk), 0)
    kpos = ws + lax.broadcasted_iota(jnp.int32, (tq, nk), 1)
    return jnp.abs(qpos - kpos) <= WINDOW


def _attn_a_kernel(sink_ref, q_ref, kl_ref, vl_ref, kc_ref, vc_ref, o_ref, *, tq, seq_len):
    h = pl.program_id(1)
    sink = sink_ref[h]
    kc = kc_ref[...]
    vc = vc_ref[...]
    bq = q_ref.shape[0]
    nk = tq + 2 * WINDOW

    def body(t, carry):
        r0 = pl.multiple_of(t * tq, tq)
        q = q_ref[pl.ds(r0, tq), :]
        s_ctx = _nt_dot(q, kc)
        m = jnp.maximum(jnp.max(s_ctx, axis=-1, keepdims=True), sink)
        if seq_len:
            q0 = pl.program_id(2) * bq + r0
            ws = pl.multiple_of(jnp.clip(q0 - WINDOW, 0, seq_len - nk), WINDOW)
            s_loc = _nt_dot(q, kl_ref[pl.ds(ws, nk), :])
            s_loc = jnp.where(_window_mask(q0, ws, tq, nk), s_loc, NEG_INF)
            m = jnp.maximum(m, jnp.max(s_loc, axis=-1, keepdims=True))
        p_ctx = jnp.exp(s_ctx - m)
        den = jnp.sum(p_ctx, axis=-1, keepdims=True) + jnp.exp(sink - m)
        o = jnp.dot(p_ctx.astype(BF16), vc, preferred_element_type=F32)
        if seq_len:
            p_loc = jnp.exp(s_loc - m)
            den = den + jnp.sum(p_loc, axis=-1, keepdims=True)
            o = o + jnp.dot(p_loc.astype(BF16), vl_ref[pl.ds(ws, nk), :], preferred_element_type=F32)
        o_ref[pl.ds(r0, tq), :] = (o / den).astype(o_ref.dtype)
        return carry

    lax.fori_loop(0, bq // tq, body, 0)


def _attn_a(z, zc, sink, *, batch, seq_len, n_ctx, q_from_ctx, col_q, col_k, col_v):
    g = A_HEADS // A_KV_HEADS
    if q_from_ctx:
        lq, bq, tq, qsrc = n_ctx, n_ctx, n_ctx, zc
    else:
        lq, bq, tq, qsrc = seq_len, 1024, 256, z
    nqb = lq // bq
    lat_len = 0 if q_from_ctx else seq_len
    lat_block = 16 if q_from_ctx else seq_len
    grid_spec = pltpu.PrefetchScalarGridSpec(
        num_scalar_prefetch=1,
        grid=(batch, A_HEADS, nqb),
        in_specs=[
            pl.BlockSpec((bq, LANES), lambda b, h, i, s: (b * nqb + i, col_q + h)),
            pl.BlockSpec((lat_block, LANES), lambda b, h, i, s: (b * (seq_len // lat_block), col_k + h // g)),
            pl.BlockSpec((lat_block, LANES), lambda b, h, i, s: (b * (seq_len // lat_block), col_v + h // g)),
            pl.BlockSpec((n_ctx, LANES), lambda b, h, i, s: (b, col_k + h // g)),
            pl.BlockSpec((n_ctx, LANES), lambda b, h, i, s: (b, col_v + h // g)),
        ],
        out_specs=pl.BlockSpec((bq, LANES), lambda b, h, i, s: (b * nqb + i, h)),
    )
    return pl.pallas_call(
        functools.partial(_attn_a_kernel, tq=tq, seq_len=lat_len),
        out_shape=jax.ShapeDtypeStruct((batch * lq, A_HEADS * HEAD_DIM), BF16),
        grid_spec=grid_spec,
        compiler_params=_params(("arbitrary",) * 3, 2 * bq * LANES * 2, 2 * lat_block * LANES * 2,
                                2 * n_ctx * LANES * 2),
        name="attn_window_ctx" if q_from_ctx else "attn_window",
    )(sink, qsrc, z, z, zc, zc)


def _softmax_two_parts(q, k_lat, k_ctx):
    s_ctx = _nt_dot(q, k_ctx)
    m = jnp.max(s_ctx, axis=-1, keepdims=True)
    if k_lat is not None:
        s_lat = _nt_dot(q, k_lat)
        m = jnp.maximum(m, jnp.max(s_lat, axis=-1, keepdims=True))
    p_ctx = jnp.exp(s_ctx - m)
    den = jnp.sum(p_ctx, axis=-1, keepdims=True)
    p_lat = None
    if k_lat is not None:
        p_lat = jnp.exp(s_lat - m)
        den = den + jnp.sum(p_lat, axis=-1, keepdims=True)
    return p_lat, p_ctx, den


def _attn_b_kernel(q1_ref, q2_ref, k1l_ref, k2l_ref, vl_ref, k1c_ref, k2c_ref, vc_ref,
                   lq1_ref, lk1_ref, lq2_ref, lk2_ref, sub_ref, o_ref, *, tq, has_lat, lam_init):
    lam = (jnp.exp(jnp.sum(lq1_ref[0] * lk1_ref[0], axis=-1, keepdims=True))
           - jnp.exp(jnp.sum(lq2_ref[0] * lk2_ref[0], axis=-1, keepdims=True)) + lam_init)
    bq = q1_ref.shape[0]
    k1c, k2c, vc = k1c_ref[...], k2c_ref[...], vc_ref[...]

    def body(t, carry):
        r0 = pl.multiple_of(t * tq, tq)
        q1 = q1_ref[pl.ds(r0, tq), :]
        q2 = q2_ref[pl.ds(r0, tq), :]
        p1l, p1c, d1 = _softmax_two_parts(q1, k1l_ref[...] if has_lat else None, k1c)
        p2l, p2c, d2 = _softmax_two_parts(q2, k2l_ref[...] if has_lat else None, k2c)
        w1 = 1.0 / d1
        w2 = lam / d2
        o = jnp.dot((p1c * w1 - p2c * w2).astype(BF16), vc, preferred_element_type=F32)
        if has_lat:
            o = o + jnp.dot((p1l * w1 - p2l * w2).astype(BF16), vl_ref[...], preferred_element_type=F32)
        ms = jnp.mean(o * o, axis=-1, keepdims=True)
        o = o * lax.rsqrt(ms + EPS) * sub_ref[...] * (1.0 - lam_init)
        o_ref[pl.ds(r0, tq), :] = o.astype(o_ref.dtype)
        return carry

    lax.fori_loop(0, bq // tq, body, 0)


def _attn_b(z, zc, lq1, lk1, lq2, lk2, subln, *, batch, seq_len, n_ctx, q_from_ctx, col_q, col_k, col_v,
            lam_init):
    dv = 2 * HEAD_DIM
    if q_from_ctx:
        lq, bq, tq, qsrc = n_ctx, n_ctx, n_ctx, zc
    else:
        lq, bq, tq, qsrc = seq_len, 1024, 256, z
    nqb = lq // bq
    lat_block = 16 if q_from_ctx else seq_len
    lat_tiles = seq_len // lat_block
    lam_spec = pl.BlockSpec((1, 1, HEAD_DIM), lambda b, h, i: (h, 0, 0))
    in_specs = [
        pl.BlockSpec((bq, LANES), lambda b, h, i: (b * nqb + i, col_q + 2 * h)),
        pl.BlockSpec((bq, LANES), lambda b, h, i: (b * nqb + i, col_q + 2 * h + 1)),
        pl.BlockSpec((lat_block, LANES), lambda b, h, i: (b * lat_tiles, col_k + 2 * h)),
        pl.BlockSpec((lat_block, LANES), lambda b, h, i: (b * lat_tiles, col_k + 2 * h + 1)),
        pl.BlockSpec((lat_block, dv), lambda b, h, i: (b * lat_tiles, col_v // 2 + h)),
        pl.BlockSpec((n_ctx, LANES), lambda b, h, i: (b, col_k + 2 * h)),
        pl.BlockSpec((n_ctx, LANES), lambda b, h, i: (b, col_k + 2 * h + 1)),
        pl.BlockSpec((n_ctx, dv), lambda b, h, i: (b, col_v // 2 + h)),
        lam_spec, lam_spec, lam_spec, lam_spec,
        pl.BlockSpec((1, dv), lambda b, h, i: (0, 0)),
    ]
    return pl.pallas_call(
        functools.partial(_attn_b_kernel, tq=tq, has_lat=not q_from_ctx, lam_init=lam_init),
        out_shape=jax.ShapeDtypeStruct((batch * lq, B_HEADS * dv), BF16),
        grid=(batch, B_HEADS, nqb),
        in_specs=in_specs,
        out_specs=pl.BlockSpec((bq, dv), lambda b, h, i: (b * nqb + i, h)),
        compiler_params=_params(("arbitrary",) * 3, 2 * bq * LANES * 2, 4 * lat_block * LANES * 2,
                                4 * n_ctx * LANES * 2, bq * dv * 2, 6 * tq * seq_len * 4 // 2),
        name="attn_diff_ctx" if q_from_ctx else "attn_diff",
    )(qsrc, qsrc, z, z, z, zc, zc, zc,
      lq1.reshape(B_HEADS, 1, HEAD_DIM), lk1.reshape(B_HEADS, 1, HEAD_DIM),
      lq2.reshape(B_HEADS, 1, HEAD_DIM), lk2.reshape(B_HEADS, 1, HEAD_DIM), subln.reshape(1, dv))


def _attn_c_kernel(q_ref, kl_ref, vl_ref, kc_ref, vc_ref, o_ref, *, tq):
    bq = q_ref.shape[0]
    kc, vc = kc_ref[...], vc_ref[...]

    def body(t, carry):
        r0 = pl.multiple_of(t * tq, tq)
        p_lat, p_ctx, den = _softmax_two_parts(q_ref[pl.ds(r0, tq), :], kl_ref[...], kc)
        o = (jnp.dot(p_lat.astype(BF16), vl_ref[...], preferred_element_type=F32)
             + jnp.dot(p_ctx.astype(BF16), vc, preferred_element_type=F32))
        o_ref[pl.ds(r0, tq), :] = (o / den).astype(o_ref.dtype)
        return carry

    lax.fori_loop(0, bq // tq, body, 0)


def _attn_c(z, zc, *, batch, seq_len, n_ctx, n_heads, col_k, col_v):
    g = n_heads // C_KV_HEADS
    bq, tq = 1024, 256
    nqb = seq_len // bq
    return pl.pallas_call(
        functools.partial(_attn_c_kernel, tq=tq),
        out_shape=jax.ShapeDtypeStruct((batch * seq_len, n_heads * HEAD_DIM), BF16),
        grid=(batch, n_heads, nqb),
        in_specs=[
            pl.BlockSpec((bq, LANES), lambda b, h, i: (b * nqb + i, h)),
            pl.BlockSpec((seq_len, LANES), lambda b, h, i: (b, col_k + h // g)),
            pl.BlockSpec((seq_len, LANES), lambda b, h, i: (b, col_v + h // g)),
            pl.BlockSpec((n_ctx, LANES), lambda b, h, i: (b, col_k + h // g)),
            pl.BlockSpec((n_ctx, LANES), lambda b, h, i: (b, col_v + h // g)),
        ],
        out_specs=pl.BlockSpec((bq, LANES), lambda b, h, i: (b * nqb + i, h)),
        compiler_params=_params(("arbitrary",) * 3, 2 * bq * LANES * 2, 2 * seq_len * LANES * 2,
                                2 * n_ctx * LANES * 2, 3 * tq * seq_len * 4 // 2),
        name="attn_gqa",
    )(z, z, z, zc, zc)


def _outproj_kernel(*refs, n_lhs):
    lhs = refs[:n_lhs]
    w_ref, res_ref, gate_ref, o_ref = refs[n_lhs:]
    acc = None
    k0 = 0
    for a_ref in lhs:
        kw = a_ref.shape[1]
        part = jnp.dot(a_ref[...], w_ref[k0:k0 + kw, :], preferred_element_type=F32)
        acc = part if acc is None else acc + part
        k0 += kw
    o_ref[...] = res_ref[...] + gate_ref[0] * acc


def _out_proj(lhs_list, w_bf16, res2d, gate, *, tm, rows_per_mod):
    rows, n = res2d.shape
    kdim = w_bf16.shape[0]
    tn = 512
    in_specs = [pl.BlockSpec((tm, a.shape[1]), lambda i, j: (i, 0)) for a in lhs_list]
    in_specs += [
        pl.BlockSpec((kdim, tn), lambda i, j: (0, j)),
        pl.BlockSpec((tm, tn), lambda i, j: (i, j)),
        pl.BlockSpec((1, 1, tn), lambda i, j: ((i * tm) // rows_per_mod, 0, j)),
    ]
    return pl.pallas_call(
        functools.partial(_outproj_kernel, n_lhs=len(lhs_list)),
        out_shape=jax.ShapeDtypeStruct((rows, n), F32),
        grid=(rows // tm, n // tn),
        in_specs=in_specs,
        out_specs=pl.BlockSpec((tm, tn), lambda i, j: (i, j)),
        compiler_params=_params(("arbitrary", "arbitrary"), tm * kdim * 2, kdim * tn * 2, 2 * tm * tn * 4),
        name="out_proj",
    )(*lhs_list, w_bf16, res2d, gate)


def _ffn_kernel(x_ref, g_ref, sh_ref, sc_ref, gate_ref, wg_ref, wu_ref, wd_ref, o_ref, h_ref, acc_ref):
    f = pl.program_id(1)

    @pl.when(f == 0)
    def _():
        h_ref[...] = _rms_mod(x_ref[...], g_ref[...], sh_ref[0], sc_ref[0]).astype(BF16)
        acc_ref[...] = jnp.zeros_like(acc_ref)

    h = h_ref[...]
    g_ = jnp.dot(h, wg_ref[...], preferred_element_type=F32)
    u_ = jnp.dot(h, wu_ref[...], preferred_element_type=F32)
    a = (g_ * jax.nn.sigmoid(g_) * u_).astype(BF16)
    acc_ref[...] += jnp.dot(a, wd_ref[...], preferred_element_type=F32)

    @pl.when(f == pl.num_programs(1) - 1)
    def _():
        o_ref[...] = x_ref[...] + gate_ref[0] * acc_ref[...]


def _dense_ffn(x2d, norm_g, shift, scale, gate, w_in_bf16, w_down_bf16, *, tm, rows_per_mod):
    rows, d = x2d.shape
    d_ff = w_down_bf16.shape[0]
    tf = 512
    nf = d_ff // tf
    mod_spec = pl.BlockSpec((1, 1, d), lambda i, f: ((i * tm) // rows_per_mod, 0, 0))
    return pl.pallas_call(
        _ffn_kernel,
        out_shape=jax.ShapeDtypeStruct((rows, d), F32),
        grid=(rows // tm, nf),
        in_specs=[
            pl.BlockSpec((tm, d), lambda i, f: (i, 0)),
            pl.BlockSpec((1, d), lambda i, f: (0, 0)),
            mod_spec, mod_spec, mod_spec,
            pl.BlockSpec((d, tf), lambda i, f: (0, f)),
            pl.BlockSpec((d, tf), lambda i, f: (0, nf + f)),
            pl.BlockSpec((tf, d), lambda i, f: (f, 0)),
        ],
        out_specs=pl.BlockSpec((tm, d), lambda i, f: (i, 0)),
        scratch_shapes=[pltpu.VMEM((tm, d), BF16), pltpu.VMEM((tm, d), F32)],
        compiler_params=_params(("arbitrary", "arbitrary"), 2 * tm * d * 4, 3 * d * tf * 2, tm * d * 3),
        name="dense_swiglu",
    )(x2d, norm_g.reshape(1, d), shift, scale, gate, w_in_bf16, w_in_bf16, w_down_bf16)


def _split3(v):
    hi = v.astype(BF16)
    r = v - hi.astype(F32)
    mid = r.astype(BF16)
    lo = (r - mid.astype(F32)).astype(BF16)
    return hi, mid, lo


def _router_kernel(x_ref, g_ref, sh_ref, sc_ref, wr_ref, h_ref, ids_ref, gates_ref, cnt_ref, carry_ref):
    i = pl.program_id(0)
    tm = x_ref.shape[0]

    @pl.when(i == 0)
    def _():
        carry_ref[...] = jnp.zeros_like(carry_ref)

    h = _rms_mod(x_ref[...], g_ref[...], sh_ref[0], sc_ref[0])
    h_ref[...] = h
    h0, h1, h2 = _split3(h)
    w0, w1, w2 = _split3(wr_ref[...])
    dot = functools.partial(jnp.dot, preferred_element_type=F32)
    logits = (dot(h0, w0) + (dot(h0, w1) + dot(h1, w0))
              + (dot(h0, w2) + dot(h1, w1) + dot(h2, w0)))
    lane = lax.broadcasted_iota(jnp.int32, (tm, LANES), 1)
    logits = jnp.where(lane < N_EXPERTS, logits, -jnp.inf)
    v0 = jnp.max(logits, axis=-1, keepdims=True)
    i0 = jnp.min(jnp.where(logits == v0, lane, LANES), axis=-1, keepdims=True)
    rest = jnp.where(lane == i0, -jnp.inf, logits)
    v1 = jnp.max(rest, axis=-1, keepdims=True)
    i1 = jnp.min(jnp.where(rest == v1, lane, LANES), axis=-1, keepdims=True)
    e1 = jnp.exp(v1 - v0)
    g0 = 1.0 / (1.0 + e1)
    g1 = e1 / (1.0 + e1)

    sel = (lane == i0) | (lane == i1)
    row = lax.broadcasted_iota(jnp.int32, (tm, tm), 0)
    col = lax.broadcasted_iota(jnp.int32, (tm, tm), 1)
    tri = (col < row).astype(BF16)
    before = jnp.dot(tri, sel.astype(BF16), preferred_element_type=F32) + carry_ref[...]
    r0 = jnp.sum(jnp.where(lane == i0, before, 0.0), axis=-1, keepdims=True).astype(jnp.int32)
    r1 = jnp.sum(jnp.where(lane == i1, before, 0.0), axis=-1, keepdims=True).astype(jnp.int32)
    total = carry_ref[...] + jnp.sum(sel.astype(F32), axis=0, keepdims=True)
    carry_ref[...] = total

    ids_ref[...] = jnp.where(lane == 0, i0, jnp.where(lane == 1, i1, jnp.where(lane == 2, r0, r1)))
    gates_ref[...] = jnp.where(lane == 0, g0, g1)
    cnt_ref[...] = jnp.broadcast_to(total, cnt_ref.shape).astype(jnp.int32)


def _router(x2d, norm_g, shift, scale, w_router, *, tm, rows_per_mod):
    rows, d = x2d.shape
    wr = jnp.zeros((d, LANES), F32).at[:, :N_EXPERTS].set(w_router)
    mod_spec = pl.BlockSpec((1, 1, d), lambda i: ((i * tm) // rows_per_mod, 0, 0))
    return pl.pallas_call(
        _router_kernel,
        out_shape=(jax.ShapeDtypeStruct((rows, d), F32),
                   jax.ShapeDtypeStruct((rows, LANES), jnp.int32),
                   jax.ShapeDtypeStruct((rows, LANES), F32),
                   jax.ShapeDtypeStruct((8, LANES), jnp.int32)),
        grid=(rows // tm,),
        in_specs=[pl.BlockSpec((tm, d), lambda i: (i, 0)),
                  pl.BlockSpec((1, d), lambda i: (0, 0)),
                  mod_spec, mod_spec,
                  pl.BlockSpec((d, LANES), lambda i: (0, 0))],
        out_specs=(pl.BlockSpec((tm, d), lambda i: (i, 0)),
                   pl.BlockSpec((tm, LANES), lambda i: (i, 0)),
                   pl.BlockSpec((tm, LANES), lambda i: (i, 0)),
                   pl.BlockSpec((8, LANES), lambda i: (0, 0))),
        scratch_shapes=[pltpu.VMEM((1, LANES), F32)],
        compiler_params=_params(("arbitrary",), 2 * tm * d * 4, d * LANES * 4, 2 * tm * LANES * 4),
        name="router",
    )(x2d, norm_g.reshape(1, d), shift, scale, wr)


def _row_copy(src_hbm, row, dst_ref, r, sem):
    return pltpu.make_async_copy(src_hbm.at[pl.ds(row, 1), :], dst_ref.at[pl.ds(r, 1), :], sem)


def _dispatch_kernel(idx_ref, src_hbm, o_ref, buf_ref, sem):
    tr = buf_ref.shape[0]
    base = pl.program_id(0) * tr

    def start(r, c):
        _row_copy(src_hbm, idx_ref[base + r], buf_ref, r, sem).start()
        return c

    def wait(r, c):
        _row_copy(src_hbm, 0, buf_ref, r, sem).wait()
        return c

    lax.fori_loop(0, tr, start, 0)
    lax.fori_loop(0, tr, wait, 0)
    o_ref[...] = buf_ref[...].astype(o_ref.dtype)


def _dispatch(h2d, row_tok, *, tr):
    n_rows = row_tok.shape[0]
    d = h2d.shape[1]
    grid_spec = pltpu.PrefetchScalarGridSpec(
        num_scalar_prefetch=1,
        grid=(n_rows // tr,),
        in_specs=[pl.BlockSpec(memory_space=pl.ANY)],
        out_specs=pl.BlockSpec((tr, d), lambda i, idx: (i, 0)),
        scratch_shapes=[pltpu.VMEM((tr, d), F32), pltpu.SemaphoreType.DMA(())],
    )
    return pl.pallas_call(
        _dispatch_kernel,
        out_shape=jax.ShapeDtypeStruct((n_rows, d), BF16),
        grid_spec=grid_spec,
        compiler_params=_params(("arbitrary",), tr * d * 2, tr * d * 2),
        name="expert_dispatch",
    )(row_tok, h2d)


def _combine_kernel(d0_ref, d1_ref, ys_hbm, x_ref, gate_ref, rg_ref, o_ref, a_ref, b_ref, sem):
    tt = x_ref.shape[0]
    base = pl.program_id(0) * tt

    def start(r, c):
        _row_copy(ys_hbm, d0_ref[base + r], a_ref, r, sem.at[0]).start()
        _row_copy(ys_hbm, d1_ref[base + r], b_ref, r, sem.at[1]).start()
        return c

    def wait(r, c):
        _row_copy(ys_hbm, 0, a_ref, r, sem.at[0]).wait()
        _row_copy(ys_hbm, 0, b_ref, r, sem.at[1]).wait()
        return c

    lax.fori_loop(0, tt, start, 0)
    lax.fori_loop(0, tt, wait, 0)
    rg = rg_ref[...]
    mix = rg[:, 0:1] * a_ref[...] + rg[:, 1:2] * b_ref[...]
    o_ref[...] = x_ref[...] + gate_ref[0] * mix


def _combine(ys, dest0, dest1, x2d, gate, row_gates, *, tt, rows_per_mod):
    rows, d = x2d.shape
    grid_spec = pltpu.PrefetchScalarGridSpec(
        num_scalar_prefetch=2,
        grid=(rows // tt,),
        in_specs=[pl.BlockSpec(memory_space=pl.ANY),
                  pl.BlockSpec((tt, d), lambda i, a, b: (i, 0)),
                  pl.BlockSpec((1, 1, d), lambda i, a, b: ((i * tt) // rows_per_mod, 0, 0)),
                  pl.BlockSpec((tt, LANES), lambda i, a, b: (i, 0))],
        out_specs=pl.BlockSpec((tt, d), lambda i, a, b: (i, 0)),
        scratch_shapes=[pltpu.VMEM((tt, d), F32), pltpu.VMEM((tt, d), F32), pltpu.SemaphoreType.DMA((2,))],
    )
    return pl.pallas_call(
        _combine_kernel,
        out_shape=jax.ShapeDtypeStruct((rows, d), F32),
        grid_spec=grid_spec,
        compiler_params=_params(("arbitrary",), 2 * tt * d * 4, 2 * tt * d * 2),
        name="expert_combine",
    )(dest0, dest1, ys, x2d, gate, row_gates)


def _gmm_up_kernel(te_ref, nv_ref, x_ref, wg_ref, wu_ref, o_ref):
    valid = pl.program_id(1) < nv_ref[0]

    @pl.when(valid)
    def _():
        x = x_ref[...]
        g_ = jnp.dot(x, wg_ref[...], preferred_element_type=F32)
        u_ = jnp.dot(x, wu_ref[...], preferred_element_type=F32)
        o_ref[...] = (g_ * jax.nn.sigmoid(g_) * u_).astype(o_ref.dtype)

    @pl.when(jnp.logical_not(valid))
    def _():
        o_ref[...] = jnp.zeros_like(o_ref)


def _gmm_down_kernel(te_ref, nv_ref, h_ref, wd_ref, o_ref):
    valid = pl.program_id(1) < nv_ref[0]

    @pl.when(valid)
    def _():
        o_ref[...] = jnp.dot(h_ref[...], wd_ref[...], preferred_element_type=F32)

    @pl.when(jnp.logical_not(valid))
    def _():
        o_ref[...] = jnp.zeros_like(o_ref)


def _expert_ffn(xs, w_in_bf16, w_down_bf16, tile_expert, n_valid, *, tr):
    n_rows, d = xs.shape
    d_ff = w_down_bf16.shape[1]
    n_tiles = n_rows // tr
    tn1 = 1024
    nj1 = d_ff // tn1

    def row(r, nv):
        return jnp.minimum(r, nv[0] - 1)

    up_spec = pltpu.PrefetchScalarGridSpec(
        num_scalar_prefetch=2,
        grid=(nj1, n_tiles),
        in_specs=[
            pl.BlockSpec((tr, d), lambda j, r, te, nv: (row(r, nv), 0)),
            pl.BlockSpec((None, d, tn1), lambda j, r, te, nv: (te[row(r, nv)], 0, j)),
            pl.BlockSpec((None, d, tn1), lambda j, r, te, nv: (te[row(r, nv)], 0, nj1 + j)),
        ],
        out_specs=pl.BlockSpec((tr, tn1), lambda j, r, te, nv: (r, j)),
    )
    hidden = pl.pallas_call(
        _gmm_up_kernel,
        out_shape=jax.ShapeDtypeStruct((n_rows, d_ff), BF16),
        grid_spec=up_spec,
        compiler_params=_params(("arbitrary", "arbitrary"), tr * d * 2, 2 * d * tn1 * 2, tr * tn1 * 2),
        name="expert_up",
    )(tile_expert, n_valid, xs, w_in_bf16, w_in_bf16)

    tn2 = 512
    down_spec = pltpu.PrefetchScalarGridSpec(
        num_scalar_prefetch=2,
        grid=(d // tn2, n_tiles),
        in_specs=[
            pl.BlockSpec((tr, d_ff), lambda j, r, te, nv: (row(r, nv), 0)),
            pl.BlockSpec((None, d_ff, tn2), lambda j, r, te, nv: (te[row(r, nv)], 0, j)),
        ],
        out_specs=pl.BlockSpec((tr, tn2), lambda j, r, te, nv: (r, j)),
    )
    return pl.pallas_call(
        _gmm_down_kernel,
        out_shape=jax.ShapeDtypeStruct((n_rows, d), F32),
        grid_spec=down_spec,
        compiler_params=_params(("arbitrary", "arbitrary"), tr * d_ff * 2, d_ff * tn2 * 2, tr * tn2 * 4),
        name="expert_down",
    )(tile_expert, n_valid, hidden, w_down_bf16)


def _moe(x2d, norm_g, shift, scale, gate, w_router, w_in_bf16, w_down_bf16, *, rows_per_mod):
    rows, d = x2d.shape
    tr = 256
    h2d, ids, gates, counts = _router(x2d, norm_g, shift, scale, w_router, tm=512, rows_per_mod=rows_per_mod)
    counts = counts[0, :N_EXPERTS]
    padded = (counts + tr - 1) // tr * tr
    pad_end = jnp.cumsum(padded)
    pad_start = pad_end - padded
    dest0 = pad_start[ids[:, 0]] + ids[:, 2]
    dest1 = pad_start[ids[:, 1]] + ids[:, 3]
    n_rows = 2 * rows + N_EXPERTS * tr
    n_tiles = n_rows // tr
    tok = jnp.arange(rows, dtype=jnp.int32)
    row_tok = jnp.zeros((n_rows,), jnp.int32).at[dest0].set(tok).at[dest1].set(tok)
    tile_expert = jnp.minimum(
        jnp.sum(jnp.arange(n_tiles, dtype=jnp.int32)[:, None] * tr >= pad_end[None, :], axis=1),
        N_EXPERTS - 1).astype(jnp.int32)
    n_valid = (pad_end[-1:] // tr).astype(jnp.int32)

    xs = _dispatch(h2d, row_tok, tr=tr)
    ys = _expert_ffn(xs, w_in_bf16, w_down_bf16, tile_expert, n_valid, tr=tr)
    return _combine(ys, dest0, dest1, x2d, gate, gates, tt=256, rows_per_mod=rows_per_mod)


def _rope_tables(seq_len):
    t = jnp.arange(seq_len, dtype=jnp.int32)
    row = (t // GRID_W).astype(F32)[:, None]
    col = (t % GRID_W).astype(F32)[:, None]
    inv = ROPE_THETA ** (-jnp.arange(0, ROT_AXIS, 2, dtype=F32) / ROT_AXIS)
    ar, ac = row * inv, col * inv
    zero = jnp.zeros_like(ar)
    cos = jnp.concatenate([jnp.cos(ar), jnp.cos(ar), jnp.cos(ac), jnp.cos(ac)], axis=-1)
    s_next = jnp.concatenate([-jnp.sin(ar), zero, -jnp.sin(ac), zero], axis=-1)
    s_prev = jnp.concatenate([zero, jnp.sin(ar), zero, jnp.sin(ac)], axis=-1)
    return cos, s_next, s_prev


def _mod_vectors(c, c_ctx, w_mod, b_mod):
    batch, d = c.shape
    cvec = jnp.zeros((8, d), F32).at[:batch].set(c).at[batch].set(c_ctx)
    m = _modulation(jnp.concatenate([cvec, cvec], axis=0), w_mod, b_mod)
    lat = m[:batch].reshape(batch, N_MOD, 1, d)
    ctx = m[batch].reshape(N_MOD, 1, 1, d)
    return [lat[:, k] for k in range(N_MOD)], [ctx[k] for k in range(N_MOD)]


def kernel(x, c, ctx, c_ctx, e_norm1, e_norm2, e_w_mod, e_b_mod, e_w_in, e_w_out, e_a_qnorm, e_a_knorm,
           e_a_sink, e_b_qnorm, e_b_knorm, e_b_lam_q1, e_b_lam_k1, e_b_lam_q2, e_b_lam_k2, e_b_subln,
           e_ffn_w_in, e_ffn_w_down, o_norm1, o_norm2, o_w_mod, o_b_mod, o_w_in, o_w_out, o_c_qnorm,
           o_c_knorm, o_router, o_exp_w_in, o_exp_w_down):
    batch, seq_len, d = x.shape
    n_ctx = ctx.shape[1]
    qk_scale = HEAD_DIM ** -0.5
    tables = _rope_tables(seq_len)
    ctx_tables = tuple(t[:n_ctx] for t in tables)
    x2d = x.reshape(batch * seq_len, d)
    xc2d = ctx.reshape(batch * n_ctx, d)
    tm = 1024

    (sh1, sc1, g1, sh2, sc2, g2), (csh1, csc1, cg1, csh2, csc2, cg2) = _mod_vectors(
        c, c_ctx, e_w_mod[0], e_b_mod[0])
    a_q, a_kv = A_HEADS * HEAD_DIM, A_KV_HEADS * HEAD_DIM
    b_qk, b_v = B_HEADS * 2 * HEAD_DIM, B_HEADS * 2 * HEAD_DIM
    ones = lambda n: jnp.ones((n,), F32)
    tile = lambda v, n: jnp.tile(v, n // HEAD_DIM)
    col_gain = jnp.concatenate([tile(e_a_qnorm[0] * qk_scale, a_q), tile(e_a_knorm[0], a_kv), ones(a_kv),
                                tile(e_b_qnorm[0] * qk_scale, b_qk), tile(e_b_knorm[0], b_qk), ones(b_v)])
    kinds = lambda *pairs: jnp.concatenate([jnp.full((n // HEAD_DIM,), k, jnp.int32) for k, n in pairs])
    col_kind = kinds((1, a_q), (1, a_kv), (0, a_kv), (1, b_qk), (1, b_qk), (0, b_v))
    w_in0 = e_w_in[0].astype(BF16)
    z = _in_proj(x2d, e_norm1[0], sh1, sc1, w_in0, col_gain, col_kind, tables,
                 tm=tm, rows_per_mod=seq_len, rope=True)
    zc = _in_proj(xc2d, e_norm1[0], csh1, csc1, w_in0, col_gain, col_kind, ctx_tables,
                  tm=n_ctx, rows_per_mod=batch * n_ctx, rope=False)

    ca_q, ca_k, ca_v = 0, a_q // LANES, (a_q + a_kv) // LANES
    cb_q = (a_q + 2 * a_kv) // LANES
    cb_k, cb_v = cb_q + b_qk // LANES, cb_q + 2 * b_qk // LANES
    lam_init = 0.8 - 0.6 * math.exp(-0.3 * 0)
    dims = dict(batch=batch, seq_len=seq_len, n_ctx=n_ctx)
    attn_a = functools.partial(_attn_a, z, zc, e_a_sink[0], col_q=ca_q, col_k=ca_k, col_v=ca_v, **dims)
    attn_b = functools.partial(_attn_b, z, zc, e_b_lam_q1[0], e_b_lam_k1[0], e_b_lam_q2[0], e_b_lam_k2[0],
                               e_b_subln[0], col_q=cb_q, col_k=cb_k, col_v=cb_v, lam_init=lam_init, **dims)
    w_out0 = e_w_out[0].astype(BF16)
    x2d = _out_proj([attn_a(q_from_ctx=False), attn_b(q_from_ctx=False)], w_out0, x2d, g1,
                    tm=tm, rows_per_mod=seq_len)
    xc2d = _out_proj([attn_a(q_from_ctx=True), attn_b(q_from_ctx=True)], w_out0, xc2d, cg1,
                     tm=n_ctx, rows_per_mod=batch * n_ctx)
    ffn_in, ffn_down = e_ffn_w_in[0].astype(BF16), e_ffn_w_down[0].astype(BF16)
    x2d = _dense_ffn(x2d, e_norm2[0], sh2, sc2, g2, ffn_in, ffn_down, tm=512, rows_per_mod=seq_len)
    xc2d = _dense_ffn(xc2d, e_norm2[0], csh2, csc2, cg2, ffn_in, ffn_down, tm=n_ctx,
                      rows_per_mod=batch * n_ctx)

    (sh1, sc1, g1, sh2, sc2, g2), (csh1, csc1, _, _, _, _) = _mod_vectors(c, c_ctx, o_w_mod[0], o_b_mod[0])
    c_q = d
    c_kv = C_KV_HEADS * HEAD_DIM
    col_gain = jnp.concatenate([tile(o_c_qnorm[0] * qk_scale, c_q), tile(o_c_knorm[0], c_kv), ones(c_kv)])
    col_kind = kinds((1, c_q), (1, c_kv), (0, c_kv))
    w_in1 = o_w_in[0].astype(BF16)
    z = _in_proj(x2d, o_norm1[0], sh1, sc1, w_in1, col_gain, col_kind, tables,
                 tm=tm, rows_per_mod=seq_len, rope=True)
    zc = _in_proj(xc2d, o_norm1[0], csh1, csc1, w_in1, col_gain, col_kind, ctx_tables,
                  tm=n_ctx, rows_per_mod=batch * n_ctx, rope=False)
    o = _attn_c(z, zc, n_heads=c_q // HEAD_DIM, col_k=c_q // LANES, col_v=(c_q + c_kv) // LANES, **dims)
    x2d = _out_proj([o], o_w_out[0].astype(BF16), x2d, g1, tm=tm, rows_per_mod=seq_len)
    x2d = _moe(x2d, o_norm2[0], sh2, sc2, g2, o_router[0], o_exp_w_in[0].astype(BF16),
               o_exp_w_down[0].astype(BF16), rows_per_mod=seq_len)
    return x2d.reshape(batch, seq_len, d)
```

```python
import functools
import math

import jax
import jax.numpy as jnp
from jax import lax
from jax.experimental import pallas as pl
from jax.experimental.pallas import tpu as pltpu

F32 = jnp.float32
BF16 = jnp.bfloat16

HEAD_DIM = 128
GRID_W = 64
ROT_AXIS = HEAD_DIM // 2
ROPE_THETA = 10000.0
EPS = 1e-6
NEG_INF = -1e30
LOG2_E = math.log2(math.e)
N_MOD = 6
WINDOW = 128
A_HEADS, A_KV_HEADS = 8, 2
B_HEADS = 4
C_KV_HEADS = 4
N_EXPERTS = 8
LANES = 128
V7X_VMEM_BYTES = 64 * 1024 * 1024
VMEM_TEMP_ALLOWANCE = 16 * 1024 * 1024


def _vmem_limit(*block_bytes):
    need = 2 * sum(block_bytes) + VMEM_TEMP_ALLOWANCE
    return int(min(need, V7X_VMEM_BYTES - 6 * 1024 * 1024))


def _params(sem, *block_bytes):
    return pltpu.CompilerParams(dimension_semantics=sem, vmem_limit_bytes=_vmem_limit(*block_bytes))


def _nt_dot(a, b):
    return lax.dot_general(a, b, (((1,), (1,)), ((), ())), preferred_element_type=F32)


def _rms_mod(x, g, shift, scale):
    ms = jnp.mean(x * x, axis=-1, keepdims=True)
    y = x * lax.rsqrt(ms + EPS) * g
    return y * (1.0 + scale) + shift


def _mod_kernel(c_ref, w_ref, b_ref, o_ref):
    c = c_ref[...]
    s = c * jax.nn.sigmoid(c)
    s_hi = s.astype(BF16).astype(F32)
    top = lax.broadcasted_iota(jnp.int32, s.shape, 0) < 8
    lhs = jnp.where(top, s_hi, s - s_hi).astype(BF16)
    acc = jnp.dot(lhs, w_ref[...].astype(BF16), preferred_element_type=F32)
    o_ref[...] = acc[0:8] + acc[8:16] + b_ref[...]


def _modulation(cvec, w_mod, b_mod):
    d, n = w_mod.shape
    tn = 1024
    return pl.pallas_call(
        _mod_kernel,
        out_shape=jax.ShapeDtypeStruct((8, n), F32),
        grid=(n // tn,),
        in_specs=[pl.BlockSpec((16, d), lambda j: (0, 0)),
                  pl.BlockSpec((d, tn), lambda j: (0, j)),
                  pl.BlockSpec((1, tn), lambda j: (0, j))],
        out_specs=pl.BlockSpec((8, tn), lambda j: (0, j)),
        compiler_params=_params(("arbitrary",), d * tn * 4, d * tn * 2),
        name="modulation",
    )(cvec, w_mod, b_mod.reshape(1, n))


def _inproj_kernel(kind_ref, x_ref, g_ref, sh_ref, sc_ref, w_ref, gc_ref, cos_ref, sa_ref, sb_ref,
                   o_ref, h_ref, *, rope):
    j = pl.program_id(1)
    tn = o_ref.shape[1]
    nch = tn // LANES

    @pl.when(j == 0)
    def _():
        h_ref[...] = _rms_mod(x_ref[...], g_ref[...], sh_ref[0], sc_ref[0]).astype(BF16)

    acc = jnp.dot(h_ref[...], w_ref[...], preferred_element_type=F32)
    for c in range(nch):
        a = acc[:, c * LANES:(c + 1) * LANES]
        kind = kind_ref[j * nch + c]

        @pl.when(kind == 0)
        def _():
            o_ref[:, c * LANES:(c + 1) * LANES] = a.astype(o_ref.dtype)

        @pl.when(kind == 1)
        def _():
            ms = jnp.mean(a * a, axis=-1, keepdims=True)
            y = a * lax.rsqrt(ms + EPS) * gc_ref[:, c * LANES:(c + 1) * LANES]
            if rope:
                y = (y * cos_ref[...] + pltpu.roll(y, LANES - 32, 1) * sa_ref[...]
                     + pltpu.roll(y, 32, 1) * sb_ref[...])
            o_ref[:, c * LANES:(c + 1) * LANES] = y.astype(o_ref.dtype)


def _in_proj(x2d, norm_g, shift, scale, w_bf16, col_gain, col_kind, tables, *, tm, rows_per_mod, rope):
    rows, d = x2d.shape
    n = w_bf16.shape[1]
    tn = 512
    cos_t, sa_t, sb_t = tables
    pos_tiles = cos_t.shape[0] // tm

    grid_spec = pltpu.PrefetchScalarGridSpec(
        num_scalar_prefetch=1,
        grid=(rows // tm, n // tn),
        in_specs=[
            pl.BlockSpec((tm, d), lambda i, j, k: (i, 0)),
            pl.BlockSpec((1, d), lambda i, j, k: (0, 0)),
            pl.BlockSpec((1, 1, d), lambda i, j, k: ((i * tm) // rows_per_mod, 0, 0)),
            pl.BlockSpec((1, 1, d), lambda i, j, k: ((i * tm) // rows_per_mod, 0, 0)),
            pl.BlockSpec((d, tn), lambda i, j, k: (0, j)),
            pl.BlockSpec((1, tn), lambda i, j, k: (0, j)),
            pl.BlockSpec((tm, LANES), lambda i, j, k: (i % pos_tiles, 0)),
            pl.BlockSpec((tm, LANES), lambda i, j, k: (i % pos_tiles, 0)),
            pl.BlockSpec((tm, LANES), lambda i, j, k: (i % pos_tiles, 0)),
        ],
        out_specs=pl.BlockSpec((tm, tn), lambda i, j, k: (i, j)),
        scratch_shapes=[pltpu.VMEM((tm, d), BF16)],
    )
    return pl.pallas_call(
        functools.partial(_inproj_kernel, rope=rope),
        out_shape=jax.ShapeDtypeStruct((rows, n), BF16),
        grid_spec=grid_spec,
        compiler_params=_params(("arbitrary", "arbitrary"), tm * d * 4, d * tn * 2, tm * tn * 2,
                                3 * tm * LANES * 4, tm * d),
        name="in_proj_rope" if rope else "in_proj_ctx",
    )(col_kind, x2d, norm_g.reshape(1, d), shift, scale, w_bf16, col_gain.reshape(1, n), cos_t, sa_t, sb_t)


def _window_mask(q0, ws, tq, nk):
    qpos = q0 + lax.broadcasted_iota(jnp.int32, (tq, nk), 0)
    kpos = ws + lax.broadcasted_iota(jnp.int32, (tq, nk), 1)
    return jnp.abs(qpos - kpos) <= WINDOW


def _attn_a_kernel(sink_ref, q_ref, kl_ref, vl_ref, kc_ref, vc_ref, o_ref, *, tq, seq_len):
    h = pl.program_id(1)
    sink = sink_ref[h] * LOG2_E
    kc = kc_ref[...]
    vc = vc_ref[...]
    bq = q_ref.shape[0]
    nk = tq + 2 * WINDOW

    def body(t, carry):
        r0 = pl.multiple_of(t * tq, tq)
        q = q_ref[pl.ds(r0, tq), :]
        s_ctx = _nt_dot(q, kc)
        m = jnp.maximum(jnp.max(s_ctx, axis=-1, keepdims=True), sink)
        if seq_len:
            q0 = pl.program_id(2) * bq + r0
            ws = pl.multiple_of(jnp.clip(q0 - WINDOW, 0, seq_len - nk), WINDOW)
            s_loc = _nt_dot(q, kl_ref[pl.ds(ws, nk), :])
            s_loc = jnp.where(_window_mask(q0, ws, tq, nk), s_loc, NEG_INF)
            m = jnp.maximum(m, jnp.max(s_loc, axis=-1, keepdims=True))
        p_ctx = jnp.exp2(s_ctx - m)
        den = jnp.sum(p_ctx, axis=-1, keepdims=True) + jnp.exp2(sink - m)
        o = jnp.dot(p_ctx.astype(BF16), vc, preferred_element_type=F32)
        if seq_len:
            p_loc = jnp.exp2(s_loc - m)
            den = den + jnp.sum(p_loc, axis=-1, keepdims=True)
            o = o + jnp.dot(p_loc.astype(BF16), vl_ref[pl.ds(ws, nk), :], preferred_element_type=F32)
        o_ref[pl.ds(r0, tq), :] = (o / den).astype(o_ref.dtype)
        return carry

    lax.fori_loop(0, bq // tq, body, 0)


def _attn_a(z, zc, sink, *, batch, seq_len, n_ctx, q_from_ctx, col_q, col_k, col_v):
    g = A_HEADS // A_KV_HEADS
    if q_from_ctx:
        lq, bq, tq, qsrc = n_ctx, n_ctx, n_ctx, zc
    else:
        lq, bq, tq, qsrc = seq_len, 1024, 256, z
    nqb = lq // bq
    lat_len = 0 if q_from_ctx else seq_len
    lat_block = 16 if q_from_ctx else seq_len
    grid_spec = pltpu.PrefetchScalarGridSpec(
        num_scalar_prefetch=1,
        grid=(batch, A_HEADS, nqb),
        in_specs=[
            pl.BlockSpec((bq, LANES), lambda b, h, i, s: (b * nqb + i, col_q + h)),
            pl.BlockSpec((lat_block, LANES), lambda b, h, i, s: (b * (seq_len // lat_block), col_k + h // g)),
            pl.BlockSpec((lat_block, LANES), lambda b, h, i, s: (b * (seq_len // lat_block), col_v + h // g)),
            pl.BlockSpec((n_ctx, LANES), lambda b, h, i, s: (b, col_k + h // g)),
            pl.BlockSpec((n_ctx, LANES), lambda b, h, i, s: (b, col_v + h // g)),
        ],
        out_specs=pl.BlockSpec((bq, LANES), lambda b, h, i, s: (b * nqb + i, h)),
    )
    return pl.pallas_call(
        functools.partial(_attn_a_kernel, tq=tq, seq_len=lat_len),
        out_shape=jax.ShapeDtypeStruct((batch * lq, A_HEADS * HEAD_DIM), BF16),
        grid_spec=grid_spec,
        compiler_params=_params(("arbitrary",) * 3, 2 * bq * LANES * 2, 2 * lat_block * LANES * 2,
                                2 * n_ctx * LANES * 2),
        name="attn_window_ctx" if q_from_ctx else "attn_window",
    )(sink, qsrc, z, z, zc, zc)


def _pipeline_blocks(n_blocks, depth, score_stage, value_stage):
    score_stage(0, 0)

    def body(u, carry):
        for i in range(depth):
            b = u * depth + i
            score_stage(jnp.minimum(b + 1, n_blocks - 1), (i + 1) % depth)
            value_stage(b, i)
        return carry

    lax.fori_loop(0, n_blocks // depth, body, 0)


def _join_rows(dst_ref, lat_ref, ctx_ref, has_lat):
    n_lat = lat_ref.shape[0] if has_lat else 0
    if has_lat:
        dst_ref[0:n_lat, 0:lat_ref.shape[1]] = lat_ref[...]
    dst_ref[n_lat:n_lat + ctx_ref.shape[0], 0:ctx_ref.shape[1]] = ctx_ref[...]


def _exp2_numerators(q, k_all):
    s = _nt_dot(q, k_all)
    return jnp.exp2(s - jnp.max(s, axis=-1, keepdims=True))


def _attn_b_kernel(q1_ref, q2_ref, k1l_ref, k2l_ref, vl_ref, k1c_ref, k2c_ref, vc_ref,
                   lq1_ref, lk1_ref, lq2_ref, lk2_ref, sub_ref, o_ref, k1_ref, k2_ref, v_ref, *p_slots,
                   tq, has_lat, lam_init):
    lam = (jnp.exp(jnp.sum(lq1_ref[0] * lk1_ref[0], axis=-1, keepdims=True))
           - jnp.exp(jnp.sum(lq2_ref[0] * lk2_ref[0], axis=-1, keepdims=True)) + lam_init)
    _join_rows(k1_ref, k1l_ref, k1c_ref, has_lat)
    _join_rows(k2_ref, k2l_ref, k2c_ref, has_lat)
    _join_rows(v_ref, vl_ref, vc_ref, has_lat)

    def score_stage(b, slot):
        r0 = pl.multiple_of(b * tq, tq)
        p1 = _exp2_numerators(q1_ref[pl.ds(r0, tq), :], k1_ref[...])
        p2 = _exp2_numerators(q2_ref[pl.ds(r0, tq), :], k2_ref[...])
        w1 = 1.0 / jnp.sum(p1, axis=-1, keepdims=True)
        w2 = lam / jnp.sum(p2, axis=-1, keepdims=True)
        p_slots[slot][...] = (p1 * w1 - p2 * w2).astype(BF16)

    def value_stage(b, slot):
        r0 = pl.multiple_of(b * tq, tq)
        o = jnp.dot(p_slots[slot][...], v_ref[...], preferred_element_type=F32)
        ms = jnp.mean(o * o, axis=-1, keepdims=True)
        o = o * lax.rsqrt(ms + EPS) * sub_ref[...] * (1.0 - lam_init)
        o_ref[pl.ds(r0, tq), :] = o.astype(o_ref.dtype)

    _pipeline_blocks(q1_ref.shape[0] // tq, len(p_slots), score_stage, value_stage)


def _attn_b(z, zc, lq1, lk1, lq2, lk2, subln, *, batch, seq_len, n_ctx, q_from_ctx, col_q, col_k, col_v,
            lam_init):
    dv = 2 * HEAD_DIM
    if q_from_ctx:
        lq, bq, tq, qsrc, depth = n_ctx, n_ctx, n_ctx, zc, 1
    else:
        lq, bq, tq, qsrc, depth = seq_len, seq_len, 256, z, 4
    nqb = lq // bq
    lat_block = 16 if q_from_ctx else seq_len
    lat_tiles = seq_len // lat_block
    n_keys = n_ctx if q_from_ctx else seq_len + n_ctx
    lam_spec = pl.BlockSpec((1, 1, HEAD_DIM), lambda b, h, i: (h, 0, 0))
    in_specs = [
        pl.BlockSpec((bq, LANES), lambda b, h, i: (b * nqb + i, col_q + 2 * h)),
        pl.BlockSpec((bq, LANES), lambda b, h, i: (b * nqb + i, col_q + 2 * h + 1)),
        pl.BlockSpec((lat_block, LANES), lambda b, h, i: (b * lat_tiles, col_k + 2 * h)),
        pl.BlockSpec((lat_block, LANES), lambda b, h, i: (b * lat_tiles, col_k + 2 * h + 1)),
        pl.BlockSpec((lat_block, dv), lambda b, h, i: (b * lat_tiles, col_v // 2 + h)),
        pl.BlockSpec((n_ctx, LANES), lambda b, h, i: (b, col_k + 2 * h)),
        pl.BlockSpec((n_ctx, LANES), lambda b, h, i: (b, col_k + 2 * h + 1)),
        pl.BlockSpec((n_ctx, dv), lambda b, h, i: (b, col_v // 2 + h)),
        lam_spec, lam_spec, lam_spec, lam_spec,
        pl.BlockSpec((1, dv), lambda b, h, i: (0, 0)),
    ]
    return pl.pallas_call(
        functools.partial(_attn_b_kernel, tq=tq, has_lat=not q_from_ctx, lam_init=lam_init),
        out_shape=jax.ShapeDtypeStruct((batch * lq, B_HEADS * dv), BF16),
        grid=(batch, B_HEADS, nqb),
        in_specs=in_specs,
        out_specs=pl.BlockSpec((bq, dv), lambda b, h, i: (b * nqb + i, h)),
        scratch_shapes=[pltpu.VMEM((n_keys, LANES), BF16), pltpu.VMEM((n_keys, LANES), BF16),
                        pltpu.VMEM((n_keys, dv), BF16)] + [pltpu.VMEM((tq, n_keys), BF16)] * depth,
        compiler_params=_params(("arbitrary",) * 3, 2 * bq * LANES * 2, 4 * lat_block * LANES * 2,
                                4 * n_ctx * LANES * 2, bq * dv * 2, (4 * LANES + depth * tq) * n_keys,
                                4 * tq * n_keys * 4 // 2),
        name="attn_diff_ctx" if q_from_ctx else "attn_diff",
    )(qsrc, qsrc, z, z, z, zc, zc, zc,
      lq1.reshape(B_HEADS, 1, HEAD_DIM), lk1.reshape(B_HEADS, 1, HEAD_DIM),
      lq2.reshape(B_HEADS, 1, HEAD_DIM), lk2.reshape(B_HEADS, 1, HEAD_DIM), subln.reshape(1, dv))


def _attn_c_kernel(q_ref, kl_ref, vl_ref, kc_ref, vc_ref, o_ref, k_ref, v1_ref, *p_slots, tq, group):
    dh = vl_ref.shape[1]

    @pl.when(pl.program_id(1) % group == 0)
    def _():
        _join_rows(k_ref, kl_ref, kc_ref, True)
        _join_rows(v1_ref, vl_ref, vc_ref, True)
        v1_ref[:, dh:] = jnp.ones((v1_ref.shape[0], v1_ref.shape[1] - dh), v1_ref.dtype)

    def score_stage(b, slot):
        r0 = pl.multiple_of(b * tq, tq)
        p_slots[slot][...] = _exp2_numerators(q_ref[pl.ds(r0, tq), :], k_ref[...]).astype(BF16)

    def value_stage(b, slot):
        r0 = pl.multiple_of(b * tq, tq)
        acc = jnp.dot(p_slots[slot][...], v1_ref[...], preferred_element_type=F32)
        o_ref[pl.ds(r0, tq), :] = (acc[:, :dh] / acc[:, dh:dh + 1]).astype(o_ref.dtype)

    _pipeline_blocks(q_ref.shape[0] // tq, len(p_slots), score_stage, value_stage)


def _attn_c(z, zc, *, batch, seq_len, n_ctx, n_heads, col_k, col_v):
    g = n_heads // C_KV_HEADS
    tq, depth = 256, 4
    n_keys = seq_len + n_ctx
    return pl.pallas_call(
        functools.partial(_attn_c_kernel, tq=tq, group=g),
        out_shape=jax.ShapeDtypeStruct((batch * seq_len, n_heads * HEAD_DIM), BF16),
        grid=(batch, n_heads),
        scratch_shapes=[pltpu.VMEM((n_keys, LANES), BF16), pltpu.VMEM((n_keys, 2 * LANES), BF16)]
        + [pltpu.VMEM((tq, n_keys), BF16)] * depth,
        in_specs=[
            pl.BlockSpec((seq_len, LANES), lambda b, h: (b, h)),
            pl.BlockSpec((seq_len, LANES), lambda b, h: (b, col_k + h // g)),
            pl.BlockSpec((seq_len, LANES), lambda b, h: (b, col_v + h // g)),
            pl.BlockSpec((n_ctx, LANES), lambda b, h: (b, col_k + h // g)),
            pl.BlockSpec((n_ctx, LANES), lambda b, h: (b, col_v + h // g)),
        ],
        out_specs=pl.BlockSpec((seq_len, LANES), lambda b, h: (b, h)),
        compiler_params=_params(("arbitrary",) * 2, 4 * seq_len * LANES * 2, 2 * n_ctx * LANES * 2,
                                (3 * LANES + depth * tq) * n_keys, 3 * tq * n_keys * 4 // 2),
        name="attn_gqa",
    )(z, z, z, zc, zc)


def _outproj_kernel(*refs, n_lhs):
    lhs = refs[:n_lhs]
    w_ref, res_ref, gate_ref, o_ref = refs[n_lhs:]
    acc = None
    k0 = 0
    for a_ref in lhs:
        kw = a_ref.shape[1]
        part = jnp.dot(a_ref[...], w_ref[k0:k0 + kw, :], preferred_element_type=F32)
        acc = part if acc is None else acc + part
        k0 += kw
    o_ref[...] = res_ref[...] + gate_ref[0] * acc


def _out_proj(lhs_list, w_bf16, res2d, gate, *, tm, rows_per_mod):
    rows, n = res2d.shape
    kdim = w_bf16.shape[0]
    tn = 512
    in_specs = [pl.BlockSpec((tm, a.shape[1]), lambda i, j: (i, 0)) for a in lhs_list]
    in_specs += [
        pl.BlockSpec((kdim, tn), lambda i, j: (0, j)),
        pl.BlockSpec((tm, tn), lambda i, j: (i, j)),
        pl.BlockSpec((1, 1, tn), lambda i, j: ((i * tm) // rows_per_mod, 0, j)),
    ]
    return pl.pallas_call(
        functools.partial(_outproj_kernel, n_lhs=len(lhs_list)),
        out_shape=jax.ShapeDtypeStruct((rows, n), F32),
        grid=(rows // tm, n // tn),
        in_specs=in_specs,
        out_specs=pl.BlockSpec((tm, tn), lambda i, j: (i, j)),
        compiler_params=_params(("arbitrary", "arbitrary"), tm * kdim * 2, kdim * tn * 2, 2 * tm * tn * 4),
        name="out_proj",
    )(*lhs_list, w_bf16, res2d, gate)


def _ffn_kernel(x_ref, g_ref, sh_ref, sc_ref, gate_ref, wg_ref, wu_ref, wd_ref, o_ref, h_ref, acc_ref):
    f = pl.program_id(1)

    @pl.when(f == 0)
    def _():
        h_ref[...] = _rms_mod(x_ref[...], g_ref[...], sh_ref[0], sc_ref[0]).astype(BF16)
        acc_ref[...] = jnp.zeros_like(acc_ref)

    h = h_ref[...]
    g_ = jnp.dot(h, wg_ref[...], preferred_element_type=F32)
    u_ = jnp.dot(h, wu_ref[...], preferred_element_type=F32)
    a = (g_ * jax.nn.sigmoid(g_) * u_).astype(BF16)
    acc_ref[...] += jnp.dot(a, wd_ref[...], preferred_element_type=F32)

    @pl.when(f == pl.num_programs(1) - 1)
    def _():
        o_ref[...] = x_ref[...] + gate_ref[0] * acc_ref[...]


def _dense_ffn(x2d, norm_g, shift, scale, gate, w_in_bf16, w_down_bf16, *, tm, rows_per_mod):
    rows, d = x2d.shape
    d_ff = w_down_bf16.shape[0]
    tf = 512
    nf = d_ff // tf
    mod_spec = pl.BlockSpec((1, 1, d), lambda i, f: ((i * tm) // rows_per_mod, 0, 0))
    return pl.pallas_call(
        _ffn_kernel,
        out_shape=jax.ShapeDtypeStruct((rows, d), F32),
        grid=(rows // tm, nf),
        in_specs=[
            pl.BlockSpec((tm, d), lambda i, f: (i, 0)),
            pl.BlockSpec((1, d), lambda i, f: (0, 0)),
            mod_spec, mod_spec, mod_spec,
            pl.BlockSpec((d, tf), lambda i, f: (0, f)),
            pl.BlockSpec((d, tf), lambda i, f: (0, nf + f)),
            pl.BlockSpec((tf, d), lambda i, f: (f, 0)),
        ],
        out_specs=pl.BlockSpec((tm, d), lambda i, f: (i, 0)),
        scratch_shapes=[pltpu.VMEM((tm, d), BF16), pltpu.VMEM((tm, d), F32)],
        compiler_params=_params(("arbitrary", "arbitrary"), 2 * tm * d * 4, 3 * d * tf * 2, tm * d * 3),
        name="dense_swiglu",
    )(x2d, norm_g.reshape(1, d), shift, scale, gate, w_in_bf16, w_in_bf16, w_down_bf16)


def _split3(v):
    hi = v.astype(BF16)
    r = v - hi.astype(F32)
    mid = r.astype(BF16)
    lo = (r - mid.astype(F32)).astype(BF16)
    return hi, mid, lo


def _router_kernel(x_ref, g_ref, sh_ref, sc_ref, wr_ref, h_ref, ids_ref, gates_ref, cnt_ref, carry_ref):
    i = pl.program_id(0)
    tm = x_ref.shape[0]

    @pl.when(i == 0)
    def _():
        carry_ref[...] = jnp.zeros_like(carry_ref)

    h = _rms_mod(x_ref[...], g_ref[...], sh_ref[0], sc_ref[0])
    h_ref[...] = h
    h0, h1, h2 = _split3(h)
    w0, w1, w2 = _split3(wr_ref[...])
    dot = functools.partial(jnp.dot, preferred_element_type=F32)
    logits = (dot(h0, w0) + (dot(h0, w1) + dot(h1, w0))
              + (dot(h0, w2) + dot(h1, w1) + dot(h2, w0)))
    lane = lax.broadcasted_iota(jnp.int32, (tm, LANES), 1)
    logits = jnp.where(lane < N_EXPERTS, logits, -jnp.inf)
    v0 = jnp.max(logits, axis=-1, keepdims=True)
    i0 = jnp.min(jnp.where(logits == v0, lane, LANES), axis=-1, keepdims=True)
    rest = jnp.where(lane == i0, -jnp.inf, logits)
    v1 = jnp.max(rest, axis=-1, keepdims=True)
    i1 = jnp.min(jnp.where(rest == v1, lane, LANES), axis=-1, keepdims=True)
    e1 = jnp.exp(v1 - v0)
    g0 = 1.0 / (1.0 + e1)
    g1 = e1 / (1.0 + e1)

    sel = (lane == i0) | (lane == i1)
    row = lax.broadcasted_iota(jnp.int32, (tm, tm), 0)
    col = lax.broadcasted_iota(jnp.int32, (tm, tm), 1)
    tri = (col < row).astype(BF16)
    before = jnp.dot(tri, sel.astype(BF16), preferred_element_type=F32) + carry_ref[...]
    r0 = jnp.sum(jnp.where(lane == i0, before, 0.0), axis=-1, keepdims=True).astype(jnp.int32)
    r1 = jnp.sum(jnp.where(lane == i1, before, 0.0), axis=-1, keepdims=True).astype(jnp.int32)
    total = carry_ref[...] + jnp.sum(sel.astype(F32), axis=0, keepdims=True)
    carry_ref[...] = total

    ids_ref[...] = jnp.where(lane == 0, i0, jnp.where(lane == 1, i1, jnp.where(lane == 2, r0, r1)))
    gates_ref[...] = jnp.where(lane == 0, g0, g1)
    cnt_ref[...] = jnp.broadcast_to(total, cnt_ref.shape).astype(jnp.int32)


def _router(x2d, norm_g, shift, scale, w_router, *, tm, rows_per_mod):
    rows, d = x2d.shape
    wr = jnp.zeros((d, LANES), F32).at[:, :N_EXPERTS].set(w_router)
    mod_spec = pl.BlockSpec((1, 1, d), lambda i: ((i * tm) // rows_per_mod, 0, 0))
    return pl.pallas_call(
        _router_kernel,
        out_shape=(jax.ShapeDtypeStruct((rows, d), F32),
                   jax.ShapeDtypeStruct((rows, LANES), jnp.int32),
                   jax.ShapeDtypeStruct((rows, LANES), F32),
                   jax.ShapeDtypeStruct((8, LANES), jnp.int32)),
        grid=(rows // tm,),
        in_specs=[pl.BlockSpec((tm, d), lambda i: (i, 0)),
                  pl.BlockSpec((1, d), lambda i: (0, 0)),
                  mod_spec, mod_spec,
                  pl.BlockSpec((d, LANES), lambda i: (0, 0))],
        out_specs=(pl.BlockSpec((tm, d), lambda i: (i, 0)),
                   pl.BlockSpec((tm, LANES), lambda i: (i, 0)),
                   pl.BlockSpec((tm, LANES), lambda i: (i, 0)),
                   pl.BlockSpec((8, LANES), lambda i: (0, 0))),
        scratch_shapes=[pltpu.VMEM((1, LANES), F32)],
        compiler_params=_params(("arbitrary",), 2 * tm * d * 4, d * LANES * 4, 2 * tm * LANES * 4),
        name="router",
    )(x2d, norm_g.reshape(1, d), shift, scale, wr)


def _row_copy(src_hbm, row, dst_ref, r, sem):
    return pltpu.make_async_copy(src_hbm.at[pl.ds(row, 1), :], dst_ref.at[pl.ds(r, 1), :], sem)


def _dispatch_kernel(idx_ref, src_hbm, o_ref, buf_ref, sem):
    tr = buf_ref.shape[0]
    base = pl.program_id(0) * tr

    def start(r, c):
        _row_copy(src_hbm, idx_ref[base + r], buf_ref, r, sem).start()
        return c

    def wait(r, c):
        _row_copy(src_hbm, 0, buf_ref, r, sem).wait()
        return c

    lax.fori_loop(0, tr, start, 0)
    lax.fori_loop(0, tr, wait, 0)
    o_ref[...] = buf_ref[...].astype(o_ref.dtype)


def _dispatch(h2d, row_tok, *, tr):
    n_rows = row_tok.shape[0]
    d = h2d.shape[1]
    grid_spec = pltpu.PrefetchScalarGridSpec(
        num_scalar_prefetch=1,
        grid=(n_rows // tr,),
        in_specs=[pl.BlockSpec(memory_space=pl.ANY)],
        out_specs=pl.BlockSpec((tr, d), lambda i, idx: (i, 0)),
        scratch_shapes=[pltpu.VMEM((tr, d), F32), pltpu.SemaphoreType.DMA(())],
    )
    return pl.pallas_call(
        _dispatch_kernel,
        out_shape=jax.ShapeDtypeStruct((n_rows, d), BF16),
        grid_spec=grid_spec,
        compiler_params=_params(("arbitrary",), tr * d * 2, tr * d * 2),
        name="expert_dispatch",
    )(row_tok, h2d)


def _combine_kernel(d0_ref, d1_ref, ys_hbm, x_ref, gate_ref, rg_ref, o_ref, a_ref, b_ref, sem):
    tt = x_ref.shape[0]
    base = pl.program_id(0) * tt

    def start(r, c):
        _row_copy(ys_hbm, d0_ref[base + r], a_ref, r, sem.at[0]).start()
        _row_copy(ys_hbm, d1_ref[base + r], b_ref, r, sem.at[1]).start()
        return c

    def wait(r, c):
        _row_copy(ys_hbm, 0, a_ref, r, sem.at[0]).wait()
        _row_copy(ys_hbm, 0, b_ref, r, sem.at[1]).wait()
        return c

    lax.fori_loop(0, tt, start, 0)
    lax.fori_loop(0, tt, wait, 0)
    rg = rg_ref[...]
    mix = rg[:, 0:1] * a_ref[...] + rg[:, 1:2] * b_ref[...]
    o_ref[...] = x_ref[...] + gate_ref[0] * mix


def _combine(ys, dest0, dest1, x2d, gate, row_gates, *, tt, rows_per_mod):
    rows, d = x2d.shape
    grid_spec = pltpu.PrefetchScalarGridSpec(
        num_scalar_prefetch=2,
        grid=(rows // tt,),
        in_specs=[pl.BlockSpec(memory_space=pl.ANY),
                  pl.BlockSpec((tt, d), lambda i, a, b: (i, 0)),
                  pl.BlockSpec((1, 1, d), lambda i, a, b: ((i * tt) // rows_per_mod, 0, 0)),
                  pl.BlockSpec((tt, LANES), lambda i, a, b: (i, 0))],
        out_specs=pl.BlockSpec((tt, d), lambda i, a, b: (i, 0)),
        scratch_shapes=[pltpu.VMEM((tt, d), F32), pltpu.VMEM((tt, d), F32), pltpu.SemaphoreType.DMA((2,))],
    )
    return pl.pallas_call(
        _combine_kernel,
        out_shape=jax.ShapeDtypeStruct((rows, d), F32),
        grid_spec=grid_spec,
        compiler_params=_params(("arbitrary",), 2 * tt * d * 4, 2 * tt * d * 2),
        name="expert_combine",
    )(dest0, dest1, ys, x2d, gate, row_gates)


def _stream_group_weights(sched, copies, cast):
    te_ref, first_ref, grp_ref, ngrp_ref, nxt_ref = sched
    j, r = pl.program_id(0), pl.program_id(1)
    nj = pl.num_programs(0)

    @pl.when(first_ref[r] == 1)
    def _():
        g = grp_ref[r]
        ng = ngrp_ref[0]
        slot = (j * ng + g) & 1

        @pl.when((j == 0) & (g == 0))
        def _():
            for cp in copies(te_ref[r], j, slot):
                cp.start()

        for cp in copies(te_ref[r], j, slot):
            cp.wait()
        last = g == ng - 1

        @pl.when(jnp.logical_not(last & (j == nj - 1)))
        def _():
            for cp in copies(nxt_ref[r], jnp.where(last, j + 1, j), 1 - slot):
                cp.start()

        cast(slot)


def _gmm_up_kernel(te_ref, first_ref, grp_ref, ngrp_ref, nxt_ref, nv_ref, x_ref, w_hbm, o_ref,
                   stage_ref, wbf_ref, sem):
    tn = o_ref.shape[1]
    nj = pl.num_programs(0)

    def copies(e, j, slot):
        return [pltpu.make_async_copy(w_hbm.at[e, :, pl.ds(pl.multiple_of((half * nj + j) * tn, tn), tn)],
                                      stage_ref.at[slot, half], sem.at[slot, half]) for half in range(2)]

    def cast(slot):
        wbf_ref[0] = stage_ref[slot, 0].astype(BF16)
        wbf_ref[1] = stage_ref[slot, 1].astype(BF16)

    _stream_group_weights((te_ref, first_ref, grp_ref, ngrp_ref, nxt_ref), copies, cast)
    valid = pl.program_id(1) < nv_ref[0]

    @pl.when(valid)
    def _():
        x = x_ref[...]
        g_ = jnp.dot(x, wbf_ref[0], preferred_element_type=F32)
        u_ = jnp.dot(x, wbf_ref[1], preferred_element_type=F32)
        o_ref[...] = (g_ * jax.nn.sigmoid(g_) * u_).astype(o_ref.dtype)

    @pl.when(jnp.logical_not(valid))
    def _():
        o_ref[...] = jnp.zeros_like(o_ref)


def _gmm_down_kernel(te_ref, first_ref, grp_ref, ngrp_ref, nxt_ref, nv_ref, h_ref, w_hbm, o_ref,
                     stage_ref, wbf_ref, sem):
    tn = o_ref.shape[1]

    def copies(e, j, slot):
        return [pltpu.make_async_copy(w_hbm.at[e, :, pl.ds(pl.multiple_of(j * tn, tn), tn)],
                                      stage_ref.at[slot], sem.at[slot])]

    def cast(slot):
        wbf_ref[...] = stage_ref[slot].astype(BF16)

    _stream_group_weights((te_ref, first_ref, grp_ref, ngrp_ref, nxt_ref), copies, cast)
    valid = pl.program_id(1) < nv_ref[0]

    @pl.when(valid)
    def _():
        o_ref[...] = jnp.dot(h_ref[...], wbf_ref[...], preferred_element_type=F32)

    @pl.when(jnp.logical_not(valid))
    def _():
        o_ref[...] = jnp.zeros_like(o_ref)


def _expert_ffn(xs, w_in, w_down, sched, n_valid, *, tr):
    n_rows, d = xs.shape
    d_ff = w_down.shape[1]
    n_tiles = n_rows // tr
    n_sched = len(sched) + 1
    tn1 = 1024
    nj1 = d_ff // tn1

    def row_map(*args):
        r, nv = args[1], args[-1]
        return (jnp.minimum(r, nv[0] - 1), 0)

    def out_map(*args):
        return (args[1], args[0])

    up_spec = pltpu.PrefetchScalarGridSpec(
        num_scalar_prefetch=n_sched,
        grid=(nj1, n_tiles),
        in_specs=[pl.BlockSpec((tr, d), row_map), pl.BlockSpec(memory_space=pl.ANY)],
        out_specs=pl.BlockSpec((tr, tn1), out_map),
        scratch_shapes=[pltpu.VMEM((2, 2, d, tn1), F32), pltpu.VMEM((2, d, tn1), BF16),
                        pltpu.SemaphoreType.DMA((2, 2))],
    )
    stage1 = 4 * d * tn1 * 4
    hidden = pl.pallas_call(
        _gmm_up_kernel,
        out_shape=jax.ShapeDtypeStruct((n_rows, d_ff), BF16),
        grid_spec=up_spec,
        compiler_params=pltpu.CompilerParams(
            dimension_semantics=("arbitrary", "arbitrary"),
            vmem_limit_bytes=stage1 + 2 * d * tn1 * 2 + 2 * (tr * d * 2 + tr * tn1 * 2) + VMEM_TEMP_ALLOWANCE // 2),
        name="expert_up",
    )(*sched, n_valid, xs, w_in)

    tn2 = 512
    down_spec = pltpu.PrefetchScalarGridSpec(
        num_scalar_prefetch=n_sched,
        grid=(d // tn2, n_tiles),
        in_specs=[pl.BlockSpec((tr, d_ff), row_map), pl.BlockSpec(memory_space=pl.ANY)],
        out_specs=pl.BlockSpec((tr, tn2), out_map),
        scratch_shapes=[pltpu.VMEM((2, d_ff, tn2), F32), pltpu.VMEM((d_ff, tn2), BF16),
                        pltpu.SemaphoreType.DMA((2,))],
    )
    stage2 = 2 * d_ff * tn2 * 4
    return pl.pallas_call(
        _gmm_down_kernel,
        out_shape=jax.ShapeDtypeStruct((n_rows, d), F32),
        grid_spec=down_spec,
        compiler_params=pltpu.CompilerParams(
            dimension_semantics=("arbitrary", "arbitrary"),
            vmem_limit_bytes=stage2 + d_ff * tn2 * 2 + 2 * (tr * d_ff * 2 + tr * tn2 * 4) + VMEM_TEMP_ALLOWANCE // 2),
        name="expert_down",
    )(*sched, n_valid, hidden, w_down)


def _moe(x2d, norm_g, shift, scale, gate, w_router, w_in, w_down, *, rows_per_mod):
    rows, d = x2d.shape
    tr = 256
    h2d, ids, gates, counts = _router(x2d, norm_g, shift, scale, w_router, tm=512, rows_per_mod=rows_per_mod)
    counts = counts[0, :N_EXPERTS]
    padded = (counts + tr - 1) // tr * tr
    pad_end = jnp.cumsum(padded)
    pad_start = pad_end - padded
    dest0 = pad_start[ids[:, 0]] + ids[:, 2]
    dest1 = pad_start[ids[:, 1]] + ids[:, 3]
    n_rows = 2 * rows + N_EXPERTS * tr
    n_tiles = n_rows // tr
    tok = jnp.arange(rows, dtype=jnp.int32)
    row_tok = jnp.zeros((n_rows,), jnp.int32).at[dest0].set(tok).at[dest1].set(tok)
    tile_expert = jnp.minimum(
        jnp.sum(jnp.arange(n_tiles, dtype=jnp.int32)[:, None] * tr >= pad_end[None, :], axis=1),
        N_EXPERTS - 1).astype(jnp.int32)
    n_valid = (pad_end[-1:] // tr).astype(jnp.int32)
    tile_id = jnp.arange(n_tiles, dtype=jnp.int32)
    prev_expert = jnp.concatenate([jnp.full((1,), -1, jnp.int32), tile_expert[:-1]])
    first = ((tile_id < n_valid[0]) & (tile_expert != prev_expert)).astype(jnp.int32)
    grp = jnp.cumsum(first).astype(jnp.int32) - 1
    n_grp = jnp.sum(first, keepdims=True).astype(jnp.int32)
    expert_id = jnp.arange(N_EXPERTS, dtype=jnp.int32)
    grp_expert = jnp.sort(jnp.where(counts > 0, expert_id, N_EXPERTS))
    nxt = grp_expert[(grp + 1) % n_grp[0]].astype(jnp.int32)
    sched = (tile_expert, first, grp, n_grp, nxt)

    xs = _dispatch(h2d, row_tok, tr=tr)
    ys = _expert_ffn(xs, w_in, w_down, sched, n_valid, tr=tr)
    return _combine(ys, dest0, dest1, x2d, gate, gates, tt=256, rows_per_mod=rows_per_mod)


def _rope_tables(seq_len):
    t = jnp.arange(seq_len, dtype=jnp.int32)
    row = (t // GRID_W).astype(F32)[:, None]
    col = (t % GRID_W).astype(F32)[:, None]
    inv = ROPE_THETA ** (-jnp.arange(0, ROT_AXIS, 2, dtype=F32) / ROT_AXIS)
    ar, ac = row * inv, col * inv
    zero = jnp.zeros_like(ar)
    cos = jnp.concatenate([jnp.cos(ar), jnp.cos(ar), jnp.cos(ac), jnp.cos(ac)], axis=-1)
    s_next = jnp.concatenate([-jnp.sin(ar), zero, -jnp.sin(ac), zero], axis=-1)
    s_prev = jnp.concatenate([zero, jnp.sin(ar), zero, jnp.sin(ac)], axis=-1)
    return cos, s_next, s_prev


def _mod_vectors(c, c_ctx, w_mod, b_mod):
    batch, d = c.shape
    cvec = jnp.zeros((8, d), F32).at[:batch].set(c).at[batch].set(c_ctx)
    m = _modulation(jnp.concatenate([cvec, cvec], axis=0), w_mod, b_mod)
    lat = m[:batch].reshape(batch, N_MOD, 1, d)
    ctx = m[batch].reshape(N_MOD, 1, 1, d)
    return [lat[:, k] for k in range(N_MOD)], [ctx[k] for k in range(N_MOD)]


def kernel(x, c, ctx, c_ctx, e_norm1, e_norm2, e_w_mod, e_b_mod, e_w_in, e_w_out, e_a_qnorm, e_a_knorm,
           e_a_sink, e_b_qnorm, e_b_knorm, e_b_lam_q1, e_b_lam_k1, e_b_lam_q2, e_b_lam_k2, e_b_subln,
           e_ffn_w_in, e_ffn_w_down, o_norm1, o_norm2, o_w_mod, o_b_mod, o_w_in, o_w_out, o_c_qnorm,
           o_c_knorm, o_router, o_exp_w_in, o_exp_w_down):
    batch, seq_len, d = x.shape
    n_ctx = ctx.shape[1]
    qk_scale = HEAD_DIM ** -0.5 * LOG2_E
    tables = _rope_tables(seq_len)
    ctx_tables = tuple(t[:n_ctx] for t in tables)
    x2d = x.reshape(batch * seq_len, d)
    xc2d = ctx.reshape(batch * n_ctx, d)
    tm = 1024

    (sh1, sc1, g1, sh2, sc2, g2), (csh1, csc1, cg1, csh2, csc2, cg2) = _mod_vectors(
        c, c_ctx, e_w_mod[0], e_b_mod[0])
    a_q, a_kv = A_HEADS * HEAD_DIM, A_KV_HEADS * HEAD_DIM
    b_qk, b_v = B_HEADS * 2 * HEAD_DIM, B_HEADS * 2 * HEAD_DIM
    ones = lambda n: jnp.ones((n,), F32)
    tile = lambda v, n: jnp.tile(v, n // HEAD_DIM)
    col_gain = jnp.concatenate([tile(e_a_qnorm[0] * qk_scale, a_q), tile(e_a_knorm[0], a_kv), ones(a_kv),
                                tile(e_b_qnorm[0] * qk_scale, b_qk), tile(e_b_knorm[0], b_qk), ones(b_v)])
    kinds = lambda *pairs: jnp.concatenate([jnp.full((n // HEAD_DIM,), k, jnp.int32) for k, n in pairs])
    col_kind = kinds((1, a_q), (1, a_kv), (0, a_kv), (1, b_qk), (1, b_qk), (0, b_v))
    w_in0 = e_w_in[0].astype(BF16)
    z = _in_proj(x2d, e_norm1[0], sh1, sc1, w_in0, col_gain, col_kind, tables,
                 tm=tm, rows_per_mod=seq_len, rope=True)
    zc = _in_proj(xc2d, e_norm1[0], csh1, csc1, w_in0, col_gain, col_kind, ctx_tables,
                  tm=n_ctx, rows_per_mod=batch * n_ctx, rope=False)

    ca_q, ca_k, ca_v = 0, a_q // LANES, (a_q + a_kv) // LANES
    cb_q = (a_q + 2 * a_kv) // LANES
    cb_k, cb_v = cb_q + b_qk // LANES, cb_q + 2 * b_qk // LANES
    lam_init = 0.8 - 0.6 * math.exp(-0.3 * 0)
    dims = dict(batch=batch, seq_len=seq_len, n_ctx=n_ctx)
    attn_a = functools.partial(_attn_a, z, zc, e_a_sink[0], col_q=ca_q, col_k=ca_k, col_v=ca_v, **dims)
    attn_b = functools.partial(_attn_b, z, zc, e_b_lam_q1[0], e_b_lam_k1[0], e_b_lam_q2[0], e_b_lam_k2[0],
                               e_b_subln[0], col_q=cb_q, col_k=cb_k, col_v=cb_v, lam_init=lam_init, **dims)
    w_out0 = e_w_out[0].astype(BF16)
    x2d = _out_proj([attn_a(q_from_ctx=False), attn_b(q_from_ctx=False)], w_out0, x2d, g1,
                    tm=tm, rows_per_mod=seq_len)
    xc2d = _out_proj([attn_a(q_from_ctx=True), attn_b(q_from_ctx=True)], w_out0, xc2d, cg1,
                     tm=n_ctx, rows_per_mod=batch * n_ctx)
    ffn_in, ffn_down = e_ffn_w_in[0].astype(BF16), e_ffn_w_down[0].astype(BF16)
    x2d = _dense_ffn(x2d, e_norm2[0], sh2, sc2, g2, ffn_in, ffn_down, tm=512, rows_per_mod=seq_len)
    xc2d = _dense_ffn(xc2d, e_norm2[0], csh2, csc2, cg2, ffn_in, ffn_down, tm=n_ctx,
                      rows_per_mod=batch * n_ctx)

    (sh1, sc1, g1, sh2, sc2, g2), (csh1, csc1, _, _, _, _) = _mod_vectors(c, c_ctx, o_w_mod[0], o_b_mod[0])
    c_q = d
    c_kv = C_KV_HEADS * HEAD_DIM
    col_gain = jnp.concatenate([tile(o_c_qnorm[0] * qk_scale, c_q), tile(o_c_knorm[0], c_kv), ones(c_kv)])
    col_kind = kinds((1, c_q), (1, c_kv), (0, c_kv))
    w_in1 = o_w_in[0].astype(BF16)
    z = _in_proj(x2d, o_norm1[0], sh1, sc1, w_in1, col_gain, col_kind, tables,
                 tm=tm, rows_per_mod=seq_len, rope=True)
    zc = _in_proj(xc2d, o_norm1[0], csh1, csc1, w_in1, col_gain, col_kind, ctx_tables,
                  tm=n_ctx, rows_per_mod=batch * n_ctx, rope=False)
    o = _attn_c(z, zc, n_heads=c_q // HEAD_DIM, col_k=c_q // LANES, col_v=(c_q + c_kv) // LANES, **dims)
    x2d = _out_proj([o], o_w_out[0].astype(BF16), x2d, g1, tm=tm, rows_per_mod=seq_len)
    x2d = _moe(x2d, o_norm2[0], sh2, sc2, g2, o_router[0], o_exp_w_in.reshape(o_exp_w_in.shape[1:]),
               o_exp_w_down.reshape(o_exp_w_down.shape[1:]), rows_per_mod=seq_len)
    return x2d.reshape(batch, seq_len, d)
```

```python
import functools
import math

import jax
import jax.numpy as jnp
import numpy as np
from jax import lax
from jax.experimental import pallas as pl
from jax.experimental.pallas import tpu as pltpu

F32 = jnp.float32
BF16 = jnp.bfloat16

HEAD_DIM = 128
GRID_W = 64
ROT_AXIS = HEAD_DIM // 2
ROPE_THETA = 10000.0
EPS = 1e-6
NEG_INF = -1e30
LOG2_E = math.log2(math.e)
N_MOD = 6
WINDOW = 128
A_HEADS, A_KV_HEADS = 8, 2
B_HEADS = 4
C_KV_HEADS = 4
N_EXPERTS = 8
LANES = 128
V7X_VMEM_BYTES = 64 * 1024 * 1024
VMEM_TEMP_ALLOWANCE = 16 * 1024 * 1024


def _vmem_limit(*block_bytes):
    need = 2 * sum(block_bytes) + VMEM_TEMP_ALLOWANCE
    return int(min(need, V7X_VMEM_BYTES - 6 * 1024 * 1024))


def _params(sem, *block_bytes):
    return pltpu.CompilerParams(dimension_semantics=sem, vmem_limit_bytes=_vmem_limit(*block_bytes))


def _nt_dot(a, b):
    return lax.dot_general(a, b, (((1,), (1,)), ((), ())), preferred_element_type=F32)


def _rms_mod(x, g, shift, scale):
    ms = jnp.mean(x * x, axis=-1, keepdims=True)
    y = x * lax.rsqrt(ms + EPS) * g
    return y * (1.0 + scale) + shift


def _mod_kernel(c_ref, w_ref, b_ref, o_ref):
    c = c_ref[...]
    s = c * jax.nn.sigmoid(c)
    s_hi = s.astype(BF16).astype(F32)
    top = lax.broadcasted_iota(jnp.int32, s.shape, 0) < 8
    lhs = jnp.where(top, s_hi, s - s_hi).astype(BF16)
    acc = jnp.dot(lhs, w_ref[...].astype(BF16), preferred_element_type=F32)
    o_ref[...] = acc[0:8] + acc[8:16] + b_ref[...]


def _modulation(cvec, w_mod, b_mod):
    d, n = w_mod.shape
    tn = 1024
    return pl.pallas_call(
        _mod_kernel,
        out_shape=jax.ShapeDtypeStruct((8, n), F32),
        grid=(n // tn,),
        in_specs=[pl.BlockSpec((16, d), lambda j: (0, 0)),
                  pl.BlockSpec((d, tn), lambda j: (0, j)),
                  pl.BlockSpec((1, tn), lambda j: (0, j))],
        out_specs=pl.BlockSpec((8, tn), lambda j: (0, j)),
        compiler_params=_params(("arbitrary",), d * tn * 4, d * tn * 2),
        name="modulation",
    )(cvec, w_mod, b_mod.reshape(1, n))


def _inproj_kernel(kind_ref, x_ref, g_ref, sh_ref, sc_ref, w_ref, gc_ref, cos_ref, sa_ref, sb_ref,
                   o_ref, h_ref, *, rope):
    j = pl.program_id(1)
    tn = o_ref.shape[1]
    nch = tn // LANES

    @pl.when(j == 0)
    def _():
        h_ref[...] = _rms_mod(x_ref[...], g_ref[...], sh_ref[0], sc_ref[0]).astype(BF16)

    acc = jnp.dot(h_ref[...], w_ref[...], preferred_element_type=F32)
    for c in range(nch):
        a = acc[:, c * LANES:(c + 1) * LANES]
        kind = kind_ref[j * nch + c]

        @pl.when(kind == 0)
        def _():
            o_ref[:, c * LANES:(c + 1) * LANES] = a.astype(o_ref.dtype)

        @pl.when(kind == 1)
        def _():
            ms = jnp.mean(a * a, axis=-1, keepdims=True)
            y = a * lax.rsqrt(ms + EPS) * gc_ref[:, c * LANES:(c + 1) * LANES]
            if rope:
                y = (y * cos_ref[...] + pltpu.roll(y, LANES - 32, 1) * sa_ref[...]
                     + pltpu.roll(y, 32, 1) * sb_ref[...])
            o_ref[:, c * LANES:(c + 1) * LANES] = y.astype(o_ref.dtype)


def _in_proj(x2d, norm_g, shift, scale, w_bf16, col_gain, col_kind, tables, *, tm, rows_per_mod, rope):
    rows, d = x2d.shape
    n = w_bf16.shape[1]
    tn = 512
    cos_t, sa_t, sb_t = tables
    pos_tiles = cos_t.shape[0] // tm

    grid_spec = pltpu.PrefetchScalarGridSpec(
        num_scalar_prefetch=1,
        grid=(rows // tm, n // tn),
        in_specs=[
            pl.BlockSpec((tm, d), lambda i, j, k: (i, 0)),
            pl.BlockSpec((1, d), lambda i, j, k: (0, 0)),
            pl.BlockSpec((1, 1, d), lambda i, j, k: ((i * tm) // rows_per_mod, 0, 0)),
            pl.BlockSpec((1, 1, d), lambda i, j, k: ((i * tm) // rows_per_mod, 0, 0)),
            pl.BlockSpec((d, tn), lambda i, j, k: (0, j)),
            pl.BlockSpec((1, tn), lambda i, j, k: (0, j)),
            pl.BlockSpec((tm, LANES), lambda i, j, k: (i % pos_tiles, 0)),
            pl.BlockSpec((tm, LANES), lambda i, j, k: (i % pos_tiles, 0)),
            pl.BlockSpec((tm, LANES), lambda i, j, k: (i % pos_tiles, 0)),
        ],
        out_specs=pl.BlockSpec((tm, tn), lambda i, j, k: (i, j)),
        scratch_shapes=[pltpu.VMEM((tm, d), BF16)],
    )
    return pl.pallas_call(
        functools.partial(_inproj_kernel, rope=rope),
        out_shape=jax.ShapeDtypeStruct((rows, n), BF16),
        grid_spec=grid_spec,
        compiler_params=_params(("arbitrary", "arbitrary"), tm * d * 4, d * tn * 2, tm * tn * 2,
                                3 * tm * LANES * 4, tm * d),
        name="in_proj_rope" if rope else "in_proj_ctx",
    )(col_kind, x2d, norm_g.reshape(1, d), shift, scale, w_bf16, col_gain.reshape(1, n), cos_t, sa_t, sb_t)


def _window_mask(q0, ws, tq, nk):
    qpos = q0 + lax.broadcasted_iota(jnp.int32, (tq, nk), 0)
    kpos = ws + lax.broadcasted_iota(jnp.int32, (tq, nk), 1)
    return jnp.abs(qpos - kpos) <= WINDOW


def _attn_a_kernel(sink_ref, q_ref, kl_ref, vl_ref, kc_ref, vc_ref, o_ref, *, tq, seq_len):
    h = pl.program_id(1)
    sink = sink_ref[h] * LOG2_E
    kc = kc_ref[...]
    vc = vc_ref[...]
    bq = q_ref.shape[0]
    nk = tq + 2 * WINDOW

    def body(t, carry):
        r0 = pl.multiple_of(t * tq, tq)
        q = q_ref[pl.ds(r0, tq), :]
        s_ctx = _nt_dot(q, kc)
        m = jnp.maximum(jnp.max(s_ctx, axis=-1, keepdims=True), sink)
        if seq_len:
            q0 = pl.program_id(2) * bq + r0
            ws = pl.multiple_of(jnp.clip(q0 - WINDOW, 0, seq_len - nk), WINDOW)
            s_loc = _nt_dot(q, kl_ref[pl.ds(ws, nk), :])
            s_loc = jnp.where(_window_mask(q0, ws, tq, nk), s_loc, NEG_INF)
            m = jnp.maximum(m, jnp.max(s_loc, axis=-1, keepdims=True))
        p_ctx = jnp.exp2(s_ctx - m)
        den = jnp.sum(p_ctx, axis=-1, keepdims=True) + jnp.exp2(sink - m)
        o = jnp.dot(p_ctx.astype(BF16), vc, preferred_element_type=F32)
        if seq_len:
            p_loc = jnp.exp2(s_loc - m)
            den = den + jnp.sum(p_loc, axis=-1, keepdims=True)
            o = o + jnp.dot(p_loc.astype(BF16), vl_ref[pl.ds(ws, nk), :], preferred_element_type=F32)
        o_ref[pl.ds(r0, tq), :] = (o / den).astype(o_ref.dtype)
        return carry

    lax.fori_loop(0, bq // tq, body, 0)


def _attn_a(z, zc, sink, *, batch, seq_len, n_ctx, q_from_ctx, col_q, col_k, col_v):
    g = A_HEADS // A_KV_HEADS
    if q_from_ctx:
        lq, bq, tq, qsrc = n_ctx, n_ctx, n_ctx, zc
    else:
        lq, bq, tq, qsrc = seq_len, 1024, 256, z
    nqb = lq // bq
    lat_len = 0 if q_from_ctx else seq_len
    lat_block = 16 if q_from_ctx else seq_len
    grid_spec = pltpu.PrefetchScalarGridSpec(
        num_scalar_prefetch=1,
        grid=(batch, A_HEADS, nqb),
        in_specs=[
            pl.BlockSpec((bq, LANES), lambda b, h, i, s: (b * nqb + i, col_q + h)),
            pl.BlockSpec((lat_block, LANES), lambda b, h, i, s: (b * (seq_len // lat_block), col_k + h // g)),
            pl.BlockSpec((lat_block, LANES), lambda b, h, i, s: (b * (seq_len // lat_block), col_v + h // g)),
            pl.BlockSpec((n_ctx, LANES), lambda b, h, i, s: (b, col_k + h // g)),
            pl.BlockSpec((n_ctx, LANES), lambda b, h, i, s: (b, col_v + h // g)),
        ],
        out_specs=pl.BlockSpec((bq, LANES), lambda b, h, i, s: (b * nqb + i, h)),
    )
    return pl.pallas_call(
        functools.partial(_attn_a_kernel, tq=tq, seq_len=lat_len),
        out_shape=jax.ShapeDtypeStruct((batch * lq, A_HEADS * HEAD_DIM), BF16),
        grid_spec=grid_spec,
        compiler_params=_params(("arbitrary",) * 3, 2 * bq * LANES * 2, 2 * lat_block * LANES * 2,
                                2 * n_ctx * LANES * 2),
        name="attn_window_ctx" if q_from_ctx else "attn_window",
    )(sink, qsrc, z, z, zc, zc)


def _pipeline_blocks(n_blocks, depth, score_stage, value_stage):
    score_stage(0, 0)

    def body(u, carry):
        for i in range(depth):
            b = u * depth + i
            score_stage(jnp.minimum(b + 1, n_blocks - 1), (i + 1) % depth)
            value_stage(b, i)
        return carry

    lax.fori_loop(0, n_blocks // depth, body, 0)


def _join_rows(dst_ref, lat_ref, ctx_ref, has_lat):
    n_lat = lat_ref.shape[0] if has_lat else 0
    if has_lat:
        dst_ref[0:n_lat, 0:lat_ref.shape[1]] = lat_ref[...]
    dst_ref[n_lat:n_lat + ctx_ref.shape[0], 0:ctx_ref.shape[1]] = ctx_ref[...]


def _exp2_numerators(q, k_all):
    s = _nt_dot(q, k_all)
    return jnp.exp2(s - jnp.max(s, axis=-1, keepdims=True))


def _attn_b_kernel(q1_ref, q2_ref, k1l_ref, k2l_ref, vl_ref, k1c_ref, k2c_ref, vc_ref,
                   lq1_ref, lk1_ref, lq2_ref, lk2_ref, sub_ref, o_ref, k1_ref, k2_ref, v_ref, *p_slots,
                   tq, has_lat, lam_init):
    lam = (jnp.exp(jnp.sum(lq1_ref[0] * lk1_ref[0], axis=-1, keepdims=True))
           - jnp.exp(jnp.sum(lq2_ref[0] * lk2_ref[0], axis=-1, keepdims=True)) + lam_init)
    _join_rows(k1_ref, k1l_ref, k1c_ref, has_lat)
    _join_rows(k2_ref, k2l_ref, k2c_ref, has_lat)
    _join_rows(v_ref, vl_ref, vc_ref, has_lat)

    def score_stage(b, slot):
        r0 = pl.multiple_of(b * tq, tq)
        p1 = _exp2_numerators(q1_ref[pl.ds(r0, tq), :], k1_ref[...])
        p2 = _exp2_numerators(q2_ref[pl.ds(r0, tq), :], k2_ref[...])
        w1 = 1.0 / jnp.sum(p1, axis=-1, keepdims=True)
        w2 = lam / jnp.sum(p2, axis=-1, keepdims=True)
        p_slots[slot][...] = (p1 * w1 - p2 * w2).astype(BF16)

    def value_stage(b, slot):
        r0 = pl.multiple_of(b * tq, tq)
        o = jnp.dot(p_slots[slot][...], v_ref[...], preferred_element_type=F32)
        ms = jnp.mean(o * o, axis=-1, keepdims=True)
        o = o * lax.rsqrt(ms + EPS) * sub_ref[...] * (1.0 - lam_init)
        o_ref[pl.ds(r0, tq), :] = o.astype(o_ref.dtype)

    _pipeline_blocks(q1_ref.shape[0] // tq, len(p_slots), score_stage, value_stage)


def _attn_b(z, zc, lq1, lk1, lq2, lk2, subln, *, batch, seq_len, n_ctx, q_from_ctx, col_q, col_k, col_v,
            lam_init):
    dv = 2 * HEAD_DIM
    if q_from_ctx:
        lq, bq, tq, qsrc, depth = n_ctx, n_ctx, n_ctx, zc, 1
    else:
        lq, bq, tq, qsrc, depth = seq_len, seq_len, 256, z, 4
    nqb = lq // bq
    lat_block = 16 if q_from_ctx else seq_len
    lat_tiles = seq_len // lat_block
    n_keys = n_ctx if q_from_ctx else seq_len + n_ctx
    lam_spec = pl.BlockSpec((1, 1, HEAD_DIM), lambda b, h, i: (h, 0, 0))
    in_specs = [
        pl.BlockSpec((bq, LANES), lambda b, h, i: (b * nqb + i, col_q + 2 * h)),
        pl.BlockSpec((bq, LANES), lambda b, h, i: (b * nqb + i, col_q + 2 * h + 1)),
        pl.BlockSpec((lat_block, LANES), lambda b, h, i: (b * lat_tiles, col_k + 2 * h)),
        pl.BlockSpec((lat_block, LANES), lambda b, h, i: (b * lat_tiles, col_k + 2 * h + 1)),
        pl.BlockSpec((lat_block, dv), lambda b, h, i: (b * lat_tiles, col_v // 2 + h)),
        pl.BlockSpec((n_ctx, LANES), lambda b, h, i: (b, col_k + 2 * h)),
        pl.BlockSpec((n_ctx, LANES), lambda b, h, i: (b, col_k + 2 * h + 1)),
        pl.BlockSpec((n_ctx, dv), lambda b, h, i: (b, col_v // 2 + h)),
        lam_spec, lam_spec, lam_spec, lam_spec,
        pl.BlockSpec((1, dv), lambda b, h, i: (0, 0)),
    ]
    return pl.pallas_call(
        functools.partial(_attn_b_kernel, tq=tq, has_lat=not q_from_ctx, lam_init=lam_init),
        out_shape=jax.ShapeDtypeStruct((batch * lq, B_HEADS * dv), BF16),
        grid=(batch, B_HEADS, nqb),
        in_specs=in_specs,
        out_specs=pl.BlockSpec((bq, dv), lambda b, h, i: (b * nqb + i, h)),
        scratch_shapes=[pltpu.VMEM((n_keys, LANES), BF16), pltpu.VMEM((n_keys, LANES), BF16),
                        pltpu.VMEM((n_keys, dv), BF16)] + [pltpu.VMEM((tq, n_keys), BF16)] * depth,
        compiler_params=_params(("arbitrary",) * 3, 2 * bq * LANES * 2, 4 * lat_block * LANES * 2,
                                4 * n_ctx * LANES * 2, bq * dv * 2, (4 * LANES + depth * tq) * n_keys,
                                4 * tq * n_keys * 4 // 2),
        name="attn_diff_ctx" if q_from_ctx else "attn_diff",
    )(qsrc, qsrc, z, z, z, zc, zc, zc,
      lq1.reshape(B_HEADS, 1, HEAD_DIM), lk1.reshape(B_HEADS, 1, HEAD_DIM),
      lq2.reshape(B_HEADS, 1, HEAD_DIM), lk2.reshape(B_HEADS, 1, HEAD_DIM), subln.reshape(1, dv))


def _attn_c_kernel(q_ref, kl_ref, vl_ref, kc_ref, vc_ref, o_ref, k_ref, v1_ref, *p_slots, tq, group):
    dh = vl_ref.shape[1]

    @pl.when(pl.program_id(1) % group == 0)
    def _():
        _join_rows(k_ref, kl_ref, kc_ref, True)
        _join_rows(v1_ref, vl_ref, vc_ref, True)
        v1_ref[:, dh:] = jnp.ones((v1_ref.shape[0], v1_ref.shape[1] - dh), v1_ref.dtype)

    def score_stage(b, slot):
        r0 = pl.multiple_of(b * tq, tq)
        p_slots[slot][...] = _exp2_numerators(q_ref[pl.ds(r0, tq), :], k_ref[...]).astype(BF16)

    def value_stage(b, slot):
        r0 = pl.multiple_of(b * tq, tq)
        acc = jnp.dot(p_slots[slot][...], v1_ref[...], preferred_element_type=F32)
        o_ref[pl.ds(r0, tq), :] = (acc[:, :dh] / acc[:, dh:dh + 1]).astype(o_ref.dtype)

    _pipeline_blocks(q_ref.shape[0] // tq, len(p_slots), score_stage, value_stage)


def _attn_c(z, zc, *, batch, seq_len, n_ctx, n_heads, col_k, col_v):
    g = n_heads // C_KV_HEADS
    tq, depth = 256, 4
    n_keys = seq_len + n_ctx
    return pl.pallas_call(
        functools.partial(_attn_c_kernel, tq=tq, group=g),
        out_shape=jax.ShapeDtypeStruct((batch * seq_len, n_heads * HEAD_DIM), BF16),
        grid=(batch, n_heads),
        scratch_shapes=[pltpu.VMEM((n_keys, LANES), BF16), pltpu.VMEM((n_keys, 2 * LANES), BF16)]
        + [pltpu.VMEM((tq, n_keys), BF16)] * depth,
        in_specs=[
            pl.BlockSpec((seq_len, LANES), lambda b, h: (b, h)),
            pl.BlockSpec((seq_len, LANES), lambda b, h: (b, col_k + h // g)),
            pl.BlockSpec((seq_len, LANES), lambda b, h: (b, col_v + h // g)),
            pl.BlockSpec((n_ctx, LANES), lambda b, h: (b, col_k + h // g)),
            pl.BlockSpec((n_ctx, LANES), lambda b, h: (b, col_v + h // g)),
        ],
        out_specs=pl.BlockSpec((seq_len, LANES), lambda b, h: (b, h)),
        compiler_params=_params(("arbitrary",) * 2, 4 * seq_len * LANES * 2, 2 * n_ctx * LANES * 2,
                                (3 * LANES + depth * tq) * n_keys, 3 * tq * n_keys * 4 // 2),
        name="attn_gqa",
    )(z, z, z, zc, zc)


def _outproj_kernel(*refs, n_lhs):
    lhs = refs[:n_lhs]
    w_ref, res_ref, gate_ref, o_ref = refs[n_lhs:]
    acc = None
    k0 = 0
    for a_ref in lhs:
        kw = a_ref.shape[1]
        part = jnp.dot(a_ref[...], w_ref[k0:k0 + kw, :], preferred_element_type=F32)
        acc = part if acc is None else acc + part
        k0 += kw
    o_ref[...] = res_ref[...] + gate_ref[0] * acc


def _out_proj(lhs_list, w_bf16, res2d, gate, *, tm, rows_per_mod):
    rows, n = res2d.shape
    kdim = w_bf16.shape[0]
    tn = 512
    in_specs = [pl.BlockSpec((tm, a.shape[1]), lambda i, j: (i, 0)) for a in lhs_list]
    in_specs += [
        pl.BlockSpec((kdim, tn), lambda i, j: (0, j)),
        pl.BlockSpec((tm, tn), lambda i, j: (i, j)),
        pl.BlockSpec((1, 1, tn), lambda i, j: ((i * tm) // rows_per_mod, 0, j)),
    ]
    return pl.pallas_call(
        functools.partial(_outproj_kernel, n_lhs=len(lhs_list)),
        out_shape=jax.ShapeDtypeStruct((rows, n), F32),
        grid=(rows // tm, n // tn),
        in_specs=in_specs,
        out_specs=pl.BlockSpec((tm, tn), lambda i, j: (i, j)),
        compiler_params=_params(("arbitrary", "arbitrary"), tm * kdim * 2, kdim * tn * 2, 2 * tm * tn * 4),
        name="out_proj",
    )(*lhs_list, w_bf16, res2d, gate)


def _ffn_kernel(x_ref, g_ref, sh_ref, sc_ref, gate_ref, wg_ref, wu_ref, wd_ref, o_ref, h_ref, acc_ref):
    f = pl.program_id(1)

    @pl.when(f == 0)
    def _():
        h_ref[...] = _rms_mod(x_ref[...], g_ref[...], sh_ref[0], sc_ref[0]).astype(BF16)
        acc_ref[...] = jnp.zeros_like(acc_ref)

    h = h_ref[...]
    g_ = jnp.dot(h, wg_ref[...], preferred_element_type=F32)
    u_ = jnp.dot(h, wu_ref[...], preferred_element_type=F32)
    a = (g_ * jax.nn.sigmoid(g_) * u_).astype(BF16)
    acc_ref[...] += jnp.dot(a, wd_ref[...], preferred_element_type=F32)

    @pl.when(f == pl.num_programs(1) - 1)
    def _():
        o_ref[...] = x_ref[...] + gate_ref[0] * acc_ref[...]


def _dense_ffn(x2d, norm_g, shift, scale, gate, w_in_bf16, w_down_bf16, *, tm, rows_per_mod):
    rows, d = x2d.shape
    d_ff = w_down_bf16.shape[0]
    tf = 512
    nf = d_ff // tf
    mod_spec = pl.BlockSpec((1, 1, d), lambda i, f: ((i * tm) // rows_per_mod, 0, 0))
    return pl.pallas_call(
        _ffn_kernel,
        out_shape=jax.ShapeDtypeStruct((rows, d), F32),
        grid=(rows // tm, nf),
        in_specs=[
            pl.BlockSpec((tm, d), lambda i, f: (i, 0)),
            pl.BlockSpec((1, d), lambda i, f: (0, 0)),
            mod_spec, mod_spec, mod_spec,
            pl.BlockSpec((d, tf), lambda i, f: (0, f)),
            pl.BlockSpec((d, tf), lambda i, f: (0, nf + f)),
            pl.BlockSpec((tf, d), lambda i, f: (f, 0)),
        ],
        out_specs=pl.BlockSpec((tm, d), lambda i, f: (i, 0)),
        scratch_shapes=[pltpu.VMEM((tm, d), BF16), pltpu.VMEM((tm, d), F32)],
        compiler_params=_params(("arbitrary", "arbitrary"), 2 * tm * d * 4, 3 * d * tf * 2, tm * d * 3),
        name="dense_swiglu",
    )(x2d, norm_g.reshape(1, d), shift, scale, gate, w_in_bf16, w_in_bf16, w_down_bf16)


def _split3(v):
    hi = v.astype(BF16)
    r = v - hi.astype(F32)
    mid = r.astype(BF16)
    lo = (r - mid.astype(F32)).astype(BF16)
    return hi, mid, lo


def _router_kernel(x_ref, g_ref, sh_ref, sc_ref, wr_ref, h_ref, ids_ref, gates_ref, cnt_ref, carry_ref):
    i = pl.program_id(0)
    tm = x_ref.shape[0]

    @pl.when(i == 0)
    def _():
        carry_ref[...] = jnp.zeros_like(carry_ref)

    h = _rms_mod(x_ref[...], g_ref[...], sh_ref[0], sc_ref[0])
    h_ref[...] = h
    h0, h1, h2 = _split3(h)
    w0, w1, w2 = _split3(wr_ref[...])
    dot = functools.partial(jnp.dot, preferred_element_type=F32)
    logits = (dot(h0, w0) + (dot(h0, w1) + dot(h1, w0))
              + (dot(h0, w2) + dot(h1, w1) + dot(h2, w0)))
    lane = lax.broadcasted_iota(jnp.int32, (tm, LANES), 1)
    logits = jnp.where(lane < N_EXPERTS, logits, -jnp.inf)
    v0 = jnp.max(logits, axis=-1, keepdims=True)
    i0 = jnp.min(jnp.where(logits == v0, lane, LANES), axis=-1, keepdims=True)
    rest = jnp.where(lane == i0, -jnp.inf, logits)
    v1 = jnp.max(rest, axis=-1, keepdims=True)
    i1 = jnp.min(jnp.where(rest == v1, lane, LANES), axis=-1, keepdims=True)
    e1 = jnp.exp(v1 - v0)
    g0 = 1.0 / (1.0 + e1)
    g1 = e1 / (1.0 + e1)

    sel = (lane == i0) | (lane == i1)
    row = lax.broadcasted_iota(jnp.int32, (tm, tm), 0)
    col = lax.broadcasted_iota(jnp.int32, (tm, tm), 1)
    tri = (col < row).astype(BF16)
    before = jnp.dot(tri, sel.astype(BF16), preferred_element_type=F32) + carry_ref[...]
    r0 = jnp.sum(jnp.where(lane == i0, before, 0.0), axis=-1, keepdims=True).astype(jnp.int32)
    r1 = jnp.sum(jnp.where(lane == i1, before, 0.0), axis=-1, keepdims=True).astype(jnp.int32)
    total = carry_ref[...] + jnp.sum(sel.astype(F32), axis=0, keepdims=True)
    carry_ref[...] = total

    ids_ref[...] = jnp.where(lane == 0, i0, jnp.where(lane == 1, i1, jnp.where(lane == 2, r0, r1)))
    gates_ref[...] = jnp.where(lane == 0, g0, g1)
    cnt_ref[...] = jnp.broadcast_to(total, cnt_ref.shape).astype(jnp.int32)


def _router(x2d, norm_g, shift, scale, w_router, *, tm, rows_per_mod):
    rows, d = x2d.shape
    wr = jnp.zeros((d, LANES), F32).at[:, :N_EXPERTS].set(w_router)
    mod_spec = pl.BlockSpec((1, 1, d), lambda i: ((i * tm) // rows_per_mod, 0, 0))
    return pl.pallas_call(
        _router_kernel,
        out_shape=(jax.ShapeDtypeStruct((rows, d), F32),
                   jax.ShapeDtypeStruct((rows, LANES), jnp.int32),
                   jax.ShapeDtypeStruct((rows, LANES), F32),
                   jax.ShapeDtypeStruct((8, LANES), jnp.int32)),
        grid=(rows // tm,),
        in_specs=[pl.BlockSpec((tm, d), lambda i: (i, 0)),
                  pl.BlockSpec((1, d), lambda i: (0, 0)),
                  mod_spec, mod_spec,
                  pl.BlockSpec((d, LANES), lambda i: (0, 0))],
        out_specs=(pl.BlockSpec((tm, d), lambda i: (i, 0)),
                   pl.BlockSpec((tm, LANES), lambda i: (i, 0)),
                   pl.BlockSpec((tm, LANES), lambda i: (i, 0)),
                   pl.BlockSpec((8, LANES), lambda i: (0, 0))),
        scratch_shapes=[pltpu.VMEM((1, LANES), F32)],
        compiler_params=_params(("arbitrary",), 2 * tm * d * 4, d * LANES * 4, 2 * tm * LANES * 4),
        name="router",
    )(x2d, norm_g.reshape(1, d), shift, scale, wr)


GATHER_UNROLL = 8


def _start_row_gather(src_hbm, idx_ref, base, dst_ref, sem):
    def body(r8, c):
        for u in range(GATHER_UNROLL):
            r = r8 * GATHER_UNROLL + u
            pltpu.make_async_copy(src_hbm.at[pl.ds(idx_ref[base + r], 1), :],
                                  dst_ref.at[pl.ds(r, 1), :], sem).start()
        return c

    lax.fori_loop(0, dst_ref.shape[0] // GATHER_UNROLL, body, 0)


def _wait_row_gather(src_hbm, dst_ref, sem):
    pltpu.make_async_copy(src_hbm.at[pl.ds(0, dst_ref.shape[0]), :], dst_ref, sem).wait()


def _dispatch_kernel(idx_ref, src_hbm, o_ref, buf_ref, sem):
    tr = o_ref.shape[0]
    i, n = pl.program_id(0), pl.num_programs(0)
    slot = i & 1

    @pl.when(i == 0)
    def _():
        _start_row_gather(src_hbm, idx_ref, 0, buf_ref.at[0], sem.at[0])

    @pl.when(i + 1 < n)
    def _():
        _start_row_gather(src_hbm, idx_ref, (i + 1) * tr, buf_ref.at[1 - slot], sem.at[1 - slot])

    _wait_row_gather(src_hbm, buf_ref.at[slot], sem.at[slot])
    o_ref[...] = buf_ref[slot].astype(o_ref.dtype)


def _dispatch(h2d, row_tok, *, tr):
    n_rows = row_tok.shape[0]
    d = h2d.shape[1]
    grid_spec = pltpu.PrefetchScalarGridSpec(
        num_scalar_prefetch=1,
        grid=(n_rows // tr,),
        in_specs=[pl.BlockSpec(memory_space=pl.ANY)],
        out_specs=pl.BlockSpec((tr, d), lambda i, idx: (i, 0)),
        scratch_shapes=[pltpu.VMEM((2, tr, d), F32), pltpu.SemaphoreType.DMA((2,))],
    )
    return pl.pallas_call(
        _dispatch_kernel,
        out_shape=jax.ShapeDtypeStruct((n_rows, d), BF16),
        grid_spec=grid_spec,
        compiler_params=_params(("arbitrary",), tr * d * 2, tr * d * 4),
        name="expert_dispatch",
    )(row_tok, h2d)


def _combine_kernel(d0_ref, d1_ref, ys_hbm, x_ref, gate_ref, rg_ref, o_ref, a_ref, b_ref, sem):
    tt = x_ref.shape[0]
    i, n = pl.program_id(0), pl.num_programs(0)
    slot = i & 1

    def start(tile, s):
        _start_row_gather(ys_hbm, d0_ref, tile * tt, a_ref.at[s], sem.at[s, 0])
        _start_row_gather(ys_hbm, d1_ref, tile * tt, b_ref.at[s], sem.at[s, 1])

    @pl.when(i == 0)
    def _():
        start(0, 0)

    @pl.when(i + 1 < n)
    def _():
        start(i + 1, 1 - slot)

    _wait_row_gather(ys_hbm, a_ref.at[slot], sem.at[slot, 0])
    _wait_row_gather(ys_hbm, b_ref.at[slot], sem.at[slot, 1])
    rg = rg_ref[...]
    mix = rg[:, 0:1] * a_ref[slot] + rg[:, 1:2] * b_ref[slot]
    o_ref[...] = x_ref[...] + gate_ref[0] * mix


def _combine(ys, dest0, dest1, x2d, gate, row_gates, *, tt, rows_per_mod):
    rows, d = x2d.shape
    grid_spec = pltpu.PrefetchScalarGridSpec(
        num_scalar_prefetch=2,
        grid=(rows // tt,),
        in_specs=[pl.BlockSpec(memory_space=pl.ANY),
                  pl.BlockSpec((tt, d), lambda i, a, b: (i, 0)),
                  pl.BlockSpec((1, 1, d), lambda i, a, b: ((i * tt) // rows_per_mod, 0, 0)),
                  pl.BlockSpec((tt, LANES), lambda i, a, b: (i, 0))],
        out_specs=pl.BlockSpec((tt, d), lambda i, a, b: (i, 0)),
        scratch_shapes=[pltpu.VMEM((2, tt, d), F32), pltpu.VMEM((2, tt, d), F32),
                        pltpu.SemaphoreType.DMA((2, 2))],
    )
    return pl.pallas_call(
        _combine_kernel,
        out_shape=jax.ShapeDtypeStruct((rows, d), F32),
        grid_spec=grid_spec,
        compiler_params=_params(("arbitrary",), 2 * tt * d * 4, 2 * tt * d * 4),
        name="expert_combine",
    )(dest0, dest1, ys, x2d, gate, row_gates)


def _stream_group_weights(sched, copies, cast):
    te_ref, first_ref, grp_ref, ngrp_ref, nxt_ref = sched
    j, r = pl.program_id(0), pl.program_id(1)
    nj = pl.num_programs(0)

    @pl.when(first_ref[r] == 1)
    def _():
        g = grp_ref[r]
        ng = ngrp_ref[0]
        slot = (j * ng + g) & 1

        @pl.when((j == 0) & (g == 0))
        def _():
            for cp in copies(te_ref[r], j, slot):
                cp.start()

        for cp in copies(te_ref[r], j, slot):
            cp.wait()
        last = g == ng - 1

        @pl.when(jnp.logical_not(last & (j == nj - 1)))
        def _():
            for cp in copies(nxt_ref[r], jnp.where(last, j + 1, j), 1 - slot):
                cp.start()

        cast(slot)


def _gmm_up_kernel(te_ref, first_ref, grp_ref, ngrp_ref, nxt_ref, nv_ref, x_ref, w_hbm, o_ref,
                   stage_ref, wbf_ref, sem):
    tn = o_ref.shape[1]
    nj = pl.num_programs(0)

    def copies(e, j, slot):
        return [pltpu.make_async_copy(w_hbm.at[e, :, pl.ds(pl.multiple_of((half * nj + j) * tn, tn), tn)],
                                      stage_ref.at[slot, half], sem.at[slot, half]) for half in range(2)]

    def cast(slot):
        wbf_ref[0] = stage_ref[slot, 0].astype(BF16)
        wbf_ref[1] = stage_ref[slot, 1].astype(BF16)

    _stream_group_weights((te_ref, first_ref, grp_ref, ngrp_ref, nxt_ref), copies, cast)
    valid = pl.program_id(1) < nv_ref[0]

    @pl.when(valid)
    def _():
        x = x_ref[...]
        g_ = jnp.dot(x, wbf_ref[0], preferred_element_type=F32)
        u_ = jnp.dot(x, wbf_ref[1], preferred_element_type=F32)
        o_ref[...] = (g_ * jax.nn.sigmoid(g_) * u_).astype(o_ref.dtype)

    @pl.when(jnp.logical_not(valid))
    def _():
        o_ref[...] = jnp.zeros_like(o_ref)


def _gmm_down_kernel(te_ref, first_ref, grp_ref, ngrp_ref, nxt_ref, nv_ref, h_ref, w_hbm, o_ref,
                     stage_ref, wbf_ref, sem):
    tn = o_ref.shape[1]

    def copies(e, j, slot):
        return [pltpu.make_async_copy(w_hbm.at[e, :, pl.ds(pl.multiple_of(j * tn, tn), tn)],
                                      stage_ref.at[slot], sem.at[slot])]

    def cast(slot):
        wbf_ref[...] = stage_ref[slot].astype(BF16)

    _stream_group_weights((te_ref, first_ref, grp_ref, ngrp_ref, nxt_ref), copies, cast)
    valid = pl.program_id(1) < nv_ref[0]

    @pl.when(valid)
    def _():
        o_ref[...] = jnp.dot(h_ref[...], wbf_ref[...], preferred_element_type=F32)

    @pl.when(jnp.logical_not(valid))
    def _():
        o_ref[...] = jnp.zeros_like(o_ref)


def _expert_ffn(xs, w_in, w_down, sched, n_valid, *, tr):
    n_rows, d = xs.shape
    d_ff = w_down.shape[1]
    n_tiles = n_rows // tr
    n_sched = len(sched) + 1
    tn1 = 1024
    nj1 = d_ff // tn1

    def row_map(*args):
        r, nv = args[1], args[-1]
        return (jnp.minimum(r, nv[0] - 1), 0)

    def out_map(*args):
        return (args[1], args[0])

    up_spec = pltpu.PrefetchScalarGridSpec(
        num_scalar_prefetch=n_sched,
        grid=(nj1, n_tiles),
        in_specs=[pl.BlockSpec((tr, d), row_map), pl.BlockSpec(memory_space=pl.ANY)],
        out_specs=pl.BlockSpec((tr, tn1), out_map),
        scratch_shapes=[pltpu.VMEM((2, 2, d, tn1), F32), pltpu.VMEM((2, d, tn1), BF16),
                        pltpu.SemaphoreType.DMA((2, 2))],
    )
    stage1 = 4 * d * tn1 * 4
    hidden = pl.pallas_call(
        _gmm_up_kernel,
        out_shape=jax.ShapeDtypeStruct((n_rows, d_ff), BF16),
        grid_spec=up_spec,
        compiler_params=pltpu.CompilerParams(
            dimension_semantics=("arbitrary", "arbitrary"),
            vmem_limit_bytes=stage1 + 2 * d * tn1 * 2 + 2 * (tr * d * 2 + tr * tn1 * 2) + VMEM_TEMP_ALLOWANCE // 2),
        name="expert_up",
    )(*sched, n_valid, xs, w_in)

    tn2 = 512
    down_spec = pltpu.PrefetchScalarGridSpec(
        num_scalar_prefetch=n_sched,
        grid=(d // tn2, n_tiles),
        in_specs=[pl.BlockSpec((tr, d_ff), row_map), pl.BlockSpec(memory_space=pl.ANY)],
        out_specs=pl.BlockSpec((tr, tn2), out_map),
        scratch_shapes=[pltpu.VMEM((2, d_ff, tn2), F32), pltpu.VMEM((d_ff, tn2), BF16),
                        pltpu.SemaphoreType.DMA((2,))],
    )
    stage2 = 2 * d_ff * tn2 * 4
    return pl.pallas_call(
        _gmm_down_kernel,
        out_shape=jax.ShapeDtypeStruct((n_rows, d), F32),
        grid_spec=down_spec,
        compiler_params=pltpu.CompilerParams(
            dimension_semantics=("arbitrary", "arbitrary"),
            vmem_limit_bytes=stage2 + d_ff * tn2 * 2 + 2 * (tr * d_ff * 2 + tr * tn2 * 4) + VMEM_TEMP_ALLOWANCE // 2),
        name="expert_down",
    )(*sched, n_valid, hidden, w_down)


def _moe(x2d, norm_g, shift, scale, gate, w_router, w_in, w_down, *, rows_per_mod):
    rows, d = x2d.shape
    tr = 256
    h2d, ids, gates, counts = _router(x2d, norm_g, shift, scale, w_router, tm=512, rows_per_mod=rows_per_mod)
    counts = counts[0, :N_EXPERTS]
    padded = (counts + tr - 1) // tr * tr
    pad_end = jnp.cumsum(padded)
    pad_start = pad_end - padded
    dest0 = pad_start[ids[:, 0]] + ids[:, 2]
    dest1 = pad_start[ids[:, 1]] + ids[:, 3]
    n_rows = 2 * rows + N_EXPERTS * tr
    n_tiles = n_rows // tr
    tok = jnp.arange(rows, dtype=jnp.int32)
    row_tok = jnp.zeros((n_rows,), jnp.int32).at[dest0].set(tok).at[dest1].set(tok)
    tile_expert = jnp.minimum(
        jnp.sum(jnp.arange(n_tiles, dtype=jnp.int32)[:, None] * tr >= pad_end[None, :], axis=1),
        N_EXPERTS - 1).astype(jnp.int32)
    n_valid = (pad_end[-1:] // tr).astype(jnp.int32)
    tile_id = jnp.arange(n_tiles, dtype=jnp.int32)
    prev_expert = jnp.concatenate([jnp.full((1,), -1, jnp.int32), tile_expert[:-1]])
    first = ((tile_id < n_valid[0]) & (tile_expert != prev_expert)).astype(jnp.int32)
    grp = jnp.cumsum(first).astype(jnp.int32) - 1
    n_grp = jnp.sum(first, keepdims=True).astype(jnp.int32)
    expert_id = jnp.arange(N_EXPERTS, dtype=jnp.int32)
    grp_expert = jnp.sort(jnp.where(counts > 0, expert_id, N_EXPERTS))
    nxt = grp_expert[(grp + 1) % n_grp[0]].astype(jnp.int32)
    sched = (tile_expert, first, grp, n_grp, nxt)

    xs = _dispatch(h2d, row_tok, tr=tr)
    ys = _expert_ffn(xs, w_in, w_down, sched, n_valid, tr=tr)
    return _combine(ys, dest0, dest1, x2d, gate, gates, tt=256, rows_per_mod=rows_per_mod)


def _rope_tables(seq_len):
    t = np.arange(seq_len)
    row = (t // GRID_W).astype(np.float64)[:, None]
    col = (t % GRID_W).astype(np.float64)[:, None]
    inv = ROPE_THETA ** (-np.arange(0, ROT_AXIS, 2, dtype=np.float64) / ROT_AXIS)
    ar, ac = row * inv, col * inv
    zero = np.zeros_like(ar)
    cos = np.concatenate([np.cos(ar), np.cos(ar), np.cos(ac), np.cos(ac)], axis=-1)
    s_next = np.concatenate([-np.sin(ar), zero, -np.sin(ac), zero], axis=-1)
    s_prev = np.concatenate([zero, np.sin(ar), zero, np.sin(ac)], axis=-1)
    return tuple(jnp.asarray(a.astype(np.float32)) for a in (cos, s_next, s_prev))


def _mod_vectors(c, c_ctx, w_mod, b_mod):
    batch, d = c.shape
    cvec = jnp.zeros((8, d), F32).at[:batch].set(c).at[batch].set(c_ctx)
    m = _modulation(jnp.concatenate([cvec, cvec], axis=0), w_mod, b_mod)
    lat = m[:batch].reshape(batch, N_MOD, 1, d)
    ctx = m[batch].reshape(N_MOD, 1, 1, d)
    return [lat[:, k] for k in range(N_MOD)], [ctx[k] for k in range(N_MOD)]


def kernel(x, c, ctx, c_ctx, e_norm1, e_norm2, e_w_mod, e_b_mod, e_w_in, e_w_out, e_a_qnorm, e_a_knorm,
           e_a_sink, e_b_qnorm, e_b_knorm, e_b_lam_q1, e_b_lam_k1, e_b_lam_q2, e_b_lam_k2, e_b_subln,
           e_ffn_w_in, e_ffn_w_down, o_norm1, o_norm2, o_w_mod, o_b_mod, o_w_in, o_w_out, o_c_qnorm,
           o_c_knorm, o_router, o_exp_w_in, o_exp_w_down):
    batch, seq_len, d = x.shape
    n_ctx = ctx.shape[1]
    qk_scale = HEAD_DIM ** -0.5 * LOG2_E
    tables = _rope_tables(seq_len)
    ctx_tables = tuple(t[:n_ctx] for t in tables)
    x2d = x.reshape(batch * seq_len, d)
    xc2d = ctx.reshape(batch * n_ctx, d)
    tm = 1024

    (sh1, sc1, g1, sh2, sc2, g2), (csh1, csc1, cg1, csh2, csc2, cg2) = _mod_vectors(
        c, c_ctx, e_w_mod[0], e_b_mod[0])
    a_q, a_kv = A_HEADS * HEAD_DIM, A_KV_HEADS * HEAD_DIM
    b_qk, b_v = B_HEADS * 2 * HEAD_DIM, B_HEADS * 2 * HEAD_DIM
    ones = lambda n: jnp.ones((n,), F32)
    tile = lambda v, n: jnp.tile(v, n // HEAD_DIM)
    col_gain = jnp.concatenate([tile(e_a_qnorm[0] * qk_scale, a_q), tile(e_a_knorm[0], a_kv), ones(a_kv),
                                tile(e_b_qnorm[0] * qk_scale, b_qk), tile(e_b_knorm[0], b_qk), ones(b_v)])
    kinds = lambda *pairs: jnp.concatenate([jnp.full((n // HEAD_DIM,), k, jnp.int32) for k, n in pairs])
    col_kind = kinds((1, a_q), (1, a_kv), (0, a_kv), (1, b_qk), (1, b_qk), (0, b_v))
    w_in0 = e_w_in[0].astype(BF16)
    z = _in_proj(x2d, e_norm1[0], sh1, sc1, w_in0, col_gain, col_kind, tables,
                 tm=tm, rows_per_mod=seq_len, rope=True)
    zc = _in_proj(xc2d, e_norm1[0], csh1, csc1, w_in0, col_gain, col_kind, ctx_tables,
                  tm=n_ctx, rows_per_mod=batch * n_ctx, rope=False)

    ca_q, ca_k, ca_v = 0, a_q // LANES, (a_q + a_kv) // LANES
    cb_q = (a_q + 2 * a_kv) // LANES
    cb_k, cb_v = cb_q + b_qk // LANES, cb_q + 2 * b_qk // LANES
    lam_init = 0.8 - 0.6 * math.exp(-0.3 * 0)
    dims = dict(batch=batch, seq_len=seq_len, n_ctx=n_ctx)
    attn_a = functools.partial(_attn_a, z, zc, e_a_sink[0], col_q=ca_q, col_k=ca_k, col_v=ca_v, **dims)
    attn_b = functools.partial(_attn_b, z, zc, e_b_lam_q1[0], e_b_lam_k1[0], e_b_lam_q2[0], e_b_lam_k2[0],
                               e_b_subln[0], col_q=cb_q, col_k=cb_k, col_v=cb_v, lam_init=lam_init, **dims)
    w_out0 = e_w_out[0].astype(BF16)
    x2d = _out_proj([attn_a(q_from_ctx=False), attn_b(q_from_ctx=False)], w_out0, x2d, g1,
                    tm=tm, rows_per_mod=seq_len)
    xc2d = _out_proj([attn_a(q_from_ctx=True), attn_b(q_from_ctx=True)], w_out0, xc2d, cg1,
                     tm=n_ctx, rows_per_mod=batch * n_ctx)
    ffn_in, ffn_down = e_ffn_w_in[0].astype(BF16), e_ffn_w_down[0].astype(BF16)
    x2d = _dense_ffn(x2d, e_norm2[0], sh2, sc2, g2, ffn_in, ffn_down, tm=512, rows_per_mod=seq_len)
    xc2d = _dense_ffn(xc2d, e_norm2[0], csh2, csc2, cg2, ffn_in, ffn_down, tm=n_ctx,
                      rows_per_mod=batch * n_ctx)

    (sh1, sc1, g1, sh2, sc2, g2), (csh1, csc1, _, _, _, _) = _mod_vectors(c, c_ctx, o_w_mod[0], o_b_mod[0])
    c_q = d
    c_kv = C_KV_HEADS * HEAD_DIM
    col_gain = jnp.concatenate([tile(o_c_qnorm[0] * qk_scale, c_q), tile(o_c_knorm[0], c_kv), ones(c_kv)])
    col_kind = kinds((1, c_q), (1, c_kv), (0, c_kv))
    w_in1 = o_w_in[0].astype(BF16)
    z = _in_proj(x2d, o_norm1[0], sh1, sc1, w_in1, col_gain, col_kind, tables,
                 tm=tm, rows_per_mod=seq_len, rope=True)
    zc = _in_proj(xc2d, o_norm1[0], csh1, csc1, w_in1, col_gain, col_kind, ctx_tables,
                  tm=n_ctx, rows_per_mod=batch * n_ctx, rope=False)
    o = _attn_c(z, zc, n_heads=c_q // HEAD_DIM, col_k=c_q // LANES, col_v=(c_q + c_kv) // LANES, **dims)
    x2d = _out_proj([o], o_w_out[0].astype(BF16), x2d, g1, tm=tm, rows_per_mod=seq_len)
    x2d = _moe(x2d, o_norm2[0], sh2, sc2, g2, o_router[0], o_exp_w_in.reshape(o_exp_w_in.shape[1:]),
               o_exp_w_down.reshape(o_exp_w_down.shape[1:]), rows_per_mod=seq_len)
    return x2d.reshape(batch, seq_len, d)
```

```python
import functools
import math

import jax
import jax.numpy as jnp
import numpy as np
from jax import lax
from jax.experimental import pallas as pl
from jax.experimental.pallas import tpu as pltpu

F32 = jnp.float32
BF16 = jnp.bfloat16

HEAD_DIM = 128
GRID_W = 64
ROT_AXIS = HEAD_DIM // 2
ROPE_THETA = 10000.0
EPS = 1e-6
NEG_INF = -1e30
LOG2_E = math.log2(math.e)
N_MOD = 6
WINDOW = 128
A_HEADS, A_KV_HEADS = 8, 2
B_HEADS = 4
C_KV_HEADS = 4
N_EXPERTS = 8
LANES = 128
V7X_VMEM_BYTES = 64 * 1024 * 1024
VMEM_TEMP_ALLOWANCE = 16 * 1024 * 1024


def _vmem_limit(*block_bytes):
    need = 2 * sum(block_bytes) + VMEM_TEMP_ALLOWANCE
    return int(min(need, V7X_VMEM_BYTES - 6 * 1024 * 1024))


def _params(sem, *block_bytes):
    return pltpu.CompilerParams(dimension_semantics=sem, vmem_limit_bytes=_vmem_limit(*block_bytes))


def _nt_dot(a, b):
    return lax.dot_general(a, b, (((1,), (1,)), ((), ())), preferred_element_type=F32)


def _rms_mod(x, g, shift, scale):
    ms = jnp.mean(x * x, axis=-1, keepdims=True)
    y = x * lax.rsqrt(ms + EPS) * g
    return y * (1.0 + scale) + shift


def _mod_kernel(c_ref, w_ref, b_ref, o_ref):
    c = c_ref[...]
    s = c * jax.nn.sigmoid(c)
    s_hi = s.astype(BF16).astype(F32)
    top = lax.broadcasted_iota(jnp.int32, s.shape, 0) < 8
    lhs = jnp.where(top, s_hi, s - s_hi).astype(BF16)
    acc = jnp.dot(lhs, w_ref[...].astype(BF16), preferred_element_type=F32)
    o_ref[...] = acc[0:8] + acc[8:16] + b_ref[...]


def _modulation(cvec, w_mod, b_mod):
    d, n = w_mod.shape
    tn = 1024
    return pl.pallas_call(
        _mod_kernel,
        out_shape=jax.ShapeDtypeStruct((8, n), F32),
        grid=(n // tn,),
        in_specs=[pl.BlockSpec((16, d), lambda j: (0, 0)),
                  pl.BlockSpec((d, tn), lambda j: (0, j)),
                  pl.BlockSpec((1, tn), lambda j: (0, j))],
        out_specs=pl.BlockSpec((8, tn), lambda j: (0, j)),
        compiler_params=_params(("arbitrary",), d * tn * 4, d * tn * 2),
        name="modulation",
    )(cvec, w_mod, b_mod.reshape(1, n))


def _inproj_kernel(kind_ref, x_ref, g_ref, sh_ref, sc_ref, w_ref, gc_ref, cos_ref, sa_ref, sb_ref,
                   o_ref, h_ref, *, rope):
    j = pl.program_id(1)
    tn = o_ref.shape[1]
    nch = tn // LANES

    @pl.when(j == 0)
    def _():
        h_ref[...] = _rms_mod(x_ref[...], g_ref[...], sh_ref[0], sc_ref[0]).astype(BF16)

    acc = jnp.dot(h_ref[...], w_ref[...], preferred_element_type=F32)
    for c in range(nch):
        a = acc[:, c * LANES:(c + 1) * LANES]
        kind = kind_ref[j * nch + c]

        @pl.when(kind == 0)
        def _():
            o_ref[:, c * LANES:(c + 1) * LANES] = a.astype(o_ref.dtype)

        @pl.when(kind == 1)
        def _():
            ms = jnp.mean(a * a, axis=-1, keepdims=True)
            y = a * lax.rsqrt(ms + EPS) * gc_ref[:, c * LANES:(c + 1) * LANES]
            if rope:
                y = (y * cos_ref[...] + pltpu.roll(y, LANES - 32, 1) * sa_ref[...]
                     + pltpu.roll(y, 32, 1) * sb_ref[...])
            o_ref[:, c * LANES:(c + 1) * LANES] = y.astype(o_ref.dtype)


def _in_proj(x2d, norm_g, shift, scale, w_bf16, col_gain, col_kind, tables, *, tm, rows_per_mod, rope):
    rows, d = x2d.shape
    n = w_bf16.shape[1]
    tn = 512
    cos_t, sa_t, sb_t = tables
    pos_tiles = cos_t.shape[0] // tm

    grid_spec = pltpu.PrefetchScalarGridSpec(
        num_scalar_prefetch=1,
        grid=(rows // tm, n // tn),
        in_specs=[
            pl.BlockSpec((tm, d), lambda i, j, k: (i, 0)),
            pl.BlockSpec((1, d), lambda i, j, k: (0, 0)),
            pl.BlockSpec((1, 1, d), lambda i, j, k: ((i * tm) // rows_per_mod, 0, 0)),
            pl.BlockSpec((1, 1, d), lambda i, j, k: ((i * tm) // rows_per_mod, 0, 0)),
            pl.BlockSpec((d, tn), lambda i, j, k: (0, j)),
            pl.BlockSpec((1, tn), lambda i, j, k: (0, j)),
            pl.BlockSpec((tm, LANES), lambda i, j, k: (i % pos_tiles, 0)),
            pl.BlockSpec((tm, LANES), lambda i, j, k: (i % pos_tiles, 0)),
            pl.BlockSpec((tm, LANES), lambda i, j, k: (i % pos_tiles, 0)),
        ],
        out_specs=pl.BlockSpec((tm, tn), lambda i, j, k: (i, j)),
        scratch_shapes=[pltpu.VMEM((tm, d), BF16)],
    )
    return pl.pallas_call(
        functools.partial(_inproj_kernel, rope=rope),
        out_shape=jax.ShapeDtypeStruct((rows, n), BF16),
        grid_spec=grid_spec,
        compiler_params=_params(("arbitrary", "arbitrary"), tm * d * 4, d * tn * 2, tm * tn * 2,
                                3 * tm * LANES * 4, tm * d),
        name="in_proj_rope" if rope else "in_proj_ctx",
    )(col_kind, x2d, norm_g.reshape(1, d), shift, scale, w_bf16, col_gain.reshape(1, n), cos_t, sa_t, sb_t)


def _window_mask(q0, ws, tq, nk):
    qpos = q0 + lax.broadcasted_iota(jnp.int32, (tq, nk), 0)
    kpos = ws + lax.broadcasted_iota(jnp.int32, (tq, nk), 1)
    return jnp.abs(qpos - kpos) <= WINDOW


def _attn_a_kernel(sink_ref, q_ref, kl_ref, vl_ref, kc_ref, vc_ref, o_ref, vl1_ref, vc1_ref, *slots,
                   tq, seq_len, group):
    depth = len(slots) // 2
    p_slots, sink_slots = slots[:depth], slots[depth:]
    sink = sink_ref[pl.program_id(1)] * LOG2_E
    dh = vc_ref.shape[1]
    nk = tq + 2 * WINDOW if seq_len else 0

    @pl.when(pl.program_id(1) % group == 0)
    def _():
        if seq_len:
            vl1_ref[:, :dh] = vl_ref[...]
            vl1_ref[:, dh:] = jnp.ones_like(vl_ref)
        vc1_ref[:, :dh] = vc_ref[...]
        vc1_ref[:, dh:] = jnp.ones_like(vc_ref)

    def window_start(r0):
        return pl.multiple_of(jnp.clip(r0 - WINDOW, 0, seq_len - nk), WINDOW)

    def score_stage(b, slot):
        r0 = pl.multiple_of(b * tq, tq)
        q = q_ref[pl.ds(r0, tq), :]
        s_ctx = _nt_dot(q, kc_ref[...])
        m = jnp.maximum(jnp.max(s_ctx, axis=-1, keepdims=True), sink)
        if seq_len:
            ws = window_start(r0)
            s_loc = _nt_dot(q, kl_ref[pl.ds(ws, nk), :])
            s_loc = jnp.where(_window_mask(r0, ws, tq, nk), s_loc, NEG_INF)
            m = jnp.maximum(m, jnp.max(s_loc, axis=-1, keepdims=True))
            p_slots[slot][:, :nk] = jnp.exp2(s_loc - m).astype(BF16)
        p_slots[slot][:, nk:] = jnp.exp2(s_ctx - m).astype(BF16)
        sink_slots[slot][...] = jnp.broadcast_to(jnp.exp2(sink - m), sink_slots[slot].shape)

    def value_stage(b, slot):
        r0 = pl.multiple_of(b * tq, tq)
        acc = jnp.dot(p_slots[slot][:, nk:], vc1_ref[...], preferred_element_type=F32)
        if seq_len:
            acc = acc + jnp.dot(p_slots[slot][:, :nk], vl1_ref[pl.ds(window_start(r0), nk), :],
                                preferred_element_type=F32)
        den = acc[:, dh:dh + 1] + sink_slots[slot][:, 0:1]
        o_ref[pl.ds(r0, tq), :] = (acc[:, :dh] / den).astype(o_ref.dtype)

    _pipeline_blocks(q_ref.shape[0] // tq, depth, score_stage, value_stage)


def _attn_a(z, zc, sink, *, batch, seq_len, n_ctx, q_from_ctx, col_q, col_k, col_v):
    g = A_HEADS // A_KV_HEADS
    if q_from_ctx:
        lq, tq, qsrc, depth = n_ctx, n_ctx, zc, 1
    else:
        lq, tq, qsrc, depth = seq_len, 256, z, 4
    lat_len = 0 if q_from_ctx else seq_len
    lat_block = 16 if q_from_ctx else seq_len
    lat_tiles = seq_len // lat_block
    n_keys = n_ctx + (0 if q_from_ctx else tq + 2 * WINDOW)
    grid_spec = pltpu.PrefetchScalarGridSpec(
        num_scalar_prefetch=1,
        grid=(batch, A_HEADS),
        in_specs=[
            pl.BlockSpec((lq, LANES), lambda b, h, s: (b, col_q + h)),
            pl.BlockSpec((lat_block, LANES), lambda b, h, s: (b * lat_tiles, col_k + h // g)),
            pl.BlockSpec((lat_block, LANES), lambda b, h, s: (b * lat_tiles, col_v + h // g)),
            pl.BlockSpec((n_ctx, LANES), lambda b, h, s: (b, col_k + h // g)),
            pl.BlockSpec((n_ctx, LANES), lambda b, h, s: (b, col_v + h // g)),
        ],
        out_specs=pl.BlockSpec((lq, LANES), lambda b, h, s: (b, h)),
        scratch_shapes=[pltpu.VMEM((lat_block, 2 * LANES), BF16), pltpu.VMEM((n_ctx, 2 * LANES), BF16)]
        + [pltpu.VMEM((tq, n_keys), BF16)] * depth + [pltpu.VMEM((tq, LANES), F32)] * depth,
    )
    return pl.pallas_call(
        functools.partial(_attn_a_kernel, tq=tq, seq_len=lat_len, group=g),
        out_shape=jax.ShapeDtypeStruct((batch * lq, A_HEADS * HEAD_DIM), BF16),
        grid_spec=grid_spec,
        compiler_params=_params(("arbitrary",) * 2, 2 * lq * LANES * 2, 4 * lat_block * LANES * 2,
                                4 * n_ctx * LANES * 2, depth * tq * (n_keys + 2 * LANES)),
        name="attn_window_ctx" if q_from_ctx else "attn_window",
    )(sink, qsrc, z, z, zc, zc)


def _pipeline_blocks(n_blocks, depth, score_stage, value_stage):
    score_stage(0, 0)

    def body(u, carry):
        for i in range(depth):
            b = u * depth + i
            score_stage(jnp.minimum(b + 1, n_blocks - 1), (i + 1) % depth)
            value_stage(b, i)
        return carry

    lax.fori_loop(0, n_blocks // depth, body, 0)


def _join_rows(dst_ref, lat_ref, ctx_ref, has_lat):
    n_lat = lat_ref.shape[0] if has_lat else 0
    if has_lat:
        dst_ref[0:n_lat, 0:lat_ref.shape[1]] = lat_ref[...]
    dst_ref[n_lat:n_lat + ctx_ref.shape[0], 0:ctx_ref.shape[1]] = ctx_ref[...]


def _exp2_numerators(q, k_all):
    s = _nt_dot(q, k_all)
    return jnp.exp2(s - jnp.max(s, axis=-1, keepdims=True))


def _attn_b_kernel(q1_ref, q2_ref, k1l_ref, k2l_ref, vl_ref, k1c_ref, k2c_ref, vc_ref,
                   lq1_ref, lk1_ref, lq2_ref, lk2_ref, sub_ref, o_ref, k1_ref, k2_ref, v_ref, *p_slots,
                   tq, has_lat, lam_init):
    lam = (jnp.exp(jnp.sum(lq1_ref[0] * lk1_ref[0], axis=-1, keepdims=True))
           - jnp.exp(jnp.sum(lq2_ref[0] * lk2_ref[0], axis=-1, keepdims=True)) + lam_init)
    _join_rows(k1_ref, k1l_ref, k1c_ref, has_lat)
    _join_rows(k2_ref, k2l_ref, k2c_ref, has_lat)
    _join_rows(v_ref, vl_ref, vc_ref, has_lat)

    def score_stage(b, slot):
        r0 = pl.multiple_of(b * tq, tq)
        p1 = _exp2_numerators(q1_ref[pl.ds(r0, tq), :], k1_ref[...])
        p2 = _exp2_numerators(q2_ref[pl.ds(r0, tq), :], k2_ref[...])
        w1 = 1.0 / jnp.sum(p1, axis=-1, keepdims=True)
        w2 = lam / jnp.sum(p2, axis=-1, keepdims=True)
        p_slots[slot][...] = (p1 * w1 - p2 * w2).astype(BF16)

    def value_stage(b, slot):
        r0 = pl.multiple_of(b * tq, tq)
        o = jnp.dot(p_slots[slot][...], v_ref[...], preferred_element_type=F32)
        ms = jnp.mean(o * o, axis=-1, keepdims=True)
        o = o * lax.rsqrt(ms + EPS) * sub_ref[...] * (1.0 - lam_init)
        o_ref[pl.ds(r0, tq), :] = o.astype(o_ref.dtype)

    _pipeline_blocks(q1_ref.shape[0] // tq, len(p_slots), score_stage, value_stage)


def _attn_b(z, zc, lq1, lk1, lq2, lk2, subln, *, batch, seq_len, n_ctx, q_from_ctx, col_q, col_k, col_v,
            lam_init):
    dv = 2 * HEAD_DIM
    if q_from_ctx:
        lq, bq, tq, qsrc, depth = n_ctx, n_ctx, n_ctx, zc, 1
    else:
        lq, bq, tq, qsrc, depth = seq_len, seq_len, 256, z, 4
    nqb = lq // bq
    lat_block = 16 if q_from_ctx else seq_len
    lat_tiles = seq_len // lat_block
    n_keys = n_ctx if q_from_ctx else seq_len + n_ctx
    lam_spec = pl.BlockSpec((1, 1, HEAD_DIM), lambda b, h, i: (h, 0, 0))
    in_specs = [
        pl.BlockSpec((bq, LANES), lambda b, h, i: (b * nqb + i, col_q + 2 * h)),
        pl.BlockSpec((bq, LANES), lambda b, h, i: (b * nqb + i, col_q + 2 * h + 1)),
        pl.BlockSpec((lat_block, LANES), lambda b, h, i: (b * lat_tiles, col_k + 2 * h)),
        pl.BlockSpec((lat_block, LANES), lambda b, h, i: (b * lat_tiles, col_k + 2 * h + 1)),
        pl.BlockSpec((lat_block, dv), lambda b, h, i: (b * lat_tiles, col_v // 2 + h)),
        pl.BlockSpec((n_ctx, LANES), lambda b, h, i: (b, col_k + 2 * h)),
        pl.BlockSpec((n_ctx, LANES), lambda b, h, i: (b, col_k + 2 * h + 1)),
        pl.BlockSpec((n_ctx, dv), lambda b, h, i: (b, col_v // 2 + h)),
        lam_spec, lam_spec, lam_spec, lam_spec,
        pl.BlockSpec((1, dv), lambda b, h, i: (0, 0)),
    ]
    return pl.pallas_call(
        functools.partial(_attn_b_kernel, tq=tq, has_lat=not q_from_ctx, lam_init=lam_init),
        out_shape=jax.ShapeDtypeStruct((batch * lq, B_HEADS * dv), BF16),
        grid=(batch, B_HEADS, nqb),
        in_specs=in_specs,
        out_specs=pl.BlockSpec((bq, dv), lambda b, h, i: (b * nqb + i, h)),
        scratch_shapes=[pltpu.VMEM((n_keys, LANES), BF16), pltpu.VMEM((n_keys, LANES), BF16),
                        pltpu.VMEM((n_keys, dv), BF16)] + [pltpu.VMEM((tq, n_keys), BF16)] * depth,
        compiler_params=_params(("arbitrary",) * 3, 2 * bq * LANES * 2, 4 * lat_block * LANES * 2,
                                4 * n_ctx * LANES * 2, bq * dv * 2, (4 * LANES + depth * tq) * n_keys,
                                4 * tq * n_keys * 4 // 2),
        name="attn_diff_ctx" if q_from_ctx else "attn_diff",
    )(qsrc, qsrc, z, z, z, zc, zc, zc,
      lq1.reshape(B_HEADS, 1, HEAD_DIM), lk1.reshape(B_HEADS, 1, HEAD_DIM),
      lq2.reshape(B_HEADS, 1, HEAD_DIM), lk2.reshape(B_HEADS, 1, HEAD_DIM), subln.reshape(1, dv))


def _attn_c_kernel(q_ref, kl_ref, vl_ref, kc_ref, vc_ref, o_ref, k_ref, v1_ref, *p_slots, tq, group):
    dh = vl_ref.shape[1]

    @pl.when(pl.program_id(1) % group == 0)
    def _():
        _join_rows(k_ref, kl_ref, kc_ref, True)
        _join_rows(v1_ref, vl_ref, vc_ref, True)
        v1_ref[:, dh:] = jnp.ones((v1_ref.shape[0], v1_ref.shape[1] - dh), v1_ref.dtype)

    def score_stage(b, slot):
        r0 = pl.multiple_of(b * tq, tq)
        p_slots[slot][...] = _exp2_numerators(q_ref[pl.ds(r0, tq), :], k_ref[...]).astype(BF16)

    def value_stage(b, slot):
        r0 = pl.multiple_of(b * tq, tq)
        acc = jnp.dot(p_slots[slot][...], v1_ref[...], preferred_element_type=F32)
        o_ref[pl.ds(r0, tq), :] = (acc[:, :dh] / acc[:, dh:dh + 1]).astype(o_ref.dtype)

    _pipeline_blocks(q_ref.shape[0] // tq, len(p_slots), score_stage, value_stage)


def _attn_c(z, zc, *, batch, seq_len, n_ctx, n_heads, col_k, col_v):
    g = n_heads // C_KV_HEADS
    tq, depth = 256, 4
    n_keys = seq_len + n_ctx
    return pl.pallas_call(
        functools.partial(_attn_c_kernel, tq=tq, group=g),
        out_shape=jax.ShapeDtypeStruct((batch * seq_len, n_heads * HEAD_DIM), BF16),
        grid=(batch, n_heads),
        scratch_shapes=[pltpu.VMEM((n_keys, LANES), BF16), pltpu.VMEM((n_keys, 2 * LANES), BF16)]
        + [pltpu.VMEM((tq, n_keys), BF16)] * depth,
        in_specs=[
            pl.BlockSpec((seq_len, LANES), lambda b, h: (b, h)),
            pl.BlockSpec((seq_len, LANES), lambda b, h: (b, col_k + h // g)),
            pl.BlockSpec((seq_len, LANES), lambda b, h: (b, col_v + h // g)),
            pl.BlockSpec((n_ctx, LANES), lambda b, h: (b, col_k + h // g)),
            pl.BlockSpec((n_ctx, LANES), lambda b, h: (b, col_v + h // g)),
        ],
        out_specs=pl.BlockSpec((seq_len, LANES), lambda b, h: (b, h)),
        compiler_params=_params(("arbitrary",) * 2, 4 * seq_len * LANES * 2, 2 * n_ctx * LANES * 2,
                                (3 * LANES + depth * tq) * n_keys, 3 * tq * n_keys * 4 // 2),
        name="attn_gqa",
    )(z, z, z, zc, zc)


def _outproj_kernel(*refs, n_lhs):
    lhs = refs[:n_lhs]
    w_ref, res_ref, gate_ref, o_ref = refs[n_lhs:]
    acc = None
    k0 = 0
    for a_ref in lhs:
        kw = a_ref.shape[1]
        part = jnp.dot(a_ref[...], w_ref[k0:k0 + kw, :], preferred_element_type=F32)
        acc = part if acc is None else acc + part
        k0 += kw
    o_ref[...] = res_ref[...] + gate_ref[0] * acc


def _out_proj(lhs_list, w_bf16, res2d, gate, *, tm, rows_per_mod):
    rows, n = res2d.shape
    kdim = w_bf16.shape[0]
    tn = 512
    in_specs = [pl.BlockSpec((tm, a.shape[1]), lambda i, j: (i, 0)) for a in lhs_list]
    in_specs += [
        pl.BlockSpec((kdim, tn), lambda i, j: (0, j)),
        pl.BlockSpec((tm, tn), lambda i, j: (i, j)),
        pl.BlockSpec((1, 1, tn), lambda i, j: ((i * tm) // rows_per_mod, 0, j)),
    ]
    return pl.pallas_call(
        functools.partial(_outproj_kernel, n_lhs=len(lhs_list)),
        out_shape=jax.ShapeDtypeStruct((rows, n), F32),
        grid=(rows // tm, n // tn),
        in_specs=in_specs,
        out_specs=pl.BlockSpec((tm, tn), lambda i, j: (i, j)),
        compiler_params=_params(("arbitrary", "arbitrary"), tm * kdim * 2, kdim * tn * 2, 2 * tm * tn * 4),
        name="out_proj",
    )(*lhs_list, w_bf16, res2d, gate)


def _ffn_kernel(x_ref, g_ref, sh_ref, sc_ref, gate_ref, wg_ref, wu_ref, wd_ref, o_ref, h_ref, acc_ref):
    f = pl.program_id(1)

    @pl.when(f == 0)
    def _():
        h_ref[...] = _rms_mod(x_ref[...], g_ref[...], sh_ref[0], sc_ref[0]).astype(BF16)
        acc_ref[...] = jnp.zeros_like(acc_ref)

    h = h_ref[...]
    g_ = jnp.dot(h, wg_ref[...], preferred_element_type=F32)
    u_ = jnp.dot(h, wu_ref[...], preferred_element_type=F32)
    a = (g_ * jax.nn.sigmoid(g_) * u_).astype(BF16)
    acc_ref[...] += jnp.dot(a, wd_ref[...], preferred_element_type=F32)

    @pl.when(f == pl.num_programs(1) - 1)
    def _():
        o_ref[...] = x_ref[...] + gate_ref[0] * acc_ref[...]


def _dense_ffn(x2d, norm_g, shift, scale, gate, w_in_bf16, w_down_bf16, *, tm, rows_per_mod):
    rows, d = x2d.shape
    d_ff = w_down_bf16.shape[0]
    tf = 512
    nf = d_ff // tf
    mod_spec = pl.BlockSpec((1, 1, d), lambda i, f: ((i * tm) // rows_per_mod, 0, 0))
    return pl.pallas_call(
        _ffn_kernel,
        out_shape=jax.ShapeDtypeStruct((rows, d), F32),
        grid=(rows // tm, nf),
        in_specs=[
            pl.BlockSpec((tm, d), lambda i, f: (i, 0)),
            pl.BlockSpec((1, d), lambda i, f: (0, 0)),
            mod_spec, mod_spec, mod_spec,
            pl.BlockSpec((d, tf), lambda i, f: (0, f)),
            pl.BlockSpec((d, tf), lambda i, f: (0, nf + f)),
            pl.BlockSpec((tf, d), lambda i, f: (f, 0)),
        ],
        out_specs=pl.BlockSpec((tm, d), lambda i, f: (i, 0)),
        scratch_shapes=[pltpu.VMEM((tm, d), BF16), pltpu.VMEM((tm, d), F32)],
        compiler_params=_params(("arbitrary", "arbitrary"), 2 * tm * d * 4, 3 * d * tf * 2, tm * d * 3),
        name="dense_swiglu",
    )(x2d, norm_g.reshape(1, d), shift, scale, gate, w_in_bf16, w_in_bf16, w_down_bf16)


def _split3(v):
    hi = v.astype(BF16)
    r = v - hi.astype(F32)
    mid = r.astype(BF16)
    lo = (r - mid.astype(F32)).astype(BF16)
    return hi, mid, lo


def _router_kernel(x_ref, g_ref, sh_ref, sc_ref, wr_ref, h_ref, ids_ref, gates_ref, cnt_ref, carry_ref):
    i = pl.program_id(0)
    tm = x_ref.shape[0]

    @pl.when(i == 0)
    def _():
        carry_ref[...] = jnp.zeros_like(carry_ref)

    h = _rms_mod(x_ref[...], g_ref[...], sh_ref[0], sc_ref[0])
    h_ref[...] = h
    h0, h1, h2 = _split3(h)
    w0, w1, w2 = _split3(wr_ref[...])
    dot = functools.partial(jnp.dot, preferred_element_type=F32)
    logits = (dot(h0, w0) + (dot(h0, w1) + dot(h1, w0))
              + (dot(h0, w2) + dot(h1, w1) + dot(h2, w0)))
    lane = lax.broadcasted_iota(jnp.int32, (tm, LANES), 1)
    logits = jnp.where(lane < N_EXPERTS, logits, -jnp.inf)
    v0 = jnp.max(logits, axis=-1, keepdims=True)
    i0 = jnp.min(jnp.where(logits == v0, lane, LANES), axis=-1, keepdims=True)
    rest = jnp.where(lane == i0, -jnp.inf, logits)
    v1 = jnp.max(rest, axis=-1, keepdims=True)
    i1 = jnp.min(jnp.where(rest == v1, lane, LANES), axis=-1, keepdims=True)
    e1 = jnp.exp(v1 - v0)
    g0 = 1.0 / (1.0 + e1)
    g1 = e1 / (1.0 + e1)

    sel = (lane == i0) | (lane == i1)
    row = lax.broadcasted_iota(jnp.int32, (tm, tm), 0)
    col = lax.broadcasted_iota(jnp.int32, (tm, tm), 1)
    tri = (col < row).astype(BF16)
    before = jnp.dot(tri, sel.astype(BF16), preferred_element_type=F32) + carry_ref[...]
    r0 = jnp.sum(jnp.where(lane == i0, before, 0.0), axis=-1, keepdims=True).astype(jnp.int32)
    r1 = jnp.sum(jnp.where(lane == i1, before, 0.0), axis=-1, keepdims=True).astype(jnp.int32)
    total = carry_ref[...] + jnp.sum(sel.astype(F32), axis=0, keepdims=True)
    carry_ref[...] = total

    ids_ref[...] = jnp.where(lane == 0, i0, jnp.where(lane == 1, i1, jnp.where(lane == 2, r0, r1)))
    gates_ref[...] = jnp.where(lane == 0, g0, g1)
    cnt_ref[...] = jnp.broadcast_to(total, cnt_ref.shape).astype(jnp.int32)


def _router(x2d, norm_g, shift, scale, w_router, *, tm, rows_per_mod):
    rows, d = x2d.shape
    wr = jnp.zeros((d, LANES), F32).at[:, :N_EXPERTS].set(w_router)
    mod_spec = pl.BlockSpec((1, 1, d), lambda i: ((i * tm) // rows_per_mod, 0, 0))
    return pl.pallas_call(
        _router_kernel,
        out_shape=(jax.ShapeDtypeStruct((rows, d), F32),
                   jax.ShapeDtypeStruct((rows, LANES), jnp.int32),
                   jax.ShapeDtypeStruct((rows, LANES), F32),
                   jax.ShapeDtypeStruct((8, LANES), jnp.int32)),
        grid=(rows // tm,),
        in_specs=[pl.BlockSpec((tm, d), lambda i: (i, 0)),
                  pl.BlockSpec((1, d), lambda i: (0, 0)),
                  mod_spec, mod_spec,
                  pl.BlockSpec((d, LANES), lambda i: (0, 0))],
        out_specs=(pl.BlockSpec((tm, d), lambda i: (i, 0)),
                   pl.BlockSpec((tm, LANES), lambda i: (i, 0)),
                   pl.BlockSpec((tm, LANES), lambda i: (i, 0)),
                   pl.BlockSpec((8, LANES), lambda i: (0, 0))),
        scratch_shapes=[pltpu.VMEM((1, LANES), F32)],
        compiler_params=_params(("arbitrary",), 2 * tm * d * 4, d * LANES * 4, 2 * tm * LANES * 4),
        name="router",
    )(x2d, norm_g.reshape(1, d), shift, scale, wr)


GATHER_UNROLL = 8


def _start_row_gather(src_hbm, idx_ref, base, dst_ref, sem):
    def body(r8, c):
        for u in range(GATHER_UNROLL):
            r = r8 * GATHER_UNROLL + u
            pltpu.make_async_copy(src_hbm.at[pl.ds(idx_ref[base + r], 1), :],
                                  dst_ref.at[pl.ds(r, 1), :], sem).start()
        return c

    lax.fori_loop(0, dst_ref.shape[0] // GATHER_UNROLL, body, 0)


def _wait_row_gather(src_hbm, dst_ref, sem):
    pltpu.make_async_copy(src_hbm.at[pl.ds(0, dst_ref.shape[0]), :], dst_ref, sem).wait()


def _dispatch_kernel(idx_ref, src_hbm, o_ref, buf_ref, sem):
    tr = o_ref.shape[0]
    i, n = pl.program_id(0), pl.num_programs(0)
    slot = i & 1

    @pl.when(i == 0)
    def _():
        _start_row_gather(src_hbm, idx_ref, 0, buf_ref.at[0], sem.at[0])

    @pl.when(i + 1 < n)
    def _():
        _start_row_gather(src_hbm, idx_ref, (i + 1) * tr, buf_ref.at[1 - slot], sem.at[1 - slot])

    _wait_row_gather(src_hbm, buf_ref.at[slot], sem.at[slot])
    o_ref[...] = buf_ref[slot].astype(o_ref.dtype)


def _dispatch(h2d, row_tok, *, tr):
    n_rows = row_tok.shape[0]
    d = h2d.shape[1]
    grid_spec = pltpu.PrefetchScalarGridSpec(
        num_scalar_prefetch=1,
        grid=(n_rows // tr,),
        in_specs=[pl.BlockSpec(memory_space=pl.ANY)],
        out_specs=pl.BlockSpec((tr, d), lambda i, idx: (i, 0)),
        scratch_shapes=[pltpu.VMEM((2, tr, d), F32), pltpu.SemaphoreType.DMA((2,))],
    )
    return pl.pallas_call(
        _dispatch_kernel,
        out_shape=jax.ShapeDtypeStruct((n_rows, d), BF16),
        grid_spec=grid_spec,
        compiler_params=_params(("arbitrary",), tr * d * 2, tr * d * 4),
        name="expert_dispatch",
    )(row_tok, h2d)


def _combine_kernel(d0_ref, d1_ref, ys_hbm, x_ref, gate_ref, rg_ref, o_ref, a_ref, b_ref, sem):
    tt = x_ref.shape[0]
    i, n = pl.program_id(0), pl.num_programs(0)
    slot = i & 1

    def start(tile, s):
        _start_row_gather(ys_hbm, d0_ref, tile * tt, a_ref.at[s], sem.at[s, 0])
        _start_row_gather(ys_hbm, d1_ref, tile * tt, b_ref.at[s], sem.at[s, 1])

    @pl.when(i == 0)
    def _():
        start(0, 0)

    @pl.when(i + 1 < n)
    def _():
        start(i + 1, 1 - slot)

    _wait_row_gather(ys_hbm, a_ref.at[slot], sem.at[slot, 0])
    _wait_row_gather(ys_hbm, b_ref.at[slot], sem.at[slot, 1])
    rg = rg_ref[...]
    mix = rg[:, 0:1] * a_ref[slot] + rg[:, 1:2] * b_ref[slot]
    o_ref[...] = x_ref[...] + gate_ref[0] * mix


def _combine(ys, dest0, dest1, x2d, gate, row_gates, *, tt, rows_per_mod):
    rows, d = x2d.shape
    grid_spec = pltpu.PrefetchScalarGridSpec(
        num_scalar_prefetch=2,
        grid=(rows // tt,),
        in_specs=[pl.BlockSpec(memory_space=pl.ANY),
                  pl.BlockSpec((tt, d), lambda i, a, b: (i, 0)),
                  pl.BlockSpec((1, 1, d), lambda i, a, b: ((i * tt) // rows_per_mod, 0, 0)),
                  pl.BlockSpec((tt, LANES), lambda i, a, b: (i, 0))],
        out_specs=pl.BlockSpec((tt, d), lambda i, a, b: (i, 0)),
        scratch_shapes=[pltpu.VMEM((2, tt, d), F32), pltpu.VMEM((2, tt, d), F32),
                        pltpu.SemaphoreType.DMA((2, 2))],
    )
    return pl.pallas_call(
        _combine_kernel,
        out_shape=jax.ShapeDtypeStruct((rows, d), F32),
        grid_spec=grid_spec,
        compiler_params=_params(("arbitrary",), 2 * tt * d * 4, 2 * tt * d * 4),
        name="expert_combine",
    )(dest0, dest1, ys, x2d, gate, row_gates)


WEIGHT_DMA_PRIORITY = 1


def _stream_group_weights(sched, copies, cast):
    te_ref, first_ref, grp_ref, ngrp_ref, nxt_ref = sched
    j, r = pl.program_id(0), pl.program_id(1)
    nj = pl.num_programs(0)

    @pl.when(first_ref[r] == 1)
    def _():
        g = grp_ref[r]
        ng = ngrp_ref[0]
        slot = (j * ng + g) & 1

        @pl.when((j == 0) & (g == 0))
        def _():
            for cp in copies(te_ref[r], j, slot):
                cp.start(priority=WEIGHT_DMA_PRIORITY)

        for cp in copies(te_ref[r], j, slot):
            cp.wait()
        last = g == ng - 1

        @pl.when(jnp.logical_not(last & (j == nj - 1)))
        def _():
            for cp in copies(nxt_ref[r], jnp.where(last, j + 1, j), 1 - slot):
                cp.start(priority=WEIGHT_DMA_PRIORITY)

        cast(slot)


def _gmm_up_kernel(te_ref, first_ref, grp_ref, ngrp_ref, nxt_ref, nv_ref, x_ref, w_hbm, o_ref,
                   stage_ref, wbf_ref, sem):
    tn = o_ref.shape[1]
    nj = pl.num_programs(0)

    def copies(e, j, slot):
        return [pltpu.make_async_copy(w_hbm.at[e, :, pl.ds(pl.multiple_of((half * nj + j) * tn, tn), tn)],
                                      stage_ref.at[slot, half], sem.at[slot, half]) for half in range(2)]

    def cast(slot):
        wbf_ref[0] = stage_ref[slot, 0].astype(BF16)
        wbf_ref[1] = stage_ref[slot, 1].astype(BF16)

    _stream_group_weights((te_ref, first_ref, grp_ref, ngrp_ref, nxt_ref), copies, cast)
    valid = pl.program_id(1) < nv_ref[0]

    @pl.when(valid)
    def _():
        x = x_ref[...]
        g_ = jnp.dot(x, wbf_ref[0], preferred_element_type=F32)
        u_ = jnp.dot(x, wbf_ref[1], preferred_element_type=F32)
        o_ref[...] = (g_ * jax.nn.sigmoid(g_) * u_).astype(o_ref.dtype)

    @pl.when(jnp.logical_not(valid))
    def _():
        o_ref[...] = jnp.zeros_like(o_ref)


def _gmm_down_kernel(te_ref, first_ref, grp_ref, ngrp_ref, nxt_ref, nv_ref, h_ref, w_hbm, o_ref,
                     stage_ref, wbf_ref, sem):
    tn = o_ref.shape[1]

    def copies(e, j, slot):
        return [pltpu.make_async_copy(w_hbm.at[e, :, pl.ds(pl.multiple_of(j * tn, tn), tn)],
                                      stage_ref.at[slot], sem.at[slot])]

    def cast(slot):
        wbf_ref[...] = stage_ref[slot].astype(BF16)

    _stream_group_weights((te_ref, first_ref, grp_ref, ngrp_ref, nxt_ref), copies, cast)
    valid = pl.program_id(1) < nv_ref[0]

    @pl.when(valid)
    def _():
        o_ref[...] = jnp.dot(h_ref[...], wbf_ref[...], preferred_element_type=F32)

    @pl.when(jnp.logical_not(valid))
    def _():
        o_ref[...] = jnp.zeros_like(o_ref)


def _expert_ffn(xs, w_in, w_down, sched, n_valid, *, tr):
    n_rows, d = xs.shape
    d_ff = w_down.shape[1]
    n_tiles = n_rows // tr
    n_sched = len(sched) + 1
    tn1 = 1024
    nj1 = d_ff // tn1

    def row_map(*args):
        r, nv = args[1], args[-1]
        return (jnp.minimum(r, nv[0] - 1), 0)

    def out_map(*args):
        return (args[1], args[0])

    up_spec = pltpu.PrefetchScalarGridSpec(
        num_scalar_prefetch=n_sched,
        grid=(nj1, n_tiles),
        in_specs=[pl.BlockSpec((tr, d), row_map), pl.BlockSpec(memory_space=pl.ANY)],
        out_specs=pl.BlockSpec((tr, tn1), out_map),
        scratch_shapes=[pltpu.VMEM((2, 2, d, tn1), F32), pltpu.VMEM((2, d, tn1), BF16),
                        pltpu.SemaphoreType.DMA((2, 2))],
    )
    stage1 = 4 * d * tn1 * 4
    hidden = pl.pallas_call(
        _gmm_up_kernel,
        out_shape=jax.ShapeDtypeStruct((n_rows, d_ff), BF16),
        grid_spec=up_spec,
        compiler_params=pltpu.CompilerParams(
            dimension_semantics=("arbitrary", "arbitrary"),
            vmem_limit_bytes=stage1 + 2 * d * tn1 * 2 + 2 * (tr * d * 2 + tr * tn1 * 2) + VMEM_TEMP_ALLOWANCE // 2),
        name="expert_up",
    )(*sched, n_valid, xs, w_in)

    tn2 = 512
    down_spec = pltpu.PrefetchScalarGridSpec(
        num_scalar_prefetch=n_sched,
        grid=(d // tn2, n_tiles),
        in_specs=[pl.BlockSpec((tr, d_ff), row_map), pl.BlockSpec(memory_space=pl.ANY)],
        out_specs=pl.BlockSpec((tr, tn2), out_map),
        scratch_shapes=[pltpu.VMEM((2, d_ff, tn2), F32), pltpu.VMEM((d_ff, tn2), BF16),
                        pltpu.SemaphoreType.DMA((2,))],
    )
    stage2 = 2 * d_ff * tn2 * 4
    return pl.pallas_call(
        _gmm_down_kernel,
        out_shape=jax.ShapeDtypeStruct((n_rows, d), F32),
        grid_spec=down_spec,
        compiler_params=pltpu.CompilerParams(
            dimension_semantics=("arbitrary", "arbitrary"),
            vmem_limit_bytes=stage2 + d_ff * tn2 * 2 + 2 * (tr * d_ff * 2 + tr * tn2 * 4) + VMEM_TEMP_ALLOWANCE // 2),
        name="expert_down",
    )(*sched, n_valid, hidden, w_down)


def _moe(x2d, norm_g, shift, scale, gate, w_router, w_in, w_down, *, rows_per_mod):
    rows, d = x2d.shape
    tr = 256
    h2d, ids, gates, counts = _router(x2d, norm_g, shift, scale, w_router, tm=512, rows_per_mod=rows_per_mod)
    counts = counts[0, :N_EXPERTS]
    padded = (counts + tr - 1) // tr * tr
    pad_end = jnp.cumsum(padded)
    pad_start = pad_end - padded
    dest0 = pad_start[ids[:, 0]] + ids[:, 2]
    dest1 = pad_start[ids[:, 1]] + ids[:, 3]
    n_rows = 2 * rows + N_EXPERTS * tr
    n_tiles = n_rows // tr
    tok = jnp.arange(rows, dtype=jnp.int32)
    row_tok = jnp.zeros((n_rows,), jnp.int32).at[jnp.concatenate([dest0, dest1])].set(
        jnp.concatenate([tok, tok]))
    tile_expert = jnp.minimum(
        jnp.sum(jnp.arange(n_tiles, dtype=jnp.int32)[:, None] * tr >= pad_end[None, :], axis=1),
        N_EXPERTS - 1).astype(jnp.int32)
    n_valid = (pad_end[-1:] // tr).astype(jnp.int32)
    tile_id = jnp.arange(n_tiles, dtype=jnp.int32)
    prev_expert = jnp.concatenate([jnp.full((1,), -1, jnp.int32), tile_expert[:-1]])
    first = ((tile_id < n_valid[0]) & (tile_expert != prev_expert)).astype(jnp.int32)
    grp = jnp.cumsum(first).astype(jnp.int32) - 1
    n_grp = jnp.sum(first, keepdims=True).astype(jnp.int32)
    expert_id = jnp.arange(N_EXPERTS, dtype=jnp.int32)
    grp_expert = jnp.sort(jnp.where(counts > 0, expert_id, N_EXPERTS))
    nxt = grp_expert[(grp + 1) % n_grp[0]].astype(jnp.int32)
    sched = (tile_expert, first, grp, n_grp, nxt)

    xs = _dispatch(h2d, row_tok, tr=tr)
    ys = _expert_ffn(xs, w_in, w_down, sched, n_valid, tr=tr)
    return _combine(ys, dest0, dest1, x2d, gate, gates, tt=256, rows_per_mod=rows_per_mod)


def _rope_tables(seq_len):
    t = np.arange(seq_len)
    row = (t // GRID_W).astype(np.float64)[:, None]
    col = (t % GRID_W).astype(np.float64)[:, None]
    inv = ROPE_THETA ** (-np.arange(0, ROT_AXIS, 2, dtype=np.float64) / ROT_AXIS)
    ar, ac = row * inv, col * inv
    zero = np.zeros_like(ar)
    cos = np.concatenate([np.cos(ar), np.cos(ar), np.cos(ac), np.cos(ac)], axis=-1)
    s_next = np.concatenate([-np.sin(ar), zero, -np.sin(ac), zero], axis=-1)
    s_prev = np.concatenate([zero, np.sin(ar), zero, np.sin(ac)], axis=-1)
    return tuple(jnp.asarray(a.astype(np.float32)) for a in (cos, s_next, s_prev))


def _mod_vectors(c, c_ctx, w_mod, b_mod):
    batch, d = c.shape
    cvec = jnp.zeros((8, d), F32).at[:batch].set(c).at[batch].set(c_ctx)
    m = _modulation(jnp.concatenate([cvec, cvec], axis=0), w_mod, b_mod)
    lat = m[:batch].reshape(batch, N_MOD, 1, d)
    ctx = m[batch].reshape(N_MOD, 1, 1, d)
    return [lat[:, k] for k in range(N_MOD)], [ctx[k] for k in range(N_MOD)]


def kernel(x, c, ctx, c_ctx, e_norm1, e_norm2, e_w_mod, e_b_mod, e_w_in, e_w_out, e_a_qnorm, e_a_knorm,
           e_a_sink, e_b_qnorm, e_b_knorm, e_b_lam_q1, e_b_lam_k1, e_b_lam_q2, e_b_lam_k2, e_b_subln,
           e_ffn_w_in, e_ffn_w_down, o_norm1, o_norm2, o_w_mod, o_b_mod, o_w_in, o_w_out, o_c_qnorm,
           o_c_knorm, o_router, o_exp_w_in, o_exp_w_down):
    batch, seq_len, d = x.shape
    n_ctx = ctx.shape[1]
    qk_scale = HEAD_DIM ** -0.5 * LOG2_E
    tables = _rope_tables(seq_len)
    ctx_tables = tuple(t[:n_ctx] for t in tables)
    x2d = x.reshape(batch * seq_len, d)
    xc2d = ctx.reshape(batch * n_ctx, d)
    tm = 1024

    (sh1, sc1, g1, sh2, sc2, g2), (csh1, csc1, cg1, csh2, csc2, cg2) = _mod_vectors(
        c, c_ctx, e_w_mod[0], e_b_mod[0])
    a_q, a_kv = A_HEADS * HEAD_DIM, A_KV_HEADS * HEAD_DIM
    b_qk, b_v = B_HEADS * 2 * HEAD_DIM, B_HEADS * 2 * HEAD_DIM
    ones = lambda n: jnp.ones((n,), F32)
    tile = lambda v, n: jnp.tile(v, n // HEAD_DIM)
    col_gain = jnp.concatenate([tile(e_a_qnorm[0] * qk_scale, a_q), tile(e_a_knorm[0], a_kv), ones(a_kv),
                                tile(e_b_qnorm[0] * qk_scale, b_qk), tile(e_b_knorm[0], b_qk), ones(b_v)])
    kinds = lambda *pairs: jnp.concatenate([jnp.full((n // HEAD_DIM,), k, jnp.int32) for k, n in pairs])
    col_kind = kinds((1, a_q), (1, a_kv), (0, a_kv), (1, b_qk), (1, b_qk), (0, b_v))
    w_in0 = e_w_in[0].astype(BF16)
    z = _in_proj(x2d, e_norm1[0], sh1, sc1, w_in0, col_gain, col_kind, tables,
                 tm=tm, rows_per_mod=seq_len, rope=True)
    zc = _in_proj(xc2d, e_norm1[0], csh1, csc1, w_in0, col_gain, col_kind, ctx_tables,
                  tm=n_ctx, rows_per_mod=batch * n_ctx, rope=False)

    ca_q, ca_k, ca_v = 0, a_q // LANES, (a_q + a_kv) // LANES
    cb_q = (a_q + 2 * a_kv) // LANES
    cb_k, cb_v = cb_q + b_qk // LANES, cb_q + 2 * b_qk // LANES
    lam_init = 0.8 - 0.6 * math.exp(-0.3 * 0)
    dims = dict(batch=batch, seq_len=seq_len, n_ctx=n_ctx)
    attn_a = functools.partial(_attn_a, z, zc, e_a_sink[0], col_q=ca_q, col_k=ca_k, col_v=ca_v, **dims)
    attn_b = functools.partial(_attn_b, z, zc, e_b_lam_q1[0], e_b_lam_k1[0], e_b_lam_q2[0], e_b_lam_k2[0],
                               e_b_subln[0], col_q=cb_q, col_k=cb_k, col_v=cb_v, lam_init=lam_init, **dims)
    w_out0 = e_w_out[0].astype(BF16)
    x2d = _out_proj([attn_a(q_from_ctx=False), attn_b(q_from_ctx=False)], w_out0, x2d, g1,
                    tm=tm, rows_per_mod=seq_len)
    xc2d = _out_proj([attn_a(q_from_ctx=True), attn_b(q_from_ctx=True)], w_out0, xc2d, cg1,
                     tm=n_ctx, rows_per_mod=batch * n_ctx)
    ffn_in, ffn_down = e_ffn_w_in[0].astype(BF16), e_ffn_w_down[0].astype(BF16)
    x2d = _dense_ffn(x2d, e_norm2[0], sh2, sc2, g2, ffn_in, ffn_down, tm=512, rows_per_mod=seq_len)
    xc2d = _dense_ffn(xc2d, e_norm2[0], csh2, csc2, cg2, ffn_in, ffn_down, tm=n_ctx,
                      rows_per_mod=batch * n_ctx)

    (sh1, sc1, g1, sh2, sc2, g2), (csh1, csc1, _, _, _, _) = _mod_vectors(c, c_ctx, o_w_mod[0], o_b_mod[0])
    c_q = d
    c_kv = C_KV_HEADS * HEAD_DIM
    col_gain = jnp.concatenate([tile(o_c_qnorm[0] * qk_scale, c_q), tile(o_c_knorm[0], c_kv), ones(c_kv)])
    col_kind = kinds((1, c_q), (1, c_kv), (0, c_kv))
    w_in1 = o_w_in[0].astype(BF16)
    z = _in_proj(x2d, o_norm1[0], sh1, sc1, w_in1, col_gain, col_kind, tables,
                 tm=tm, rows_per_mod=seq_len, rope=True)
    zc = _in_proj(xc2d, o_norm1[0], csh1, csc1, w_in1, col_gain, col_kind, ctx_tables,
                  tm=n_ctx, rows_per_mod=batch * n_ctx, rope=False)
    o = _attn_c(z, zc, n_heads=c_q // HEAD_DIM, col_k=c_q // LANES, col_v=(c_q + c_kv) // LANES, **dims)
    x2d = _out_proj([o], o_w_out[0].astype(BF16), x2d, g1, tm=tm, rows_per_mod=seq_len)
    x2d = _moe(x2d, o_norm2[0], sh2, sc2, g2, o_router[0], o_exp_w_in.reshape(o_exp_w_in.shape[1:]),
               o_exp_w_down.reshape(o_exp_w_down.shape[1:]), rows_per_mod=seq_len)
    return x2d.reshape(batch, seq_len, d)
```

```python
import functools
import math

import jax
import jax.numpy as jnp
import numpy as np
from jax import lax
from jax.experimental import pallas as pl
from jax.experimental.pallas import tpu as pltpu

F32 = jnp.float32
BF16 = jnp.bfloat16

HEAD_DIM = 128
GRID_W = 64
ROT_AXIS = HEAD_DIM // 2
ROPE_THETA = 10000.0
EPS = 1e-6
NEG_INF = -1e30
LOG2_E = math.log2(math.e)
N_MOD = 6
WINDOW = 128
A_HEADS, A_KV_HEADS = 8, 2
B_HEADS = 4
C_KV_HEADS = 4
N_EXPERTS = 8
LANES = 128
V7X_VMEM_BYTES = 64 * 1024 * 1024
VMEM_TEMP_ALLOWANCE = 16 * 1024 * 1024


def _vmem_limit(*block_bytes):
    need = 2 * sum(block_bytes) + VMEM_TEMP_ALLOWANCE
    return int(min(need, V7X_VMEM_BYTES - 6 * 1024 * 1024))


def _params(sem, *block_bytes):
    return pltpu.CompilerParams(dimension_semantics=sem, vmem_limit_bytes=_vmem_limit(*block_bytes))


def _nt_dot(a, b):
    return lax.dot_general(a, b, (((1,), (1,)), ((), ())), preferred_element_type=F32)


def _rms_mod(x, g, shift, scale):
    ms = jnp.mean(x * x, axis=-1, keepdims=True)
    y = x * lax.rsqrt(ms + EPS) * g
    return y * (1.0 + scale) + shift


def _mod_kernel(c_ref, w_ref, b_ref, o_ref):
    c = c_ref[...]
    s = c * jax.nn.sigmoid(c)
    s_hi = s.astype(BF16).astype(F32)
    top = lax.broadcasted_iota(jnp.int32, s.shape, 0) < 8
    lhs = jnp.where(top, s_hi, s - s_hi).astype(BF16)
    acc = jnp.dot(lhs, w_ref[...].astype(BF16), preferred_element_type=F32)
    o_ref[...] = acc[0:8] + acc[8:16] + b_ref[...]


def _modulation(cvec, w_mod, b_mod):
    d, n = w_mod.shape
    tn = 1024
    return pl.pallas_call(
        _mod_kernel,
        out_shape=jax.ShapeDtypeStruct((8, n), F32),
        grid=(n // tn,),
        in_specs=[pl.BlockSpec((16, d), lambda j: (0, 0)),
                  pl.BlockSpec((d, tn), lambda j: (0, j)),
                  pl.BlockSpec((1, tn), lambda j: (0, j))],
        out_specs=pl.BlockSpec((8, tn), lambda j: (0, j)),
        compiler_params=_params(("arbitrary",), d * tn * 4, d * tn * 2),
        name="modulation",
    )(cvec, w_mod, b_mod.reshape(1, n))


def _inproj_kernel(kind_ref, x_ref, g_ref, sh_ref, sc_ref, w_ref, gc_ref, cos_ref, sin_ref,
                   o_ref, h_ref, *, rope):
    j = pl.program_id(1)
    tn = o_ref.shape[1]
    nch = tn // LANES

    @pl.when(j == 0)
    def _():
        h_ref[...] = _rms_mod(x_ref[...], g_ref[...], sh_ref[0], sc_ref[0]).astype(BF16)

    acc = jnp.dot(h_ref[...], w_ref[...], preferred_element_type=F32)
    for c in range(nch):
        a = acc[:, c * LANES:(c + 1) * LANES]
        kind = kind_ref[j * nch + c]

        @pl.when(kind == 0)
        def _():
            o_ref[:, c * LANES:(c + 1) * LANES] = a.astype(o_ref.dtype)

        @pl.when(kind == 1)
        def _():
            ms = jnp.mean(a * a, axis=-1, keepdims=True)
            y = a * lax.rsqrt(ms + EPS) * gc_ref[:, c * LANES:(c + 1) * LANES]
            if rope:
                y = y * cos_ref[...] + pltpu.roll(y, LANES // 2, 1) * sin_ref[...]
            o_ref[:, c * LANES:(c + 1) * LANES] = y.astype(o_ref.dtype)


def _in_proj(x2d, norm_g, shift, scale, w_bf16, col_gain, col_kind, tables, *, tm, rows_per_mod, rope):
    rows, d = x2d.shape
    n = w_bf16.shape[1]
    tn = 512
    cos_t, sin_t = tables
    pos_tiles = cos_t.shape[0] // tm

    grid_spec = pltpu.PrefetchScalarGridSpec(
        num_scalar_prefetch=1,
        grid=(rows // tm, n // tn),
        in_specs=[
            pl.BlockSpec((tm, d), lambda i, j, k: (i, 0)),
            pl.BlockSpec((1, d), lambda i, j, k: (0, 0)),
            pl.BlockSpec((1, 1, d), lambda i, j, k: ((i * tm) // rows_per_mod, 0, 0)),
            pl.BlockSpec((1, 1, d), lambda i, j, k: ((i * tm) // rows_per_mod, 0, 0)),
            pl.BlockSpec((d, tn), lambda i, j, k: (0, j)),
            pl.BlockSpec((1, tn), lambda i, j, k: (0, j)),
            pl.BlockSpec((tm, LANES), lambda i, j, k: (i % pos_tiles, 0)),
            pl.BlockSpec((tm, LANES), lambda i, j, k: (i % pos_tiles, 0)),
        ],
        out_specs=pl.BlockSpec((tm, tn), lambda i, j, k: (i, j)),
        scratch_shapes=[pltpu.VMEM((tm, d), BF16)],
    )
    return pl.pallas_call(
        functools.partial(_inproj_kernel, rope=rope),
        out_shape=jax.ShapeDtypeStruct((rows, n), BF16),
        grid_spec=grid_spec,
        compiler_params=_params(("arbitrary", "arbitrary"), tm * d * 4, d * tn * 2, tm * tn * 2,
                                2 * tm * LANES * 4, tm * d),
        name="in_proj_rope" if rope else "in_proj_ctx",
    )(col_kind, x2d, norm_g.reshape(1, d), shift, scale, w_bf16, col_gain.reshape(1, n), cos_t, sin_t)


def _window_mask(q0, ws, tq, nk):
    qpos = q0 + lax.broadcasted_iota(jnp.int32, (tq, nk), 0)
    kpos = ws + lax.broadcasted_iota(jnp.int32, (tq, nk), 1)
    return jnp.abs(qpos - kpos) <= WINDOW


def _attn_a_kernel(sink_ref, q_ref, kl_ref, vl_ref, kc_ref, vc_ref, o_ref, vl1_ref, vc1_ref, *slots,
                   tq, seq_len, group):
    depth = len(slots) // 2
    p_slots, sink_slots = slots[:depth], slots[depth:]
    sink = sink_ref[pl.program_id(1)] * LOG2_E
    dh = vc_ref.shape[1]
    nk = tq + 2 * WINDOW if seq_len else 0

    @pl.when(pl.program_id(1) % group == 0)
    def _():
        if seq_len:
            vl1_ref[:, :dh] = vl_ref[...]
            vl1_ref[:, dh:] = jnp.ones_like(vl_ref)
        vc1_ref[:, :dh] = vc_ref[...]
        vc1_ref[:, dh:] = jnp.ones_like(vc_ref)

    def window_start(r0):
        return pl.multiple_of(jnp.clip(r0 - WINDOW, 0, seq_len - nk), WINDOW)

    def score_stage(b, slot):
        r0 = pl.multiple_of(b * tq, tq)
        q = q_ref[pl.ds(r0, tq), :]
        s_ctx = _nt_dot(q, kc_ref[...])
        m = jnp.maximum(jnp.max(s_ctx, axis=-1, keepdims=True), sink)
        if seq_len:
            ws = window_start(r0)
            s_loc = _nt_dot(q, kl_ref[pl.ds(ws, nk), :])
            s_loc = jnp.where(_window_mask(r0, ws, tq, nk), s_loc, NEG_INF)
            m = jnp.maximum(m, jnp.max(s_loc, axis=-1, keepdims=True))
            p_slots[slot][:, :nk] = jnp.exp2(s_loc - m).astype(BF16)
        p_slots[slot][:, nk:] = jnp.exp2(s_ctx - m).astype(BF16)
        sink_slots[slot][...] = jnp.broadcast_to(jnp.exp2(sink - m), sink_slots[slot].shape)

    def value_stage(b, slot):
        r0 = pl.multiple_of(b * tq, tq)
        acc = jnp.dot(p_slots[slot][:, nk:], vc1_ref[...], preferred_element_type=F32)
        if seq_len:
            acc = acc + jnp.dot(p_slots[slot][:, :nk], vl1_ref[pl.ds(window_start(r0), nk), :],
                                preferred_element_type=F32)
        den = acc[:, dh:dh + 1] + sink_slots[slot][:, 0:1]
        o_ref[pl.ds(r0, tq), :] = (acc[:, :dh] / den).astype(o_ref.dtype)

    _pipeline_blocks(q_ref.shape[0] // tq, depth, score_stage, value_stage)


def _attn_a(z, zc, sink, *, batch, seq_len, n_ctx, q_from_ctx, col_q, col_k, col_v):
    g = A_HEADS // A_KV_HEADS
    if q_from_ctx:
        lq, tq, qsrc, depth = n_ctx, n_ctx, zc, 1
    else:
        lq, tq, qsrc, depth = seq_len, 256, z, 8
    lat_len = 0 if q_from_ctx else seq_len
    lat_block = 16 if q_from_ctx else seq_len
    lat_tiles = seq_len // lat_block
    n_keys = n_ctx + (0 if q_from_ctx else tq + 2 * WINDOW)
    grid_spec = pltpu.PrefetchScalarGridSpec(
        num_scalar_prefetch=1,
        grid=(batch, A_HEADS),
        in_specs=[
            pl.BlockSpec((lq, LANES), lambda b, h, s: (b, col_q + h)),
            pl.BlockSpec((lat_block, LANES), lambda b, h, s: (b * lat_tiles, col_k + h // g)),
            pl.BlockSpec((lat_block, LANES), lambda b, h, s: (b * lat_tiles, col_v + h // g)),
            pl.BlockSpec((n_ctx, LANES), lambda b, h, s: (b, col_k + h // g)),
            pl.BlockSpec((n_ctx, LANES), lambda b, h, s: (b, col_v + h // g)),
        ],
        out_specs=pl.BlockSpec((lq, LANES), lambda b, h, s: (b, h)),
        scratch_shapes=[pltpu.VMEM((lat_block, 2 * LANES), BF16), pltpu.VMEM((n_ctx, 2 * LANES), BF16)]
        + [pltpu.VMEM((tq, n_keys), BF16)] * depth + [pltpu.VMEM((tq, LANES), F32)] * depth,
    )
    return pl.pallas_call(
        functools.partial(_attn_a_kernel, tq=tq, seq_len=lat_len, group=g),
        out_shape=jax.ShapeDtypeStruct((batch * lq, A_HEADS * HEAD_DIM), BF16),
        grid_spec=grid_spec,
        compiler_params=_params(("arbitrary",) * 2, 2 * lq * LANES * 2, 4 * lat_block * LANES * 2,
                                4 * n_ctx * LANES * 2, depth * tq * (n_keys + 2 * LANES)),
        name="attn_window_ctx" if q_from_ctx else "attn_window",
    )(sink, qsrc, z, z, zc, zc)


def _pipeline_blocks(n_blocks, depth, score_stage, value_stage):
    score_stage(0, 0)

    def body(u, carry):
        for i in range(depth):
            b = u * depth + i
            score_stage(jnp.minimum(b + 1, n_blocks - 1), (i + 1) % depth)
            value_stage(b, i)
        return carry

    lax.fori_loop(0, n_blocks // depth, body, 0)


def _join_rows(dst_ref, lat_ref, ctx_ref, has_lat):
    n_lat = lat_ref.shape[0] if has_lat else 0
    if has_lat:
        dst_ref[0:n_lat, 0:lat_ref.shape[1]] = lat_ref[...]
    dst_ref[n_lat:n_lat + ctx_ref.shape[0], 0:ctx_ref.shape[1]] = ctx_ref[...]


def _exp2_numerators(q, k_all):
    s = _nt_dot(q, k_all)
    return jnp.exp2(s - jnp.max(s, axis=-1, keepdims=True))


def _attn_b_kernel(q1_ref, q2_ref, k1l_ref, k2l_ref, vl_ref, k1c_ref, k2c_ref, vc_ref,
                   lq1_ref, lk1_ref, lq2_ref, lk2_ref, sub_ref, o_ref, k1_ref, k2_ref, v_ref, *p_slots,
                   tq, has_lat, lam_init):
    lam = (jnp.exp(jnp.sum(lq1_ref[0] * lk1_ref[0], axis=-1, keepdims=True))
           - jnp.exp(jnp.sum(lq2_ref[0] * lk2_ref[0], axis=-1, keepdims=True)) + lam_init)
    _join_rows(k1_ref, k1l_ref, k1c_ref, has_lat)
    _join_rows(k2_ref, k2l_ref, k2c_ref, has_lat)
    _join_rows(v_ref, vl_ref, vc_ref, has_lat)

    def score_stage(b, slot):
        r0 = pl.multiple_of(b * tq, tq)
        p1 = _exp2_numerators(q1_ref[pl.ds(r0, tq), :], k1_ref[...])
        p2 = _exp2_numerators(q2_ref[pl.ds(r0, tq), :], k2_ref[...])
        w1 = 1.0 / jnp.sum(p1, axis=-1, keepdims=True)
        w2 = lam / jnp.sum(p2, axis=-1, keepdims=True)
        p_slots[slot][...] = (p1 * w1 - p2 * w2).astype(BF16)

    def value_stage(b, slot):
        r0 = pl.multiple_of(b * tq, tq)
        o = jnp.dot(p_slots[slot][...], v_ref[...], preferred_element_type=F32)
        ms = jnp.mean(o * o, axis=-1, keepdims=True)
        o = o * lax.rsqrt(ms + EPS) * sub_ref[...] * (1.0 - lam_init)
        o_ref[pl.ds(r0, tq), :] = o.astype(o_ref.dtype)

    _pipeline_blocks(q1_ref.shape[0] // tq, len(p_slots), score_stage, value_stage)


def _attn_b(z, zc, lq1, lk1, lq2, lk2, subln, *, batch, seq_len, n_ctx, q_from_ctx, col_q, col_k, col_v,
            lam_init):
    dv = 2 * HEAD_DIM
    if q_from_ctx:
        lq, bq, tq, qsrc, depth = n_ctx, n_ctx, n_ctx, zc, 1
    else:
        lq, bq, tq, qsrc, depth = seq_len, seq_len, 256, z, 4
    nqb = lq // bq
    lat_block = 16 if q_from_ctx else seq_len
    lat_tiles = seq_len // lat_block
    n_keys = n_ctx if q_from_ctx else seq_len + n_ctx
    lam_spec = pl.BlockSpec((1, 1, HEAD_DIM), lambda b, h, i: (h, 0, 0))
    in_specs = [
        pl.BlockSpec((bq, LANES), lambda b, h, i: (b * nqb + i, col_q + 2 * h)),
        pl.BlockSpec((bq, LANES), lambda b, h, i: (b * nqb + i, col_q + 2 * h + 1)),
        pl.BlockSpec((lat_block, LANES), lambda b, h, i: (b * lat_tiles, col_k + 2 * h)),
        pl.BlockSpec((lat_block, LANES), lambda b, h, i: (b * lat_tiles, col_k + 2 * h + 1)),
        pl.BlockSpec((lat_block, dv), lambda b, h, i: (b * lat_tiles, col_v // 2 + h)),
        pl.BlockSpec((n_ctx, LANES), lambda b, h, i: (b, col_k + 2 * h)),
        pl.BlockSpec((n_ctx, LANES), lambda b, h, i: (b, col_k + 2 * h + 1)),
        pl.BlockSpec((n_ctx, dv), lambda b, h, i: (b, col_v // 2 + h)),
        lam_spec, lam_spec, lam_spec, lam_spec,
        pl.BlockSpec((1, dv), lambda b, h, i: (0, 0)),
    ]
    return pl.pallas_call(
        functools.partial(_attn_b_kernel, tq=tq, has_lat=not q_from_ctx, lam_init=lam_init),
        out_shape=jax.ShapeDtypeStruct((batch * lq, B_HEADS * dv), BF16),
        grid=(batch, B_HEADS, nqb),
        in_specs=in_specs,
        out_specs=pl.BlockSpec((bq, dv), lambda b, h, i: (b * nqb + i, h)),
        scratch_shapes=[pltpu.VMEM((n_keys, LANES), BF16), pltpu.VMEM((n_keys, LANES), BF16),
                        pltpu.VMEM((n_keys, dv), BF16)] + [pltpu.VMEM((tq, n_keys), BF16)] * depth,
        compiler_params=_params(("arbitrary",) * 3, 2 * bq * LANES * 2, 4 * lat_block * LANES * 2,
                                4 * n_ctx * LANES * 2, bq * dv * 2, (4 * LANES + depth * tq) * n_keys,
                                4 * tq * n_keys * 4 // 2),
        name="attn_diff_ctx" if q_from_ctx else "attn_diff",
    )(qsrc, qsrc, z, z, z, zc, zc, zc,
      lq1.reshape(B_HEADS, 1, HEAD_DIM), lk1.reshape(B_HEADS, 1, HEAD_DIM),
      lq2.reshape(B_HEADS, 1, HEAD_DIM), lk2.reshape(B_HEADS, 1, HEAD_DIM), subln.reshape(1, dv))


def _attn_c_kernel(q_ref, kl_ref, vl_ref, kc_ref, vc_ref, o_ref, k_ref, v1_ref, *p_slots, tq, group):
    dh = vl_ref.shape[1]

    @pl.when(pl.program_id(1) % group == 0)
    def _():
        _join_rows(k_ref, kl_ref, kc_ref, True)
        _join_rows(v1_ref, vl_ref, vc_ref, True)
        v1_ref[:, dh:] = jnp.ones((v1_ref.shape[0], v1_ref.shape[1] - dh), v1_ref.dtype)

    def score_stage(b, slot):
        r0 = pl.multiple_of(b * tq, tq)
        p_slots[slot][...] = _exp2_numerators(q_ref[pl.ds(r0, tq), :], k_ref[...]).astype(BF16)

    def value_stage(b, slot):
        r0 = pl.multiple_of(b * tq, tq)
        acc = jnp.dot(p_slots[slot][...], v1_ref[...], preferred_element_type=F32)
        o_ref[pl.ds(r0, tq), :] = (acc[:, :dh] / acc[:, dh:dh + 1]).astype(o_ref.dtype)

    _pipeline_blocks(q_ref.shape[0] // tq, len(p_slots), score_stage, value_stage)


def _attn_c(z, zc, *, batch, seq_len, n_ctx, n_heads, col_k, col_v):
    g = n_heads // C_KV_HEADS
    tq, depth = 256, 8
    n_keys = seq_len + n_ctx
    return pl.pallas_call(
        functools.partial(_attn_c_kernel, tq=tq, group=g),
        out_shape=jax.ShapeDtypeStruct((batch * seq_len, n_heads * HEAD_DIM), BF16),
        grid=(batch, n_heads),
        scratch_shapes=[pltpu.VMEM((n_keys, LANES), BF16), pltpu.VMEM((n_keys, 2 * LANES), BF16)]
        + [pltpu.VMEM((tq, n_keys), BF16)] * depth,
        in_specs=[
            pl.BlockSpec((seq_len, LANES), lambda b, h: (b, h)),
            pl.BlockSpec((seq_len, LANES), lambda b, h: (b, col_k + h // g)),
            pl.BlockSpec((seq_len, LANES), lambda b, h: (b, col_v + h // g)),
            pl.BlockSpec((n_ctx, LANES), lambda b, h: (b, col_k + h // g)),
            pl.BlockSpec((n_ctx, LANES), lambda b, h: (b, col_v + h // g)),
        ],
        out_specs=pl.BlockSpec((seq_len, LANES), lambda b, h: (b, h)),
        compiler_params=_params(("arbitrary",) * 2, 4 * seq_len * LANES * 2, 2 * n_ctx * LANES * 2,
                                (3 * LANES + depth * tq) * n_keys, 3 * tq * n_keys * 4 // 2),
        name="attn_gqa",
    )(z, z, z, zc, zc)


def _outproj_kernel(*refs, n_lhs):
    lhs = refs[:n_lhs]
    w_ref, res_ref, gate_ref, o_ref = refs[n_lhs:]
    acc = None
    k0 = 0
    for a_ref in lhs:
        kw = a_ref.shape[1]
        part = jnp.dot(a_ref[...], w_ref[k0:k0 + kw, :], preferred_element_type=F32)
        acc = part if acc is None else acc + part
        k0 += kw
    o_ref[...] = res_ref[...] + gate_ref[0] * acc


def _out_proj(lhs_list, w_bf16, res2d, gate, *, tm, rows_per_mod):
    rows, n = res2d.shape
    kdim = w_bf16.shape[0]
    tn = 512
    in_specs = [pl.BlockSpec((tm, a.shape[1]), lambda i, j: (i, 0)) for a in lhs_list]
    in_specs += [
        pl.BlockSpec((kdim, tn), lambda i, j: (0, j)),
        pl.BlockSpec((tm, tn), lambda i, j: (i, j)),
        pl.BlockSpec((1, 1, tn), lambda i, j: ((i * tm) // rows_per_mod, 0, j)),
    ]
    return pl.pallas_call(
        functools.partial(_outproj_kernel, n_lhs=len(lhs_list)),
        out_shape=jax.ShapeDtypeStruct((rows, n), F32),
        grid=(rows // tm, n // tn),
        in_specs=in_specs,
        out_specs=pl.BlockSpec((tm, tn), lambda i, j: (i, j)),
        compiler_params=_params(("arbitrary", "arbitrary"), tm * kdim * 2, kdim * tn * 2, 2 * tm * tn * 4),
        name="out_proj",
    )(*lhs_list, w_bf16, res2d, gate)


def _ffn_kernel(x_ref, g_ref, sh_ref, sc_ref, gate_ref, wg_ref, wu_ref, wd_ref, o_ref, h_ref, acc_ref):
    f = pl.program_id(1)

    @pl.when(f == 0)
    def _():
        h_ref[...] = _rms_mod(x_ref[...], g_ref[...], sh_ref[0], sc_ref[0]).astype(BF16)
        acc_ref[...] = jnp.zeros_like(acc_ref)

    h = h_ref[...]
    g_ = jnp.dot(h, wg_ref[...], preferred_element_type=F32)
    u_ = jnp.dot(h, wu_ref[...], preferred_element_type=F32)
    a = (g_ * jax.nn.sigmoid(g_) * u_).astype(BF16)
    acc_ref[...] += jnp.dot(a, wd_ref[...], preferred_element_type=F32)

    @pl.when(f == pl.num_programs(1) - 1)
    def _():
        o_ref[...] = x_ref[...] + gate_ref[0] * acc_ref[...]


def _dense_ffn(x2d, norm_g, shift, scale, gate, w_in_bf16, w_down_bf16, *, tm, rows_per_mod):
    rows, d = x2d.shape
    d_ff = w_down_bf16.shape[0]
    tf = 512
    nf = d_ff // tf
    mod_spec = pl.BlockSpec((1, 1, d), lambda i, f: ((i * tm) // rows_per_mod, 0, 0))
    return pl.pallas_call(
        _ffn_kernel,
        out_shape=jax.ShapeDtypeStruct((rows, d), F32),
        grid=(rows // tm, nf),
        in_specs=[
            pl.BlockSpec((tm, d), lambda i, f: (i, 0)),
            pl.BlockSpec((1, d), lambda i, f: (0, 0)),
            mod_spec, mod_spec, mod_spec,
            pl.BlockSpec((d, tf), lambda i, f: (0, f)),
            pl.BlockSpec((d, tf), lambda i, f: (0, nf + f)),
            pl.BlockSpec((tf, d), lambda i, f: (f, 0)),
        ],
        out_specs=pl.BlockSpec((tm, d), lambda i, f: (i, 0)),
        scratch_shapes=[pltpu.VMEM((tm, d), BF16), pltpu.VMEM((tm, d), F32)],
        compiler_params=_params(("arbitrary", "arbitrary"), 2 * tm * d * 4, 3 * d * tf * 2, tm * d * 3),
        name="dense_swiglu",
    )(x2d, norm_g.reshape(1, d), shift, scale, gate, w_in_bf16, w_in_bf16, w_down_bf16)


def _split3(v):
    hi = v.astype(BF16)
    r = v - hi.astype(F32)
    mid = r.astype(BF16)
    lo = (r - mid.astype(F32)).astype(BF16)
    return hi, mid, lo


def _router_kernel(x_ref, g_ref, sh_ref, sc_ref, wr_ref, h_ref, ids_ref, gates_ref, cnt_ref, carry_ref):
    i = pl.program_id(0)
    tm = x_ref.shape[0]

    @pl.when(i == 0)
    def _():
        carry_ref[...] = jnp.zeros_like(carry_ref)

    h = _rms_mod(x_ref[...], g_ref[...], sh_ref[0], sc_ref[0])
    h_ref[...] = h
    h0, h1, h2 = _split3(h)
    w0, w1, w2 = _split3(wr_ref[...])
    dot = functools.partial(jnp.dot, preferred_element_type=F32)
    logits = (dot(h0, w0) + (dot(h0, w1) + dot(h1, w0))
              + (dot(h0, w2) + dot(h1, w1) + dot(h2, w0)))
    lane = lax.broadcasted_iota(jnp.int32, (tm, LANES), 1)
    logits = jnp.where(lane < N_EXPERTS, logits, -jnp.inf)
    v0 = jnp.max(logits, axis=-1, keepdims=True)
    i0 = jnp.min(jnp.where(logits == v0, lane, LANES), axis=-1, keepdims=True)
    rest = jnp.where(lane == i0, -jnp.inf, logits)
    v1 = jnp.max(rest, axis=-1, keepdims=True)
    i1 = jnp.min(jnp.where(rest == v1, lane, LANES), axis=-1, keepdims=True)
    e1 = jnp.exp(v1 - v0)
    g0 = 1.0 / (1.0 + e1)
    g1 = e1 / (1.0 + e1)

    sel = (lane == i0) | (lane == i1)
    row = lax.broadcasted_iota(jnp.int32, (tm, tm), 0)
    col = lax.broadcasted_iota(jnp.int32, (tm, tm), 1)
    tri = (col < row).astype(BF16)
    before = jnp.dot(tri, sel.astype(BF16), preferred_element_type=F32) + carry_ref[...]
    r0 = jnp.sum(jnp.where(lane == i0, before, 0.0), axis=-1, keepdims=True).astype(jnp.int32)
    r1 = jnp.sum(jnp.where(lane == i1, before, 0.0), axis=-1, keepdims=True).astype(jnp.int32)
    total = carry_ref[...] + jnp.sum(sel.astype(F32), axis=0, keepdims=True)
    carry_ref[...] = total

    ids_ref[...] = jnp.where(lane == 0, i0, jnp.where(lane == 1, i1, jnp.where(lane == 2, r0, r1)))
    gates_ref[...] = jnp.where(lane == 0, g0, g1)
    cnt_ref[...] = jnp.broadcast_to(total, cnt_ref.shape).astype(jnp.int32)


def _router(x2d, norm_g, shift, scale, w_router, *, tm, rows_per_mod):
    rows, d = x2d.shape
    wr = jnp.zeros((d, LANES), F32).at[:, :N_EXPERTS].set(w_router)
    mod_spec = pl.BlockSpec((1, 1, d), lambda i: ((i * tm) // rows_per_mod, 0, 0))
    return pl.pallas_call(
        _router_kernel,
        out_shape=(jax.ShapeDtypeStruct((rows, d), F32),
                   jax.ShapeDtypeStruct((rows, LANES), jnp.int32),
                   jax.ShapeDtypeStruct((rows, LANES), F32),
                   jax.ShapeDtypeStruct((8, LANES), jnp.int32)),
        grid=(rows // tm,),
        in_specs=[pl.BlockSpec((tm, d), lambda i: (i, 0)),
                  pl.BlockSpec((1, d), lambda i: (0, 0)),
                  mod_spec, mod_spec,
                  pl.BlockSpec((d, LANES), lambda i: (0, 0))],
        out_specs=(pl.BlockSpec((tm, d), lambda i: (i, 0)),
                   pl.BlockSpec((tm, LANES), lambda i: (i, 0)),
                   pl.BlockSpec((tm, LANES), lambda i: (i, 0)),
                   pl.BlockSpec((8, LANES), lambda i: (0, 0))),
        scratch_shapes=[pltpu.VMEM((1, LANES), F32)],
        compiler_params=_params(("arbitrary",), 2 * tm * d * 4, d * LANES * 4, 2 * tm * LANES * 4),
        name="router",
    )(x2d, norm_g.reshape(1, d), shift, scale, wr)


GATHER_UNROLL = 8


def _start_row_gather(src_hbm, idx_ref, base, dst_ref, sem):
    def body(r8, c):
        for u in range(GATHER_UNROLL):
            r = r8 * GATHER_UNROLL + u
            pltpu.make_async_copy(src_hbm.at[pl.ds(idx_ref[base + r], 1), :],
                                  dst_ref.at[pl.ds(r, 1), :], sem).start()
        return c

    lax.fori_loop(0, dst_ref.shape[0] // GATHER_UNROLL, body, 0)


def _wait_row_gather(src_hbm, dst_ref, sem):
    pltpu.make_async_copy(src_hbm.at[pl.ds(0, dst_ref.shape[0]), :], dst_ref, sem).wait()


def _dispatch_kernel(idx_ref, src_hbm, o_ref, buf_ref, sem):
    tr = o_ref.shape[0]
    i, n = pl.program_id(0), pl.num_programs(0)
    slot = i & 1

    @pl.when(i == 0)
    def _():
        _start_row_gather(src_hbm, idx_ref, 0, buf_ref.at[0], sem.at[0])

    @pl.when(i + 1 < n)
    def _():
        _start_row_gather(src_hbm, idx_ref, (i + 1) * tr, buf_ref.at[1 - slot], sem.at[1 - slot])

    _wait_row_gather(src_hbm, buf_ref.at[slot], sem.at[slot])
    o_ref[...] = buf_ref[slot].astype(o_ref.dtype)


def _dispatch(h2d, row_tok, *, tr):
    n_rows = row_tok.shape[0]
    d = h2d.shape[1]
    grid_spec = pltpu.PrefetchScalarGridSpec(
        num_scalar_prefetch=1,
        grid=(n_rows // tr,),
        in_specs=[pl.BlockSpec(memory_space=pl.ANY)],
        out_specs=pl.BlockSpec((tr, d), lambda i, idx: (i, 0)),
        scratch_shapes=[pltpu.VMEM((2, tr, d), F32), pltpu.SemaphoreType.DMA((2,))],
    )
    return pl.pallas_call(
        _dispatch_kernel,
        out_shape=jax.ShapeDtypeStruct((n_rows, d), BF16),
        grid_spec=grid_spec,
        compiler_params=_params(("arbitrary",), tr * d * 2, tr * d * 4),
        name="expert_dispatch",
    )(row_tok, h2d)


def _combine_kernel(d0_ref, d1_ref, ys_hbm, x_ref, gate_ref, rg_ref, o_ref, a_ref, b_ref, sem):
    tt = x_ref.shape[0]
    i, n = pl.program_id(0), pl.num_programs(0)
    slot = i & 1

    def start(tile, s):
        _start_row_gather(ys_hbm, d0_ref, tile * tt, a_ref.at[s], sem.at[s, 0])
        _start_row_gather(ys_hbm, d1_ref, tile * tt, b_ref.at[s], sem.at[s, 1])

    @pl.when(i == 0)
    def _():
        start(0, 0)

    @pl.when(i + 1 < n)
    def _():
        start(i + 1, 1 - slot)

    _wait_row_gather(ys_hbm, a_ref.at[slot], sem.at[slot, 0])
    _wait_row_gather(ys_hbm, b_ref.at[slot], sem.at[slot, 1])
    rg = rg_ref[...]
    mix = rg[:, 0:1] * a_ref[slot] + rg[:, 1:2] * b_ref[slot]
    o_ref[...] = x_ref[...] + gate_ref[0] * mix


def _combine(ys, dest0, dest1, x2d, gate, row_gates, *, tt, rows_per_mod):
    rows, d = x2d.shape
    grid_spec = pltpu.PrefetchScalarGridSpec(
        num_scalar_prefetch=2,
        grid=(rows // tt,),
        in_specs=[pl.BlockSpec(memory_space=pl.ANY),
                  pl.BlockSpec((tt, d), lambda i, a, b: (i, 0)),
                  pl.BlockSpec((1, 1, d), lambda i, a, b: ((i * tt) // rows_per_mod, 0, 0)),
                  pl.BlockSpec((tt, LANES), lambda i, a, b: (i, 0))],
        out_specs=pl.BlockSpec((tt, d), lambda i, a, b: (i, 0)),
        scratch_shapes=[pltpu.VMEM((2, tt, d), F32), pltpu.VMEM((2, tt, d), F32),
                        pltpu.SemaphoreType.DMA((2, 2))],
    )
    return pl.pallas_call(
        _combine_kernel,
        out_shape=jax.ShapeDtypeStruct((rows, d), F32),
        grid_spec=grid_spec,
        compiler_params=_params(("arbitrary",), 2 * tt * d * 4, 2 * tt * d * 4),
        name="expert_combine",
    )(dest0, dest1, ys, x2d, gate, row_gates)


WEIGHT_DMA_PRIORITY = 1


def _stream_group_weights(sched, copies, cast):
    te_ref, first_ref, grp_ref, ngrp_ref, nxt_ref = sched
    j, r = pl.program_id(0), pl.program_id(1)
    nj = pl.num_programs(0)

    @pl.when(first_ref[r] == 1)
    def _():
        g = grp_ref[r]
        ng = ngrp_ref[0]
        slot = (j * ng + g) & 1

        @pl.when((j == 0) & (g == 0))
        def _():
            for cp in copies(te_ref[r], j, slot):
                cp.start(priority=WEIGHT_DMA_PRIORITY)

        for cp in copies(te_ref[r], j, slot):
            cp.wait()
        last = g == ng - 1

        @pl.when(jnp.logical_not(last & (j == nj - 1)))
        def _():
            for cp in copies(nxt_ref[r], jnp.where(last, j + 1, j), 1 - slot):
                cp.start(priority=WEIGHT_DMA_PRIORITY)

        cast(slot)


def _gmm_up_kernel(te_ref, first_ref, grp_ref, ngrp_ref, nxt_ref, nv_ref, x_ref, w_hbm, o_ref,
                   stage_ref, wbf_ref, sem):
    tn = o_ref.shape[1]
    nj = pl.num_programs(0)

    def copies(e, j, slot):
        return [pltpu.make_async_copy(w_hbm.at[e, :, pl.ds(pl.multiple_of((half * nj + j) * tn, tn), tn)],
                                      stage_ref.at[slot, half], sem.at[slot, half]) for half in range(2)]

    def cast(slot):
        wbf_ref[0] = stage_ref[slot, 0].astype(BF16)
        wbf_ref[1] = stage_ref[slot, 1].astype(BF16)

    _stream_group_weights((te_ref, first_ref, grp_ref, ngrp_ref, nxt_ref), copies, cast)
    valid = pl.program_id(1) < nv_ref[0]

    @pl.when(valid)
    def _():
        x = x_ref[...]
        g_ = jnp.dot(x, wbf_ref[0], preferred_element_type=F32)
        u_ = jnp.dot(x, wbf_ref[1], preferred_element_type=F32)
        o_ref[...] = (g_ * jax.nn.sigmoid(g_) * u_).astype(o_ref.dtype)

    @pl.when(jnp.logical_not(valid))
    def _():
        o_ref[...] = jnp.zeros_like(o_ref)


def _gmm_down_kernel(te_ref, first_ref, grp_ref, ngrp_ref, nxt_ref, nv_ref, h_ref, w_hbm, o_ref,
                     stage_ref, wbf_ref, sem):
    tn = o_ref.shape[1]

    def copies(e, j, slot):
        return [pltpu.make_async_copy(w_hbm.at[e, :, pl.ds(pl.multiple_of(j * tn, tn), tn)],
                                      stage_ref.at[slot], sem.at[slot])]

    def cast(slot):
        wbf_ref[...] = stage_ref[slot].astype(BF16)

    _stream_group_weights((te_ref, first_ref, grp_ref, ngrp_ref, nxt_ref), copies, cast)
    valid = pl.program_id(1) < nv_ref[0]

    @pl.when(valid)
    def _():
        o_ref[...] = jnp.dot(h_ref[...], wbf_ref[...], preferred_element_type=F32)

    @pl.when(jnp.logical_not(valid))
    def _():
        o_ref[...] = jnp.zeros_like(o_ref)


def _expert_ffn(xs, w_in, w_down, sched, n_valid, *, tr):
    n_rows, d = xs.shape
    d_ff = w_down.shape[1]
    n_tiles = n_rows // tr
    n_sched = len(sched) + 1
    tn1 = 1024
    nj1 = d_ff // tn1

    def row_map(*args):
        r, nv = args[1], args[-1]
        return (jnp.minimum(r, nv[0] - 1), 0)

    def out_map(*args):
        return (args[1], args[0])

    up_spec = pltpu.PrefetchScalarGridSpec(
        num_scalar_prefetch=n_sched,
        grid=(nj1, n_tiles),
        in_specs=[pl.BlockSpec((tr, d), row_map), pl.BlockSpec(memory_space=pl.ANY)],
        out_specs=pl.BlockSpec((tr, tn1), out_map),
        scratch_shapes=[pltpu.VMEM((2, 2, d, tn1), F32), pltpu.VMEM((2, d, tn1), BF16),
                        pltpu.SemaphoreType.DMA((2, 2))],
    )
    stage1 = 4 * d * tn1 * 4
    hidden = pl.pallas_call(
        _gmm_up_kernel,
        out_shape=jax.ShapeDtypeStruct((n_rows, d_ff), BF16),
        grid_spec=up_spec,
        compiler_params=pltpu.CompilerParams(
            dimension_semantics=("arbitrary", "arbitrary"),
            vmem_limit_bytes=stage1 + 2 * d * tn1 * 2 + 2 * (tr * d * 2 + tr * tn1 * 2) + VMEM_TEMP_ALLOWANCE // 2),
        name="expert_up",
    )(*sched, n_valid, xs, w_in)

    tn2 = 512
    down_spec = pltpu.PrefetchScalarGridSpec(
        num_scalar_prefetch=n_sched,
        grid=(d // tn2, n_tiles),
        in_specs=[pl.BlockSpec((tr, d_ff), row_map), pl.BlockSpec(memory_space=pl.ANY)],
        out_specs=pl.BlockSpec((tr, tn2), out_map),
        scratch_shapes=[pltpu.VMEM((2, d_ff, tn2), F32), pltpu.VMEM((d_ff, tn2), BF16),
                        pltpu.SemaphoreType.DMA((2,))],
    )
    stage2 = 2 * d_ff * tn2 * 4
    return pl.pallas_call(
        _gmm_down_kernel,
        out_shape=jax.ShapeDtypeStruct((n_rows, d), F32),
        grid_spec=down_spec,
        compiler_params=pltpu.CompilerParams(
            dimension_semantics=("arbitrary", "arbitrary"),
            vmem_limit_bytes=stage2 + d_ff * tn2 * 2 + 2 * (tr * d_ff * 2 + tr * tn2 * 4) + VMEM_TEMP_ALLOWANCE // 2),
        name="expert_down",
    )(*sched, n_valid, hidden, w_down)


def _moe(x2d, norm_g, shift, scale, gate, w_router, w_in, w_down, *, rows_per_mod):
    rows, d = x2d.shape
    tr = 256
    h2d, ids, gates, counts = _router(x2d, norm_g, shift, scale, w_router, tm=512, rows_per_mod=rows_per_mod)
    counts = counts[0, :N_EXPERTS]
    padded = (counts + tr - 1) // tr * tr
    pad_end = jnp.cumsum(padded)
    pad_start = pad_end - padded
    dest0 = pad_start[ids[:, 0]] + ids[:, 2]
    dest1 = pad_start[ids[:, 1]] + ids[:, 3]
    n_rows = 2 * rows + N_EXPERTS * tr
    n_tiles = n_rows // tr
    tok = jnp.arange(rows, dtype=jnp.int32)
    row_tok = jnp.zeros((n_rows,), jnp.int32).at[jnp.concatenate([dest0, dest1])].set(
        jnp.concatenate([tok, tok]))
    tile_expert = jnp.minimum(
        jnp.sum(jnp.arange(n_tiles, dtype=jnp.int32)[:, None] * tr >= pad_end[None, :], axis=1),
        N_EXPERTS - 1).astype(jnp.int32)
    n_valid = (pad_end[-1:] // tr).astype(jnp.int32)
    tile_id = jnp.arange(n_tiles, dtype=jnp.int32)
    prev_expert = jnp.concatenate([jnp.full((1,), -1, jnp.int32), tile_expert[:-1]])
    first = ((tile_id < n_valid[0]) & (tile_expert != prev_expert)).astype(jnp.int32)
    grp = jnp.cumsum(first).astype(jnp.int32) - 1
    n_grp = jnp.sum(first, keepdims=True).astype(jnp.int32)
    expert_id = jnp.arange(N_EXPERTS, dtype=jnp.int32)
    grp_expert = jnp.sort(jnp.where(counts > 0, expert_id, N_EXPERTS))
    nxt = grp_expert[(grp + 1) % n_grp[0]].astype(jnp.int32)
    sched = (tile_expert, first, grp, n_grp, nxt)

    xs = _dispatch(h2d, row_tok, tr=tr)
    ys = _expert_ffn(xs, w_in, w_down, sched, n_valid, tr=tr)
    return _combine(ys, dest0, dest1, x2d, gate, gates, tt=256, rows_per_mod=rows_per_mod)


def _pair_layout(w, n_cols):
    lead = w.shape[:-1]
    quarters = w.reshape(lead + (n_cols // HEAD_DIM, 2, 2, ROT_AXIS // 2))
    return jnp.swapaxes(quarters, -2, -3).reshape(lead + (n_cols,))


def _rope_tables(seq_len):
    t = np.arange(seq_len)
    row = (t // GRID_W).astype(np.float64)[:, None]
    col = (t % GRID_W).astype(np.float64)[:, None]
    inv = ROPE_THETA ** (-np.arange(0, ROT_AXIS, 2, dtype=np.float64) / ROT_AXIS)
    ar, ac = row * inv, col * inv
    cos = np.concatenate([np.cos(ar), np.cos(ac), np.cos(ar), np.cos(ac)], axis=-1)
    sin = np.concatenate([-np.sin(ar), -np.sin(ac), np.sin(ar), np.sin(ac)], axis=-1)
    return tuple(jnp.asarray(a.astype(np.float32)) for a in (cos, sin))


def _mod_vectors(c, c_ctx, w_mod, b_mod):
    batch, d = c.shape
    cvec = jnp.zeros((8, d), F32).at[:batch].set(c).at[batch].set(c_ctx)
    m = _modulation(jnp.concatenate([cvec, cvec], axis=0), w_mod, b_mod)
    lat = m[:batch].reshape(batch, N_MOD, 1, d)
    ctx = m[batch].reshape(N_MOD, 1, 1, d)
    return [lat[:, k] for k in range(N_MOD)], [ctx[k] for k in range(N_MOD)]


def kernel(x, c, ctx, c_ctx, e_norm1, e_norm2, e_w_mod, e_b_mod, e_w_in, e_w_out, e_a_qnorm, e_a_knorm,
           e_a_sink, e_b_qnorm, e_b_knorm, e_b_lam_q1, e_b_lam_k1, e_b_lam_q2, e_b_lam_k2, e_b_subln,
           e_ffn_w_in, e_ffn_w_down, o_norm1, o_norm2, o_w_mod, o_b_mod, o_w_in, o_w_out, o_c_qnorm,
           o_c_knorm, o_router, o_exp_w_in, o_exp_w_down):
    batch, seq_len, d = x.shape
    n_ctx = ctx.shape[1]
    qk_scale = HEAD_DIM ** -0.5 * LOG2_E
    tables = _rope_tables(seq_len)
    ctx_tables = tuple(t[:batch * n_ctx] for t in tables)
    x2d = x.reshape(batch * seq_len, d)
    xc2d = ctx.reshape(batch * n_ctx, d)
    tm = 1024

    (sh1, sc1, g1, sh2, sc2, g2), (csh1, csc1, cg1, csh2, csc2, cg2) = _mod_vectors(
        c, c_ctx, e_w_mod[0], e_b_mod[0])
    a_q, a_kv = A_HEADS * HEAD_DIM, A_KV_HEADS * HEAD_DIM
    b_qk, b_v = B_HEADS * 2 * HEAD_DIM, B_HEADS * 2 * HEAD_DIM
    ones = lambda n: jnp.ones((n,), F32)
    tile = lambda v, n: jnp.tile(v, n // HEAD_DIM)
    col_gain = jnp.concatenate([tile(e_a_qnorm[0] * qk_scale, a_q), tile(e_a_knorm[0], a_kv), ones(a_kv),
                                tile(e_b_qnorm[0] * qk_scale, b_qk), tile(e_b_knorm[0], b_qk), ones(b_v)])
    kinds = lambda *pairs: jnp.concatenate([jnp.full((n // HEAD_DIM,), k, jnp.int32) for k, n in pairs])
    sections = ((1, a_q), (1, a_kv), (0, a_kv), (1, b_qk), (1, b_qk), (0, b_v))
    col_kind = kinds(*sections)

    def pair_sections(w, sections):
        parts, c0 = [], 0
        for paired, n in sections:
            blk = w[..., c0:c0 + n]
            parts.append(_pair_layout(blk, n) if paired else blk)
            c0 += n
        return jnp.concatenate(parts, axis=-1)

    col_gain = pair_sections(col_gain, sections)
    w_in0 = pair_sections(e_w_in[0], sections).astype(BF16)
    z = _in_proj(x2d, e_norm1[0], sh1, sc1, w_in0, col_gain, col_kind, tables,
                 tm=tm, rows_per_mod=seq_len, rope=True)
    zc = _in_proj(xc2d, e_norm1[0], csh1, csc1, w_in0, col_gain, col_kind, ctx_tables,
                  tm=batch * n_ctx, rows_per_mod=batch * n_ctx, rope=False)

    ca_q, ca_k, ca_v = 0, a_q // LANES, (a_q + a_kv) // LANES
    cb_q = (a_q + 2 * a_kv) // LANES
    cb_k, cb_v = cb_q + b_qk // LANES, cb_q + 2 * b_qk // LANES
    lam_init = 0.8 - 0.6 * math.exp(-0.3 * 0)
    dims = dict(batch=batch, seq_len=seq_len, n_ctx=n_ctx)
    attn_a = functools.partial(_attn_a, z, zc, e_a_sink[0], col_q=ca_q, col_k=ca_k, col_v=ca_v, **dims)
    attn_b = functools.partial(_attn_b, z, zc, e_b_lam_q1[0], e_b_lam_k1[0], e_b_lam_q2[0], e_b_lam_k2[0],
                               e_b_subln[0], col_q=cb_q, col_k=cb_k, col_v=cb_v, lam_init=lam_init, **dims)
    w_out0 = e_w_out[0].astype(BF16)
    x2d = _out_proj([attn_a(q_from_ctx=False), attn_b(q_from_ctx=False)], w_out0, x2d, g1,
                    tm=tm, rows_per_mod=seq_len)
    xc2d = _out_proj([attn_a(q_from_ctx=True), attn_b(q_from_ctx=True)], w_out0, xc2d, cg1,
                     tm=batch * n_ctx, rows_per_mod=batch * n_ctx)
    ffn_in, ffn_down = e_ffn_w_in[0].astype(BF16), e_ffn_w_down[0].astype(BF16)
    x2d = _dense_ffn(x2d, e_norm2[0], sh2, sc2, g2, ffn_in, ffn_down, tm=512, rows_per_mod=seq_len)
    xc2d = _dense_ffn(xc2d, e_norm2[0], csh2, csc2, cg2, ffn_in, ffn_down, tm=batch * n_ctx,
                      rows_per_mod=batch * n_ctx)

    (sh1, sc1, g1, sh2, sc2, g2), (csh1, csc1, _, _, _, _) = _mod_vectors(c, c_ctx, o_w_mod[0], o_b_mod[0])
    c_q = d
    c_kv = C_KV_HEADS * HEAD_DIM
    sections = ((1, c_q), (1, c_kv), (0, c_kv))
    col_gain = pair_sections(
        jnp.concatenate([tile(o_c_qnorm[0] * qk_scale, c_q), tile(o_c_knorm[0], c_kv), ones(c_kv)]), sections)
    col_kind = kinds(*sections)
    w_in1 = pair_sections(o_w_in[0], sections).astype(BF16)
    z = _in_proj(x2d, o_norm1[0], sh1, sc1, w_in1, col_gain, col_kind, tables,
                 tm=tm, rows_per_mod=seq_len, rope=True)
    zc = _in_proj(xc2d, o_norm1[0], csh1, csc1, w_in1, col_gain, col_kind, ctx_tables,
                  tm=batch * n_ctx, rows_per_mod=batch * n_ctx, rope=False)
    o = _attn_c(z, zc, n_heads=c_q // HEAD_DIM, col_k=c_q // LANES, col_v=(c_q + c_kv) // LANES, **dims)
    x2d = _out_proj([o], o_w_out[0].astype(BF16), x2d, g1, tm=tm, rows_per_mod=seq_len)
    x2d = _moe(x2d, o_norm2[0], sh2, sc2, g2, o_router[0], o_exp_w_in.reshape(o_exp_w_in.shape[1:]),
               o_exp_w_down.reshape(o_exp_w_down.shape[1:]), rows_per_mod=seq_len)
    return x2d.reshape(batch, seq_len, d)
```

```python
import functools
import math

import jax
import jax.numpy as jnp
import numpy as np
from jax import lax
from jax.experimental import pallas as pl
from jax.experimental.pallas import tpu as pltpu

F32 = jnp.float32
BF16 = jnp.bfloat16

HEAD_DIM = 128
GRID_W = 64
ROT_AXIS = HEAD_DIM // 2
ROPE_THETA = 10000.0
EPS = 1e-6
NEG_INF = -1e30
LOG2_E = math.log2(math.e)
N_MOD = 6
WINDOW = 128
A_HEADS, A_KV_HEADS = 8, 2
B_HEADS = 4
C_KV_HEADS = 4
N_EXPERTS = 8
LANES = 128
V7X_VMEM_BYTES = 64 * 1024 * 1024
VMEM_TEMP_ALLOWANCE = 16 * 1024 * 1024


def _vmem_limit(*block_bytes):
    need = 2 * sum(block_bytes) + VMEM_TEMP_ALLOWANCE
    return int(min(need, V7X_VMEM_BYTES - 6 * 1024 * 1024))


def _params(sem, *block_bytes):
    return pltpu.CompilerParams(dimension_semantics=sem, vmem_limit_bytes=_vmem_limit(*block_bytes))


def _nt_dot(a, b):
    return lax.dot_general(a, b, (((1,), (1,)), ((), ())), preferred_element_type=F32)


def _rms_mod(x, g, shift, scale):
    ms = jnp.mean(x * x, axis=-1, keepdims=True)
    y = x * lax.rsqrt(ms + EPS) * g
    return y * (1.0 + scale) + shift


def _mod_kernel(c_ref, w_ref, b_ref, o_ref):
    c = c_ref[...]
    s = c * jax.nn.sigmoid(c)
    s_hi = s.astype(BF16).astype(F32)
    top = lax.broadcasted_iota(jnp.int32, s.shape, 0) < 8
    lhs = jnp.where(top, s_hi, s - s_hi).astype(BF16)
    acc = jnp.dot(lhs, w_ref[...].astype(BF16), preferred_element_type=F32)
    o_ref[...] = acc[0:8] + acc[8:16] + b_ref[...]


def _modulation(cvec, w_mod, b_mod):
    d, n = w_mod.shape
    tn = 1024
    return pl.pallas_call(
        _mod_kernel,
        out_shape=jax.ShapeDtypeStruct((8, n), F32),
        grid=(n // tn,),
        in_specs=[pl.BlockSpec((16, d), lambda j: (0, 0)),
                  pl.BlockSpec((d, tn), lambda j: (0, j)),
                  pl.BlockSpec((1, tn), lambda j: (0, j))],
        out_specs=pl.BlockSpec((8, tn), lambda j: (0, j)),
        compiler_params=_params(("arbitrary",), d * tn * 4, d * tn * 2),
        name="modulation",
    )(cvec, w_mod, b_mod.reshape(1, n))


def _inproj_kernel(kind_ref, x_ref, g_ref, sh_ref, sc_ref, w_ref, gc_ref, cos_ref, sa_ref, sb_ref,
                   o_ref, h_ref, *, rope):
    j = pl.program_id(1)
    tn = o_ref.shape[1]
    nch = tn // LANES

    @pl.when(j == 0)
    def _():
        h_ref[...] = _rms_mod(x_ref[...], g_ref[...], sh_ref[0], sc_ref[0]).astype(BF16)

    acc = jnp.dot(h_ref[...], w_ref[...], preferred_element_type=F32)
    for c in range(nch):
        a = acc[:, c * LANES:(c + 1) * LANES]
        kind = kind_ref[j * nch + c]

        @pl.when(kind == 0)
        def _():
            o_ref[:, c * LANES:(c + 1) * LANES] = a.astype(o_ref.dtype)

        @pl.when(kind == 1)
        def _():
            ms = jnp.mean(a * a, axis=-1, keepdims=True)
            y = a * lax.rsqrt(ms + EPS) * gc_ref[:, c * LANES:(c + 1) * LANES]
            if rope:
                y = (y * cos_ref[...] + pltpu.roll(y, LANES - 32, 1) * sa_ref[...]
                     + pltpu.roll(y, 32, 1) * sb_ref[...])
            o_ref[:, c * LANES:(c + 1) * LANES] = y.astype(o_ref.dtype)


def _in_proj(x2d, norm_g, shift, scale, w_bf16, col_gain, col_kind, tables, *, tm, rows_per_mod, rope):
    rows, d = x2d.shape
    n = w_bf16.shape[1]
    tn = 512
    cos_t, sa_t, sb_t = tables
    pos_tiles = cos_t.shape[0] // tm

    grid_spec = pltpu.PrefetchScalarGridSpec(
        num_scalar_prefetch=1,
        grid=(rows // tm, n // tn),
        in_specs=[
            pl.BlockSpec((tm, d), lambda i, j, k: (i, 0)),
            pl.BlockSpec((1, d), lambda i, j, k: (0, 0)),
            pl.BlockSpec((1, 1, d), lambda i, j, k: ((i * tm) // rows_per_mod, 0, 0)),
            pl.BlockSpec((1, 1, d), lambda i, j, k: ((i * tm) // rows_per_mod, 0, 0)),
            pl.BlockSpec((d, tn), lambda i, j, k: (0, j)),
            pl.BlockSpec((1, tn), lambda i, j, k: (0, j)),
            pl.BlockSpec((tm, LANES), lambda i, j, k: (i % pos_tiles, 0)),
            pl.BlockSpec((tm, LANES), lambda i, j, k: (i % pos_tiles, 0)),
            pl.BlockSpec((tm, LANES), lambda i, j, k: (i % pos_tiles, 0)),
        ],
        out_specs=pl.BlockSpec((tm, tn), lambda i, j, k: (i, j)),
        scratch_shapes=[pltpu.VMEM((tm, d), BF16)],
    )
    return pl.pallas_call(
        functools.partial(_inproj_kernel, rope=rope),
        out_shape=jax.ShapeDtypeStruct((rows, n), BF16),
        grid_spec=grid_spec,
        compiler_params=_params(("arbitrary", "arbitrary"), tm * d * 4, d * tn * 2, tm * tn * 2,
                                3 * tm * LANES * 4, tm * d),
        name="in_proj_rope" if rope else "in_proj_ctx",
    )(col_kind, x2d, norm_g.reshape(1, d), shift, scale, w_bf16, col_gain.reshape(1, n), cos_t, sa_t, sb_t)


def _window_mask(q0, ws, tq, nk):
    qpos = q0 + lax.broadcasted_iota(jnp.int32, (tq, nk), 0)
    kpos = ws + lax.broadcasted_iota(jnp.int32, (tq, nk), 1)
    return jnp.abs(qpos - kpos) <= WINDOW


def _attn_a_kernel(sink_ref, q_ref, kl_ref, vl_ref, kc_ref, vc_ref, o_ref, vl1_ref, vc1_ref, *slots,
                   tq, seq_len, group):
    depth = len(slots) // 2
    p_slots, sink_slots = slots[:depth], slots[depth:]
    sink = sink_ref[pl.program_id(1)] * LOG2_E
    dh = vc_ref.shape[1]
    nk = tq + 2 * WINDOW if seq_len else 0

    @pl.when(pl.program_id(1) % group == 0)
    def _():
        if seq_len:
            vl1_ref[:, :dh] = vl_ref[...]
            vl1_ref[:, dh:] = jnp.ones_like(vl_ref)
        vc1_ref[:, :dh] = vc_ref[...]
        vc1_ref[:, dh:] = jnp.ones_like(vc_ref)

    def window_start(r0):
        return pl.multiple_of(jnp.clip(r0 - WINDOW, 0, seq_len - nk), WINDOW)

    def score_stage(b, slot):
        r0 = pl.multiple_of(b * tq, tq)
        q = q_ref[pl.ds(r0, tq), :]
        s_ctx = _nt_dot(q, kc_ref[...])
        m = jnp.maximum(jnp.max(s_ctx, axis=-1, keepdims=True), sink)
        if seq_len:
            ws = window_start(r0)
            s_loc = _nt_dot(q, kl_ref[pl.ds(ws, nk), :])
            s_loc = jnp.where(_window_mask(r0, ws, tq, nk), s_loc, NEG_INF)
            m = jnp.maximum(m, jnp.max(s_loc, axis=-1, keepdims=True))
            p_slots[slot][:, :nk] = jnp.exp2(s_loc - m).astype(BF16)
        p_slots[slot][:, nk:] = jnp.exp2(s_ctx - m).astype(BF16)
        sink_slots[slot][...] = jnp.broadcast_to(jnp.exp2(sink - m), sink_slots[slot].shape)

    def value_stage(b, slot):
        r0 = pl.multiple_of(b * tq, tq)
        acc = jnp.dot(p_slots[slot][:, nk:], vc1_ref[...], preferred_element_type=F32)
        if seq_len:
            acc = acc + jnp.dot(p_slots[slot][:, :nk], vl1_ref[pl.ds(window_start(r0), nk), :],
                                preferred_element_type=F32)
        den = acc[:, dh:dh + 1] + sink_slots[slot][:, 0:1]
        o_ref[pl.ds(r0, tq), :] = (acc[:, :dh] / den).astype(o_ref.dtype)

    _pipeline_blocks(q_ref.shape[0] // tq, depth, score_stage, value_stage)


def _attn_a(z, zc, sink, *, batch, seq_len, n_ctx, q_from_ctx, col_q, col_k, col_v):
    g = A_HEADS // A_KV_HEADS
    if q_from_ctx:
        lq, tq, qsrc, depth = n_ctx, n_ctx, zc, 1
    else:
        lq, tq, qsrc, depth = seq_len, 256, z, 8
    lat_len = 0 if q_from_ctx else seq_len
    lat_block = 16 if q_from_ctx else seq_len
    lat_tiles = seq_len // lat_block
    n_keys = n_ctx + (0 if q_from_ctx else tq + 2 * WINDOW)
    grid_spec = pltpu.PrefetchScalarGridSpec(
        num_scalar_prefetch=1,
        grid=(batch, A_HEADS),
        in_specs=[
            pl.BlockSpec((lq, LANES), lambda b, h, s: (b, col_q + h)),
            pl.BlockSpec((lat_block, LANES), lambda b, h, s: (b * lat_tiles, col_k + h // g)),
            pl.BlockSpec((lat_block, LANES), lambda b, h, s: (b * lat_tiles, col_v + h // g)),
            pl.BlockSpec((n_ctx, LANES), lambda b, h, s: (b, col_k + h // g)),
            pl.BlockSpec((n_ctx, LANES), lambda b, h, s: (b, col_v + h // g)),
        ],
        out_specs=pl.BlockSpec((lq, LANES), lambda b, h, s: (b, h)),
        scratch_shapes=[pltpu.VMEM((lat_block, 2 * LANES), BF16), pltpu.VMEM((n_ctx, 2 * LANES), BF16)]
        + [pltpu.VMEM((tq, n_keys), BF16)] * depth + [pltpu.VMEM((tq, LANES), F32)] * depth,
    )
    return pl.pallas_call(
        functools.partial(_attn_a_kernel, tq=tq, seq_len=lat_len, group=g),
        out_shape=jax.ShapeDtypeStruct((batch * lq, A_HEADS * HEAD_DIM), BF16),
        grid_spec=grid_spec,
        compiler_params=_params(("arbitrary",) * 2, 2 * lq * LANES * 2, 4 * lat_block * LANES * 2,
                                4 * n_ctx * LANES * 2, depth * tq * (n_keys + 2 * LANES)),
        name="attn_window_ctx" if q_from_ctx else "attn_window",
    )(sink, qsrc, z, z, zc, zc)


def _pipeline_blocks(n_blocks, depth, score_stage, value_stage):
    score_stage(0, 0)

    def body(u, carry):
        for i in range(depth):
            b = u * depth + i
            score_stage(jnp.minimum(b + 1, n_blocks - 1), (i + 1) % depth)
            value_stage(b, i)
        return carry

    lax.fori_loop(0, n_blocks // depth, body, 0)


def _join_rows(dst_ref, lat_ref, ctx_ref, has_lat):
    n_lat = lat_ref.shape[0] if has_lat else 0
    if has_lat:
        dst_ref[0:n_lat, 0:lat_ref.shape[1]] = lat_ref[...]
    dst_ref[n_lat:n_lat + ctx_ref.shape[0], 0:ctx_ref.shape[1]] = ctx_ref[...]


def _exp2_numerators(q, k_all):
    s = _nt_dot(q, k_all)
    return jnp.exp2(s - jnp.max(s, axis=-1, keepdims=True))


def _attn_b_kernel(q1_ref, q2_ref, k1l_ref, k2l_ref, vl_ref, k1c_ref, k2c_ref, vc_ref,
                   lq1_ref, lk1_ref, lq2_ref, lk2_ref, sub_ref, o_ref, k1_ref, k2_ref, v_ref, *p_slots,
                   tq, has_lat, lam_init):
    lam = (jnp.exp(jnp.sum(lq1_ref[0] * lk1_ref[0], axis=-1, keepdims=True))
           - jnp.exp(jnp.sum(lq2_ref[0] * lk2_ref[0], axis=-1, keepdims=True)) + lam_init)
    _join_rows(k1_ref, k1l_ref, k1c_ref, has_lat)
    _join_rows(k2_ref, k2l_ref, k2c_ref, has_lat)
    _join_rows(v_ref, vl_ref, vc_ref, has_lat)

    def score_stage(b, slot):
        r0 = pl.multiple_of(b * tq, tq)
        p1 = _exp2_numerators(q1_ref[pl.ds(r0, tq), :], k1_ref[...])
        p2 = _exp2_numerators(q2_ref[pl.ds(r0, tq), :], k2_ref[...])
        w1 = 1.0 / jnp.sum(p1, axis=-1, keepdims=True)
        w2 = lam / jnp.sum(p2, axis=-1, keepdims=True)
        p_slots[slot][...] = (p1 * w1 - p2 * w2).astype(BF16)

    def value_stage(b, slot):
        r0 = pl.multiple_of(b * tq, tq)
        o = jnp.dot(p_slots[slot][...], v_ref[...], preferred_element_type=F32)
        ms = jnp.mean(o * o, axis=-1, keepdims=True)
        o = o * lax.rsqrt(ms + EPS) * sub_ref[...] * (1.0 - lam_init)
        o_ref[pl.ds(r0, tq), :] = o.astype(o_ref.dtype)

    _pipeline_blocks(q1_ref.shape[0] // tq, len(p_slots), score_stage, value_stage)


def _attn_b(z, zc, lq1, lk1, lq2, lk2, subln, *, batch, seq_len, n_ctx, q_from_ctx, col_q, col_k, col_v,
            lam_init):
    dv = 2 * HEAD_DIM
    if q_from_ctx:
        lq, bq, tq, qsrc, depth = n_ctx, n_ctx, n_ctx, zc, 1
    else:
        lq, bq, tq, qsrc, depth = seq_len, seq_len, 256, z, 4
    nqb = lq // bq
    lat_block = 16 if q_from_ctx else seq_len
    lat_tiles = seq_len // lat_block
    n_keys = n_ctx if q_from_ctx else seq_len + n_ctx
    lam_spec = pl.BlockSpec((1, 1, HEAD_DIM), lambda b, h, i: (h, 0, 0))
    in_specs = [
        pl.BlockSpec((bq, LANES), lambda b, h, i: (b * nqb + i, col_q + 2 * h)),
        pl.BlockSpec((bq, LANES), lambda b, h, i: (b * nqb + i, col_q + 2 * h + 1)),
        pl.BlockSpec((lat_block, LANES), lambda b, h, i: (b * lat_tiles, col_k + 2 * h)),
        pl.BlockSpec((lat_block, LANES), lambda b, h, i: (b * lat_tiles, col_k + 2 * h + 1)),
        pl.BlockSpec((lat_block, dv), lambda b, h, i: (b * lat_tiles, col_v // 2 + h)),
        pl.BlockSpec((n_ctx, LANES), lambda b, h, i: (b, col_k + 2 * h)),
        pl.BlockSpec((n_ctx, LANES), lambda b, h, i: (b, col_k + 2 * h + 1)),
        pl.BlockSpec((n_ctx, dv), lambda b, h, i: (b, col_v // 2 + h)),
        lam_spec, lam_spec, lam_spec, lam_spec,
        pl.BlockSpec((1, dv), lambda b, h, i: (0, 0)),
    ]
    return pl.pallas_call(
        functools.partial(_attn_b_kernel, tq=tq, has_lat=not q_from_ctx, lam_init=lam_init),
        out_shape=jax.ShapeDtypeStruct((batch * lq, B_HEADS * dv), BF16),
        grid=(batch, B_HEADS, nqb),
        in_specs=in_specs,
        out_specs=pl.BlockSpec((bq, dv), lambda b, h, i: (b * nqb + i, h)),
        scratch_shapes=[pltpu.VMEM((n_keys, LANES), BF16), pltpu.VMEM((n_keys, LANES), BF16),
                        pltpu.VMEM((n_keys, dv), BF16)] + [pltpu.VMEM((tq, n_keys), BF16)] * depth,
        compiler_params=_params(("arbitrary",) * 3, 2 * bq * LANES * 2, 4 * lat_block * LANES * 2,
                                4 * n_ctx * LANES * 2, bq * dv * 2, (4 * LANES + depth * tq) * n_keys,
                                4 * tq * n_keys * 4 // 2),
        name="attn_diff_ctx" if q_from_ctx else "attn_diff",
    )(qsrc, qsrc, z, z, z, zc, zc, zc,
      lq1.reshape(B_HEADS, 1, HEAD_DIM), lk1.reshape(B_HEADS, 1, HEAD_DIM),
      lq2.reshape(B_HEADS, 1, HEAD_DIM), lk2.reshape(B_HEADS, 1, HEAD_DIM), subln.reshape(1, dv))


def _attn_c_kernel(q_ref, kl_ref, vl_ref, kc_ref, vc_ref, o_ref, k_ref, v1_ref, *p_slots, tq, group):
    dh = vl_ref.shape[1]

    @pl.when(pl.program_id(1) % group == 0)
    def _():
        _join_rows(k_ref, kl_ref, kc_ref, True)
        _join_rows(v1_ref, vl_ref, vc_ref, True)
        v1_ref[:, dh:] = jnp.ones((v1_ref.shape[0], v1_ref.shape[1] - dh), v1_ref.dtype)

    def score_stage(b, slot):
        r0 = pl.multiple_of(b * tq, tq)
        p_slots[slot][...] = _exp2_numerators(q_ref[pl.ds(r0, tq), :], k_ref[...]).astype(BF16)

    def value_stage(b, slot):
        r0 = pl.multiple_of(b * tq, tq)
        acc = jnp.dot(p_slots[slot][...], v1_ref[...], preferred_element_type=F32)
        o_ref[pl.ds(r0, tq), :] = (acc[:, :dh] / acc[:, dh:dh + 1]).astype(o_ref.dtype)

    _pipeline_blocks(q_ref.shape[0] // tq, len(p_slots), score_stage, value_stage)


def _attn_c(z, zc, *, batch, seq_len, n_ctx, n_heads, col_k, col_v):
    g = n_heads // C_KV_HEADS
    tq, depth = 256, 8
    n_keys = seq_len + n_ctx
    return pl.pallas_call(
        functools.partial(_attn_c_kernel, tq=tq, group=g),
        out_shape=jax.ShapeDtypeStruct((batch * seq_len, n_heads * HEAD_DIM), BF16),
        grid=(batch, n_heads),
        scratch_shapes=[pltpu.VMEM((n_keys, LANES), BF16), pltpu.VMEM((n_keys, 2 * LANES), BF16)]
        + [pltpu.VMEM((tq, n_keys), BF16)] * depth,
        in_specs=[
            pl.BlockSpec((seq_len, LANES), lambda b, h: (b, h)),
            pl.BlockSpec((seq_len, LANES), lambda b, h: (b, col_k + h // g)),
            pl.BlockSpec((seq_len, LANES), lambda b, h: (b, col_v + h // g)),
            pl.BlockSpec((n_ctx, LANES), lambda b, h: (b, col_k + h // g)),
            pl.BlockSpec((n_ctx, LANES), lambda b, h: (b, col_v + h // g)),
        ],
        out_specs=pl.BlockSpec((seq_len, LANES), lambda b, h: (b, h)),
        compiler_params=_params(("arbitrary",) * 2, 4 * seq_len * LANES * 2, 2 * n_ctx * LANES * 2,
                                (3 * LANES + depth * tq) * n_keys, 3 * tq * n_keys * 4 // 2),
        name="attn_gqa",
    )(z, z, z, zc, zc)


def _outproj_kernel(*refs, n_lhs):
    lhs = refs[:n_lhs]
    w_ref, res_ref, gate_ref, o_ref = refs[n_lhs:]
    acc = None
    k0 = 0
    for a_ref in lhs:
        kw = a_ref.shape[1]
        part = jnp.dot(a_ref[...], w_ref[k0:k0 + kw, :], preferred_element_type=F32)
        acc = part if acc is None else acc + part
        k0 += kw
    o_ref[...] = res_ref[...] + gate_ref[0] * acc


def _out_proj(lhs_list, w_bf16, res2d, gate, *, tm, rows_per_mod):
    rows, n = res2d.shape
    kdim = w_bf16.shape[0]
    tn = 512
    in_specs = [pl.BlockSpec((tm, a.shape[1]), lambda i, j: (i, 0)) for a in lhs_list]
    in_specs += [
        pl.BlockSpec((kdim, tn), lambda i, j: (0, j)),
        pl.BlockSpec((tm, tn), lambda i, j: (i, j)),
        pl.BlockSpec((1, 1, tn), lambda i, j: ((i * tm) // rows_per_mod, 0, j)),
    ]
    return pl.pallas_call(
        functools.partial(_outproj_kernel, n_lhs=len(lhs_list)),
        out_shape=jax.ShapeDtypeStruct((rows, n), F32),
        grid=(rows // tm, n // tn),
        in_specs=in_specs,
        out_specs=pl.BlockSpec((tm, tn), lambda i, j: (i, j)),
        compiler_params=_params(("arbitrary", "arbitrary"), tm * kdim * 2, kdim * tn * 2, 2 * tm * tn * 4),
        name="out_proj",
    )(*lhs_list, w_bf16, res2d, gate)


def _ffn_kernel(x_ref, g_ref, sh_ref, sc_ref, gate_ref, wg_ref, wu_ref, wd_ref, o_ref, h_ref, acc_ref):
    f = pl.program_id(1)

    @pl.when(f == 0)
    def _():
        h_ref[...] = _rms_mod(x_ref[...], g_ref[...], sh_ref[0], sc_ref[0]).astype(BF16)
        acc_ref[...] = jnp.zeros_like(acc_ref)

    h = h_ref[...]
    g_ = jnp.dot(h, wg_ref[...], preferred_element_type=F32)
    u_ = jnp.dot(h, wu_ref[...], preferred_element_type=F32)
    a = (g_ * jax.nn.sigmoid(g_) * u_).astype(BF16)
    acc_ref[...] += jnp.dot(a, wd_ref[...], preferred_element_type=F32)

    @pl.when(f == pl.num_programs(1) - 1)
    def _():
        o_ref[...] = x_ref[...] + gate_ref[0] * acc_ref[...]


def _dense_ffn(x2d, norm_g, shift, scale, gate, w_in_bf16, w_down_bf16, *, tm, rows_per_mod):
    rows, d = x2d.shape
    d_ff = w_down_bf16.shape[0]
    tf = 512
    nf = d_ff // tf
    mod_spec = pl.BlockSpec((1, 1, d), lambda i, f: ((i * tm) // rows_per_mod, 0, 0))
    return pl.pallas_call(
        _ffn_kernel,
        out_shape=jax.ShapeDtypeStruct((rows, d), F32),
        grid=(rows // tm, nf),
        in_specs=[
            pl.BlockSpec((tm, d), lambda i, f: (i, 0)),
            pl.BlockSpec((1, d), lambda i, f: (0, 0)),
            mod_spec, mod_spec, mod_spec,
            pl.BlockSpec((d, tf), lambda i, f: (0, f)),
            pl.BlockSpec((d, tf), lambda i, f: (0, nf + f)),
            pl.BlockSpec((tf, d), lambda i, f: (f, 0)),
        ],
        out_specs=pl.BlockSpec((tm, d), lambda i, f: (i, 0)),
        scratch_shapes=[pltpu.VMEM((tm, d), BF16), pltpu.VMEM((tm, d), F32)],
        compiler_params=_params(("arbitrary", "arbitrary"), 2 * tm * d * 4, 3 * d * tf * 2, tm * d * 3),
        name="dense_swiglu",
    )(x2d, norm_g.reshape(1, d), shift, scale, gate, w_in_bf16, w_in_bf16, w_down_bf16)


def _split3(v):
    hi = v.astype(BF16)
    r = v - hi.astype(F32)
    mid = r.astype(BF16)
    lo = (r - mid.astype(F32)).astype(BF16)
    return hi, mid, lo


def _router_kernel(x_ref, g_ref, sh_ref, sc_ref, wr_ref, h_ref, ids_ref, gates_ref, cnt_ref, carry_ref):
    i = pl.program_id(0)
    tm = x_ref.shape[0]

    @pl.when(i == 0)
    def _():
        carry_ref[...] = jnp.zeros_like(carry_ref)

    h = _rms_mod(x_ref[...], g_ref[...], sh_ref[0], sc_ref[0])
    h_ref[...] = h
    h0, h1, h2 = _split3(h)
    w0, w1, w2 = _split3(wr_ref[...])
    dot = functools.partial(jnp.dot, preferred_element_type=F32)
    logits = (dot(h0, w0) + (dot(h0, w1) + dot(h1, w0))
              + (dot(h0, w2) + dot(h1, w1) + dot(h2, w0)))
    lane = lax.broadcasted_iota(jnp.int32, (tm, LANES), 1)
    logits = jnp.where(lane < N_EXPERTS, logits, -jnp.inf)
    v0 = jnp.max(logits, axis=-1, keepdims=True)
    i0 = jnp.min(jnp.where(logits == v0, lane, LANES), axis=-1, keepdims=True)
    rest = jnp.where(lane == i0, -jnp.inf, logits)
    v1 = jnp.max(rest, axis=-1, keepdims=True)
    i1 = jnp.min(jnp.where(rest == v1, lane, LANES), axis=-1, keepdims=True)
    e1 = jnp.exp(v1 - v0)
    g0 = 1.0 / (1.0 + e1)
    g1 = e1 / (1.0 + e1)

    sel = (lane == i0) | (lane == i1)
    row = lax.broadcasted_iota(jnp.int32, (tm, tm), 0)
    col = lax.broadcasted_iota(jnp.int32, (tm, tm), 1)
    tri = (col < row).astype(BF16)
    before = jnp.dot(tri, sel.astype(BF16), preferred_element_type=F32) + carry_ref[...]
    r0 = jnp.sum(jnp.where(lane == i0, before, 0.0), axis=-1, keepdims=True).astype(jnp.int32)
    r1 = jnp.sum(jnp.where(lane == i1, before, 0.0), axis=-1, keepdims=True).astype(jnp.int32)
    total = carry_ref[...] + jnp.sum(sel.astype(F32), axis=0, keepdims=True)
    carry_ref[...] = total

    ids_ref[...] = jnp.where(lane == 0, i0, jnp.where(lane == 1, i1, jnp.where(lane == 2, r0, r1)))
    gates_ref[...] = jnp.where(lane == 0, g0, g1)
    cnt_ref[...] = jnp.broadcast_to(total, cnt_ref.shape).astype(jnp.int32)


def _router(x2d, norm_g, shift, scale, w_router, *, tm, rows_per_mod):
    rows, d = x2d.shape
    wr = jnp.zeros((d, LANES), F32).at[:, :N_EXPERTS].set(w_router)
    mod_spec = pl.BlockSpec((1, 1, d), lambda i: ((i * tm) // rows_per_mod, 0, 0))
    return pl.pallas_call(
        _router_kernel,
        out_shape=(jax.ShapeDtypeStruct((rows, d), F32),
                   jax.ShapeDtypeStruct((rows, LANES), jnp.int32),
                   jax.ShapeDtypeStruct((rows, LANES), F32),
                   jax.ShapeDtypeStruct((8, LANES), jnp.int32)),
        grid=(rows // tm,),
        in_specs=[pl.BlockSpec((tm, d), lambda i: (i, 0)),
                  pl.BlockSpec((1, d), lambda i: (0, 0)),
                  mod_spec, mod_spec,
                  pl.BlockSpec((d, LANES), lambda i: (0, 0))],
        out_specs=(pl.BlockSpec((tm, d), lambda i: (i, 0)),
                   pl.BlockSpec((tm, LANES), lambda i: (i, 0)),
                   pl.BlockSpec((tm, LANES), lambda i: (i, 0)),
                   pl.BlockSpec((8, LANES), lambda i: (0, 0))),
        scratch_shapes=[pltpu.VMEM((1, LANES), F32)],
        compiler_params=_params(("arbitrary",), 2 * tm * d * 4, d * LANES * 4, 2 * tm * LANES * 4),
        name="router",
    )(x2d, norm_g.reshape(1, d), shift, scale, wr)


GATHER_UNROLL = 8


def _start_row_gather(src_hbm, idx_ref, base, dst_ref, sem):
    def body(r8, c):
        for u in range(GATHER_UNROLL):
            r = r8 * GATHER_UNROLL + u
            pltpu.make_async_copy(src_hbm.at[pl.ds(idx_ref[base + r], 1), :],
                                  dst_ref.at[pl.ds(r, 1), :], sem).start()
        return c

    lax.fori_loop(0, dst_ref.shape[0] // GATHER_UNROLL, body, 0)


def _wait_row_gather(src_hbm, dst_ref, sem):
    pltpu.make_async_copy(src_hbm.at[pl.ds(0, dst_ref.shape[0]), :], dst_ref, sem).wait()


def _dispatch_kernel(idx_ref, src_hbm, o_ref, buf_ref, sem):
    tr = o_ref.shape[0]
    i, n = pl.program_id(0), pl.num_programs(0)
    slot = i & 1

    @pl.when(i == 0)
    def _():
        _start_row_gather(src_hbm, idx_ref, 0, buf_ref.at[0], sem.at[0])

    @pl.when(i + 1 < n)
    def _():
        _start_row_gather(src_hbm, idx_ref, (i + 1) * tr, buf_ref.at[1 - slot], sem.at[1 - slot])

    _wait_row_gather(src_hbm, buf_ref.at[slot], sem.at[slot])
    o_ref[...] = buf_ref[slot].astype(o_ref.dtype)


def _dispatch(h2d, row_tok, *, tr):
    n_rows = row_tok.shape[0]
    d = h2d.shape[1]
    grid_spec = pltpu.PrefetchScalarGridSpec(
        num_scalar_prefetch=1,
        grid=(n_rows // tr,),
        in_specs=[pl.BlockSpec(memory_space=pl.ANY)],
        out_specs=pl.BlockSpec((tr, d), lambda i, idx: (i, 0)),
        scratch_shapes=[pltpu.VMEM((2, tr, d), F32), pltpu.SemaphoreType.DMA((2,))],
    )
    return pl.pallas_call(
        _dispatch_kernel,
        out_shape=jax.ShapeDtypeStruct((n_rows, d), BF16),
        grid_spec=grid_spec,
        compiler_params=_params(("arbitrary",), tr * d * 2, tr * d * 4),
        name="expert_dispatch",
    )(row_tok, h2d)


def _combine_kernel(d0_ref, d1_ref, ys_hbm, x_ref, gate_ref, rg_ref, o_ref, a_ref, b_ref, sem):
    tt = x_ref.shape[0]
    i, n = pl.program_id(0), pl.num_programs(0)
    slot = i & 1

    def start(tile, s):
        _start_row_gather(ys_hbm, d0_ref, tile * tt, a_ref.at[s], sem.at[s, 0])
        _start_row_gather(ys_hbm, d1_ref, tile * tt, b_ref.at[s], sem.at[s, 1])

    @pl.when(i == 0)
    def _():
        start(0, 0)

    @pl.when(i + 1 < n)
    def _():
        start(i + 1, 1 - slot)

    _wait_row_gather(ys_hbm, a_ref.at[slot], sem.at[slot, 0])
    _wait_row_gather(ys_hbm, b_ref.at[slot], sem.at[slot, 1])
    rg = rg_ref[...]
    mix = rg[:, 0:1] * a_ref[slot] + rg[:, 1:2] * b_ref[slot]
    o_ref[...] = x_ref[...] + gate_ref[0] * mix


def _combine(ys, dest0, dest1, x2d, gate, row_gates, *, tt, rows_per_mod):
    rows, d = x2d.shape
    grid_spec = pltpu.PrefetchScalarGridSpec(
        num_scalar_prefetch=2,
        grid=(rows // tt,),
        in_specs=[pl.BlockSpec(memory_space=pl.ANY),
                  pl.BlockSpec((tt, d), lambda i, a, b: (i, 0)),
                  pl.BlockSpec((1, 1, d), lambda i, a, b: ((i * tt) // rows_per_mod, 0, 0)),
                  pl.BlockSpec((tt, LANES), lambda i, a, b: (i, 0))],
        out_specs=pl.BlockSpec((tt, d), lambda i, a, b: (i, 0)),
        scratch_shapes=[pltpu.VMEM((2, tt, d), F32), pltpu.VMEM((2, tt, d), F32),
                        pltpu.SemaphoreType.DMA((2, 2))],
    )
    return pl.pallas_call(
        _combine_kernel,
        out_shape=jax.ShapeDtypeStruct((rows, d), F32),
        grid_spec=grid_spec,
        compiler_params=_params(("arbitrary",), 2 * tt * d * 4, 2 * tt * d * 4),
        name="expert_combine",
    )(dest0, dest1, ys, x2d, gate, row_gates)


WEIGHT_DMA_PRIORITY = 1


def _stream_group_weights(sched, copies, cast):
    te_ref, first_ref, grp_ref, ngrp_ref, nxt_ref = sched
    j, r = pl.program_id(0), pl.program_id(1)
    nj = pl.num_programs(0)

    @pl.when(first_ref[r] == 1)
    def _():
        g = grp_ref[r]
        ng = ngrp_ref[0]
        slot = (j * ng + g) & 1

        @pl.when((j == 0) & (g == 0))
        def _():
            for cp in copies(te_ref[r], j, slot):
                cp.start(priority=WEIGHT_DMA_PRIORITY)

        for cp in copies(te_ref[r], j, slot):
            cp.wait()
        last = g == ng - 1

        @pl.when(jnp.logical_not(last & (j == nj - 1)))
        def _():
            for cp in copies(nxt_ref[r], jnp.where(last, j + 1, j), 1 - slot):
                cp.start(priority=WEIGHT_DMA_PRIORITY)

        cast(slot)


def _gmm_up_kernel(te_ref, first_ref, grp_ref, ngrp_ref, nxt_ref, nv_ref, x_ref, w_hbm, o_ref,
                   stage_ref, wbf_ref, sem):
    tn = o_ref.shape[1]
    nj = pl.num_programs(0)

    def copies(e, j, slot):
        return [pltpu.make_async_copy(w_hbm.at[e, :, pl.ds(pl.multiple_of((half * nj + j) * tn, tn), tn)],
                                      stage_ref.at[slot, half], sem.at[slot, half]) for half in range(2)]

    def cast(slot):
        wbf_ref[0] = stage_ref[slot, 0].astype(BF16)
        wbf_ref[1] = stage_ref[slot, 1].astype(BF16)

    _stream_group_weights((te_ref, first_ref, grp_ref, ngrp_ref, nxt_ref), copies, cast)
    valid = pl.program_id(1) < nv_ref[0]

    @pl.when(valid)
    def _():
        x = x_ref[...]
        g_ = jnp.dot(x, wbf_ref[0], preferred_element_type=F32)
        u_ = jnp.dot(x, wbf_ref[1], preferred_element_type=F32)
        o_ref[...] = (g_ * jax.nn.sigmoid(g_) * u_).astype(o_ref.dtype)

    @pl.when(jnp.logical_not(valid))
    def _():
        o_ref[...] = jnp.zeros_like(o_ref)


def _gmm_down_kernel(te_ref, first_ref, grp_ref, ngrp_ref, nxt_ref, nv_ref, h_ref, w_hbm, o_ref,
                     stage_ref, wbf_ref, sem):
    tn = o_ref.shape[1]

    def copies(e, j, slot):
        return [pltpu.make_async_copy(w_hbm.at[e, :, pl.ds(pl.multiple_of(j * tn, tn), tn)],
                                      stage_ref.at[slot], sem.at[slot])]

    def cast(slot):
        wbf_ref[...] = stage_ref[slot].astype(BF16)

    _stream_group_weights((te_ref, first_ref, grp_ref, ngrp_ref, nxt_ref), copies, cast)
    valid = pl.program_id(1) < nv_ref[0]

    @pl.when(valid)
    def _():
        o_ref[...] = jnp.dot(h_ref[...], wbf_ref[...], preferred_element_type=F32)

    @pl.when(jnp.logical_not(valid))
    def _():
        o_ref[...] = jnp.zeros_like(o_ref)


def _expert_ffn(xs, w_in, w_down, sched, n_valid, *, tr):
    n_rows, d = xs.shape
    d_ff = w_down.shape[1]
    n_tiles = n_rows // tr
    n_sched = len(sched) + 1
    tn1 = 1024
    nj1 = d_ff // tn1

    def row_map(*args):
        r, nv = args[1], args[-1]
        return (jnp.minimum(r, nv[0] - 1), 0)

    def out_map(*args):
        return (args[1], args[0])

    up_spec = pltpu.PrefetchScalarGridSpec(
        num_scalar_prefetch=n_sched,
        grid=(nj1, n_tiles),
        in_specs=[pl.BlockSpec((tr, d), row_map), pl.BlockSpec(memory_space=pl.ANY)],
        out_specs=pl.BlockSpec((tr, tn1), out_map),
        scratch_shapes=[pltpu.VMEM((2, 2, d, tn1), F32), pltpu.VMEM((2, d, tn1), BF16),
                        pltpu.SemaphoreType.DMA((2, 2))],
    )
    stage1 = 4 * d * tn1 * 4
    hidden = pl.pallas_call(
        _gmm_up_kernel,
        out_shape=jax.ShapeDtypeStruct((n_rows, d_ff), BF16),
        grid_spec=up_spec,
        compiler_params=pltpu.CompilerParams(
            dimension_semantics=("arbitrary", "arbitrary"),
            vmem_limit_bytes=stage1 + 2 * d * tn1 * 2 + 2 * (tr * d * 2 + tr * tn1 * 2) + VMEM_TEMP_ALLOWANCE // 2),
        name="expert_up",
    )(*sched, n_valid, xs, w_in)

    tn2 = 512
    down_spec = pltpu.PrefetchScalarGridSpec(
        num_scalar_prefetch=n_sched,
        grid=(d // tn2, n_tiles),
        in_specs=[pl.BlockSpec((tr, d_ff), row_map), pl.BlockSpec(memory_space=pl.ANY)],
        out_specs=pl.BlockSpec((tr, tn2), out_map),
        scratch_shapes=[pltpu.VMEM((2, d_ff, tn2), F32), pltpu.VMEM((d_ff, tn2), BF16),
                        pltpu.SemaphoreType.DMA((2,))],
    )
    stage2 = 2 * d_ff * tn2 * 4
    return pl.pallas_call(
        _gmm_down_kernel,
        out_shape=jax.ShapeDtypeStruct((n_rows, d), F32),
        grid_spec=down_spec,
        compiler_params=pltpu.CompilerParams(
            dimension_semantics=("arbitrary", "arbitrary"),
            vmem_limit_bytes=stage2 + d_ff * tn2 * 2 + 2 * (tr * d_ff * 2 + tr * tn2 * 4) + VMEM_TEMP_ALLOWANCE // 2),
        name="expert_down",
    )(*sched, n_valid, hidden, w_down)


def _moe(x2d, norm_g, shift, scale, gate, w_router, w_in, w_down, *, rows_per_mod):
    rows, d = x2d.shape
    tr = 256
    h2d, ids, gates, counts = _router(x2d, norm_g, shift, scale, w_router, tm=512, rows_per_mod=rows_per_mod)
    counts = counts[0, :N_EXPERTS]
    padded = (counts + tr - 1) // tr * tr
    pad_end = jnp.cumsum(padded)
    pad_start = pad_end - padded
    dest0 = pad_start[ids[:, 0]] + ids[:, 2]
    dest1 = pad_start[ids[:, 1]] + ids[:, 3]
    n_rows = 2 * rows + N_EXPERTS * tr
    n_tiles = n_rows // tr
    tok = jnp.arange(rows, dtype=jnp.int32)
    row_tok = jnp.zeros((n_rows,), jnp.int32).at[jnp.concatenate([dest0, dest1])].set(
        jnp.concatenate([tok, tok]))
    tile_expert = jnp.minimum(
        jnp.sum(jnp.arange(n_tiles, dtype=jnp.int32)[:, None] * tr >= pad_end[None, :], axis=1),
        N_EXPERTS - 1).astype(jnp.int32)
    n_valid = (pad_end[-1:] // tr).astype(jnp.int32)
    tile_id = jnp.arange(n_tiles, dtype=jnp.int32)
    prev_expert = jnp.concatenate([jnp.full((1,), -1, jnp.int32), tile_expert[:-1]])
    first = ((tile_id < n_valid[0]) & (tile_expert != prev_expert)).astype(jnp.int32)
    grp = jnp.cumsum(first).astype(jnp.int32) - 1
    n_grp = jnp.sum(first, keepdims=True).astype(jnp.int32)
    expert_id = jnp.arange(N_EXPERTS, dtype=jnp.int32)
    grp_expert = jnp.sort(jnp.where(counts > 0, expert_id, N_EXPERTS))
    nxt = grp_expert[(grp + 1) % n_grp[0]].astype(jnp.int32)
    sched = (tile_expert, first, grp, n_grp, nxt)

    xs = _dispatch(h2d, row_tok, tr=tr)
    ys = _expert_ffn(xs, w_in, w_down, sched, n_valid, tr=tr)
    return _combine(ys, dest0, dest1, x2d, gate, gates, tt=256, rows_per_mod=rows_per_mod)


def _rope_tables(seq_len):
    t = np.arange(seq_len)
    row = (t // GRID_W).astype(np.float64)[:, None]
    col = (t % GRID_W).astype(np.float64)[:, None]
    inv = ROPE_THETA ** (-np.arange(0, ROT_AXIS, 2, dtype=np.float64) / ROT_AXIS)
    ar, ac = row * inv, col * inv
    zero = np.zeros_like(ar)
    cos = np.concatenate([np.cos(ar), np.cos(ar), np.cos(ac), np.cos(ac)], axis=-1)
    s_next = np.concatenate([-np.sin(ar), zero, -np.sin(ac), zero], axis=-1)
    s_prev = np.concatenate([zero, np.sin(ar), zero, np.sin(ac)], axis=-1)
    return tuple(jnp.asarray(a.astype(np.float32)) for a in (cos, s_next, s_prev))


def _mod_vectors(c, c_ctx, w_mod, b_mod):
    batch, d = c.shape
    cvec = jnp.zeros((8, d), F32).at[:batch].set(c).at[batch].set(c_ctx)
    m = _modulation(jnp.concatenate([cvec, cvec], axis=0), w_mod, b_mod)
    lat = m[:batch].reshape(batch, N_MOD, 1, d)
    ctx = m[batch].reshape(N_MOD, 1, 1, d)
    return [lat[:, k] for k in range(N_MOD)], [ctx[k] for k in range(N_MOD)]


def kernel(x, c, ctx, c_ctx, e_norm1, e_norm2, e_w_mod, e_b_mod, e_w_in, e_w_out, e_a_qnorm, e_a_knorm,
           e_a_sink, e_b_qnorm, e_b_knorm, e_b_lam_q1, e_b_lam_k1, e_b_lam_q2, e_b_lam_k2, e_b_subln,
           e_ffn_w_in, e_ffn_w_down, o_norm1, o_norm2, o_w_mod, o_b_mod, o_w_in, o_w_out, o_c_qnorm,
           o_c_knorm, o_router, o_exp_w_in, o_exp_w_down):
    batch, seq_len, d = x.shape
    n_ctx = ctx.shape[1]
    qk_scale = HEAD_DIM ** -0.5 * LOG2_E
    tables = _rope_tables(seq_len)
    ctx_tables = tuple(t[:batch * n_ctx] for t in tables)
    x2d = x.reshape(batch * seq_len, d)
    xc2d = ctx.reshape(batch * n_ctx, d)
    tm = 1024

    (sh1, sc1, g1, sh2, sc2, g2), (csh1, csc1, cg1, csh2, csc2, cg2) = _mod_vectors(
        c, c_ctx, e_w_mod[0], e_b_mod[0])
    a_q, a_kv = A_HEADS * HEAD_DIM, A_KV_HEADS * HEAD_DIM
    b_qk, b_v = B_HEADS * 2 * HEAD_DIM, B_HEADS * 2 * HEAD_DIM
    ones = lambda n: jnp.ones((n,), F32)
    tile = lambda v, n: jnp.tile(v, n // HEAD_DIM)
    col_gain = jnp.concatenate([tile(e_a_qnorm[0] * qk_scale, a_q), tile(e_a_knorm[0], a_kv), ones(a_kv),
                                tile(e_b_qnorm[0] * qk_scale, b_qk), tile(e_b_knorm[0], b_qk), ones(b_v)])
    kinds = lambda *pairs: jnp.concatenate([jnp.full((n // HEAD_DIM,), k, jnp.int32) for k, n in pairs])
    col_kind = kinds((1, a_q), (1, a_kv), (0, a_kv), (1, b_qk), (1, b_qk), (0, b_v))
    w_in0 = e_w_in[0].astype(BF16)
    z = _in_proj(x2d, e_norm1[0], sh1, sc1, w_in0, col_gain, col_kind, tables,
                 tm=tm, rows_per_mod=seq_len, rope=True)
    zc = _in_proj(xc2d, e_norm1[0], csh1, csc1, w_in0, col_gain, col_kind, ctx_tables,
                  tm=batch * n_ctx, rows_per_mod=batch * n_ctx, rope=False)

    ca_q, ca_k, ca_v = 0, a_q // LANES, (a_q + a_kv) // LANES
    cb_q = (a_q + 2 * a_kv) // LANES
    cb_k, cb_v = cb_q + b_qk // LANES, cb_q + 2 * b_qk // LANES
    lam_init = 0.8 - 0.6 * math.exp(-0.3 * 0)
    dims = dict(batch=batch, seq_len=seq_len, n_ctx=n_ctx)
    attn_a = functools.partial(_attn_a, z, zc, e_a_sink[0], col_q=ca_q, col_k=ca_k, col_v=ca_v, **dims)
    attn_b = functools.partial(_attn_b, z, zc, e_b_lam_q1[0], e_b_lam_k1[0], e_b_lam_q2[0], e_b_lam_k2[0],
                               e_b_subln[0], col_q=cb_q, col_k=cb_k, col_v=cb_v, lam_init=lam_init, **dims)
    w_out0 = e_w_out[0].astype(BF16)
    x2d = _out_proj([attn_a(q_from_ctx=False), attn_b(q_from_ctx=False)], w_out0, x2d, g1,
                    tm=tm, rows_per_mod=seq_len)
    xc2d = _out_proj([attn_a(q_from_ctx=True), attn_b(q_from_ctx=True)], w_out0, xc2d, cg1,
                     tm=batch * n_ctx, rows_per_mod=batch * n_ctx)
    ffn_in, ffn_down = e_ffn_w_in[0].astype(BF16), e_ffn_w_down[0].astype(BF16)
    x2d = _dense_ffn(x2d, e_norm2[0], sh2, sc2, g2, ffn_in, ffn_down, tm=512, rows_per_mod=seq_len)
    xc2d = _dense_ffn(xc2d, e_norm2[0], csh2, csc2, cg2, ffn_in, ffn_down, tm=batch * n_ctx,
                      rows_per_mod=batch * n_ctx)

    (sh1, sc1, g1, sh2, sc2, g2), (csh1, csc1, _, _, _, _) = _mod_vectors(c, c_ctx, o_w_mod[0], o_b_mod[0])
    c_q = d
    c_kv = C_KV_HEADS * HEAD_DIM
    col_gain = jnp.concatenate([tile(o_c_qnorm[0] * qk_scale, c_q), tile(o_c_knorm[0], c_kv), ones(c_kv)])
    col_kind = kinds((1, c_q), (1, c_kv), (0, c_kv))
    w_in1 = o_w_in[0].astype(BF16)
    z = _in_proj(x2d, o_norm1[0], sh1, sc1, w_in1, col_gain, col_kind, tables,
                 tm=tm, rows_per_mod=seq_len, rope=True)
    zc = _in_proj(xc2d, o_norm1[0], csh1, csc1, w_in1, col_gain, col_kind, ctx_tables,
                  tm=batch * n_ctx, rows_per_mod=batch * n_ctx, rope=False)
    o = _attn_c(z, zc, n_heads=c_q // HEAD_DIM, col_k=c_q // LANES, col_v=(c_q + c_kv) // LANES, **dims)
    x2d = _out_proj([o], o_w_out[0].astype(BF16), x2d, g1, tm=tm, rows_per_mod=seq_len)
    x2d = _moe(x2d, o_norm2[0], sh2, sc2, g2, o_router[0], o_exp_w_in.reshape(o_exp_w_in.shape[1:]),
               o_exp_w_down.reshape(o_exp_w_down.shape[1:]), rows_per_mod=seq_len)
    return x2d.reshape(batch, seq_len, d)
```

```python
import functools
import math

import jax
import jax.numpy as jnp
import numpy as np
from jax import lax
from jax.experimental import pallas as pl
from jax.experimental.pallas import tpu as pltpu

F32 = jnp.float32
BF16 = jnp.bfloat16

HEAD_DIM = 128
GRID_W = 64
ROT_AXIS = HEAD_DIM // 2
ROPE_THETA = 10000.0
EPS = 1e-6
NEG_INF = -1e30
LOG2_E = math.log2(math.e)
N_MOD = 6
WINDOW = 128
A_HEADS, A_KV_HEADS = 8, 2
B_HEADS = 4
C_KV_HEADS = 4
N_EXPERTS = 8
LANES = 128
V7X_VMEM_BYTES = 64 * 1024 * 1024
VMEM_TEMP_ALLOWANCE = 16 * 1024 * 1024


def _vmem_limit(*block_bytes):
    need = 2 * sum(block_bytes) + VMEM_TEMP_ALLOWANCE
    return int(min(need, V7X_VMEM_BYTES - 6 * 1024 * 1024))


def _params(sem, *block_bytes):
    return pltpu.CompilerParams(dimension_semantics=sem, vmem_limit_bytes=_vmem_limit(*block_bytes))


def _nt_dot(a, b):
    return lax.dot_general(a, b, (((1,), (1,)), ((), ())), preferred_element_type=F32)


def _rms_mod(x, g, shift, scale):
    ms = jnp.mean(x * x, axis=-1, keepdims=True)
    y = x * lax.rsqrt(ms + EPS) * g
    return y * (1.0 + scale) + shift


def _mod_kernel(c_ref, w_ref, b_ref, o_ref):
    c = c_ref[...]
    s = c * jax.nn.sigmoid(c)
    s_hi = s.astype(BF16).astype(F32)
    top = lax.broadcasted_iota(jnp.int32, s.shape, 0) < 8
    lhs = jnp.where(top, s_hi, s - s_hi).astype(BF16)
    acc = jnp.dot(lhs, w_ref[...].astype(BF16), preferred_element_type=F32)
    o_ref[...] = acc[0:8] + acc[8:16] + b_ref[...]


def _modulation(cvec, w_mod, b_mod):
    d, n = w_mod.shape
    tn = 1024
    return pl.pallas_call(
        _mod_kernel,
        out_shape=jax.ShapeDtypeStruct((8, n), F32),
        grid=(n // tn,),
        in_specs=[pl.BlockSpec((16, d), lambda j: (0, 0)),
                  pl.BlockSpec((d, tn), lambda j: (0, j)),
                  pl.BlockSpec((1, tn), lambda j: (0, j))],
        out_specs=pl.BlockSpec((8, tn), lambda j: (0, j)),
        compiler_params=_params(("arbitrary",), d * tn * 4, d * tn * 2),
        name="modulation",
    )(cvec, w_mod, b_mod.reshape(1, n))


def _inproj_kernel(kind_ref, x_ref, g_ref, sh_ref, sc_ref, w_ref, gc_ref, cos_ref, sa_ref, sb_ref,
                   o_ref, h_ref, *, rope):
    j = pl.program_id(1)
    tn = o_ref.shape[1]
    nch = tn // LANES

    @pl.when(j == 0)
    def _():
        h_ref[...] = _rms_mod(x_ref[...], g_ref[...], sh_ref[0], sc_ref[0]).astype(BF16)

    acc = jnp.dot(h_ref[...], w_ref[...], preferred_element_type=F32)
    for c in range(nch):
        a = acc[:, c * LANES:(c + 1) * LANES]
        kind = kind_ref[j * nch + c]

        @pl.when(kind == 0)
        def _():
            o_ref[:, c * LANES:(c + 1) * LANES] = a.astype(o_ref.dtype)

        @pl.when(kind == 1)
        def _():
            ms = jnp.mean(a * a, axis=-1, keepdims=True)
            y = a * lax.rsqrt(ms + EPS) * gc_ref[:, c * LANES:(c + 1) * LANES]
            if rope:
                y = (y * cos_ref[...] + pltpu.roll(y, LANES - 32, 1) * sa_ref[...]
                     + pltpu.roll(y, 32, 1) * sb_ref[...])
            o_ref[:, c * LANES:(c + 1) * LANES] = y.astype(o_ref.dtype)


def _in_proj(x2d, norm_g, shift, scale, w_bf16, col_gain, col_kind, tables, *, tm, rows_per_mod, rope):
    rows, d = x2d.shape
    n = w_bf16.shape[1]
    tn = 512
    cos_t, sa_t, sb_t = tables
    pos_tiles = cos_t.shape[0] // tm

    grid_spec = pltpu.PrefetchScalarGridSpec(
        num_scalar_prefetch=1,
        grid=(rows // tm, n // tn),
        in_specs=[
            pl.BlockSpec((tm, d), lambda i, j, k: (i, 0)),
            pl.BlockSpec((1, d), lambda i, j, k: (0, 0)),
            pl.BlockSpec((1, 1, d), lambda i, j, k: ((i * tm) // rows_per_mod, 0, 0)),
            pl.BlockSpec((1, 1, d), lambda i, j, k: ((i * tm) // rows_per_mod, 0, 0)),
            pl.BlockSpec((d, tn), lambda i, j, k: (0, j)),
            pl.BlockSpec((1, tn), lambda i, j, k: (0, j)),
            pl.BlockSpec((tm, LANES), lambda i, j, k: (i % pos_tiles, 0)),
            pl.BlockSpec((tm, LANES), lambda i, j, k: (i % pos_tiles, 0)),
            pl.BlockSpec((tm, LANES), lambda i, j, k: (i % pos_tiles, 0)),
        ],
        out_specs=pl.BlockSpec((tm, tn), lambda i, j, k: (i, j)),
        scratch_shapes=[pltpu.VMEM((tm, d), BF16)],
    )
    return pl.pallas_call(
        functools.partial(_inproj_kernel, rope=rope),
        out_shape=jax.ShapeDtypeStruct((rows, n), BF16),
        grid_spec=grid_spec,
        compiler_params=_params(("arbitrary", "arbitrary"), tm * d * 4, d * tn * 2, tm * tn * 2,
                                3 * tm * LANES * 4, tm * d),
        name="in_proj_rope" if rope else "in_proj_ctx",
    )(col_kind, x2d, norm_g.reshape(1, d), shift, scale, w_bf16, col_gain.reshape(1, n), cos_t, sa_t, sb_t)


def _window_mask(q0, ws, tq, nk):
    qpos = q0 + lax.broadcasted_iota(jnp.int32, (tq, nk), 0)
    kpos = ws + lax.broadcasted_iota(jnp.int32, (tq, nk), 1)
    return jnp.abs(qpos - kpos) <= WINDOW


def _attn_a_kernel(sink_ref, q_ref, kl_ref, vl_ref, kc_ref, vc_ref, o_ref, vl1_ref, vc1_ref, *slots,
                   tq, seq_len, group):
    depth = len(slots) // 2
    p_slots, sink_slots = slots[:depth], slots[depth:]
    sink = sink_ref[pl.program_id(1)] * LOG2_E
    dh = vc_ref.shape[1]
    nk = tq + 2 * WINDOW if seq_len else 0

    @pl.when(pl.program_id(1) % group == 0)
    def _():
        if seq_len:
            vl1_ref[:, :dh] = vl_ref[...]
            vl1_ref[:, dh:] = jnp.ones_like(vl_ref)
        vc1_ref[:, :dh] = vc_ref[...]
        vc1_ref[:, dh:] = jnp.ones_like(vc_ref)

    def window_start(r0):
        return pl.multiple_of(jnp.clip(r0 - WINDOW, 0, seq_len - nk), WINDOW)

    def score_stage(b, slot):
        r0 = pl.multiple_of(b * tq, tq)
        q = q_ref[pl.ds(r0, tq), :]
        s_ctx = _nt_dot(q, kc_ref[...])
        m = jnp.maximum(jnp.max(s_ctx, axis=-1, keepdims=True), sink)
        if seq_len:
            ws = window_start(r0)
            s_loc = _nt_dot(q, kl_ref[pl.ds(ws, nk), :])
            s_loc = jnp.where(_window_mask(r0, ws, tq, nk), s_loc, NEG_INF)
            m = jnp.maximum(m, jnp.max(s_loc, axis=-1, keepdims=True))
            p_slots[slot][:, :nk] = jnp.exp2(s_loc - m).astype(BF16)
        p_slots[slot][:, nk:] = jnp.exp2(s_ctx - m).astype(BF16)
        sink_slots[slot][...] = jnp.broadcast_to(jnp.exp2(sink - m), sink_slots[slot].shape)

    def value_stage(b, slot):
        r0 = pl.multiple_of(b * tq, tq)
        acc = jnp.dot(p_slots[slot][:, nk:], vc1_ref[...], preferred_element_type=F32)
        if seq_len:
            acc = acc + jnp.dot(p_slots[slot][:, :nk], vl1_ref[pl.ds(window_start(r0), nk), :],
                                preferred_element_type=F32)
        den = acc[:, dh:dh + 1] + sink_slots[slot][:, 0:1]
        o_ref[pl.ds(r0, tq), :] = (acc[:, :dh] / den).astype(o_ref.dtype)

    _pipeline_blocks(q_ref.shape[0] // tq, depth, score_stage, value_stage)


def _attn_a(z, zc, sink, *, batch, seq_len, n_ctx, q_from_ctx, col_q, col_k, col_v):
    g = A_HEADS // A_KV_HEADS
    if q_from_ctx:
        lq, tq, qsrc, depth = n_ctx, n_ctx, zc, 1
    else:
        lq, tq, qsrc, depth = seq_len, 256, z, 8
    lat_len = 0 if q_from_ctx else seq_len
    lat_block = 16 if q_from_ctx else seq_len
    lat_tiles = seq_len // lat_block
    n_keys = n_ctx + (0 if q_from_ctx else tq + 2 * WINDOW)
    grid_spec = pltpu.PrefetchScalarGridSpec(
        num_scalar_prefetch=1,
        grid=(batch, A_HEADS),
        in_specs=[
            pl.BlockSpec((lq, LANES), lambda b, h, s: (b, col_q + h)),
            pl.BlockSpec((lat_block, LANES), lambda b, h, s: (b * lat_tiles, col_k + h // g)),
            pl.BlockSpec((lat_block, LANES), lambda b, h, s: (b * lat_tiles, col_v + h // g)),
            pl.BlockSpec((n_ctx, LANES), lambda b, h, s: (b, col_k + h // g)),
            pl.BlockSpec((n_ctx, LANES), lambda b, h, s: (b, col_v + h // g)),
        ],
        out_specs=pl.BlockSpec((lq, LANES), lambda b, h, s: (b, h)),
        scratch_shapes=[pltpu.VMEM((lat_block, 2 * LANES), BF16), pltpu.VMEM((n_ctx, 2 * LANES), BF16)]
        + [pltpu.VMEM((tq, n_keys), BF16)] * depth + [pltpu.VMEM((tq, LANES), F32)] * depth,
    )
    return pl.pallas_call(
        functools.partial(_attn_a_kernel, tq=tq, seq_len=lat_len, group=g),
        out_shape=jax.ShapeDtypeStruct((batch * lq, A_HEADS * HEAD_DIM), BF16),
        grid_spec=grid_spec,
        compiler_params=_params(("arbitrary",) * 2, 2 * lq * LANES * 2, 4 * lat_block * LANES * 2,
                                4 * n_ctx * LANES * 2, depth * tq * (n_keys + 2 * LANES)),
        name="attn_window_ctx" if q_from_ctx else "attn_window",
    )(sink, qsrc, z, z, zc, zc)


def _pipeline_blocks(n_blocks, depth, score_stage, value_stage):
    score_stage(0, 0)

    def body(u, carry):
        for i in range(depth):
            b = u * depth + i
            score_stage(jnp.minimum(b + 1, n_blocks - 1), (i + 1) % depth)
            value_stage(b, i)
        return carry

    lax.fori_loop(0, n_blocks // depth, body, 0)


def _join_rows(dst_ref, lat_ref, ctx_ref, has_lat):
    n_lat = lat_ref.shape[0] if has_lat else 0
    if has_lat:
        dst_ref[0:n_lat, 0:lat_ref.shape[1]] = lat_ref[...]
    dst_ref[n_lat:n_lat + ctx_ref.shape[0], 0:ctx_ref.shape[1]] = ctx_ref[...]


def _exp2_numerators(q, k_all):
    s = _nt_dot(q, k_all)
    return jnp.exp2(s - jnp.max(s, axis=-1, keepdims=True))


def _attn_b_kernel(q1_ref, q2_ref, k1l_ref, k2l_ref, vl_ref, k1c_ref, k2c_ref, vc_ref,
                   lq1_ref, lk1_ref, lq2_ref, lk2_ref, sub_ref, o_ref, k1_ref, k2_ref, v_ref, *p_slots,
                   tq, has_lat, lam_init):
    lam = (jnp.exp(jnp.sum(lq1_ref[0] * lk1_ref[0], axis=-1, keepdims=True))
           - jnp.exp(jnp.sum(lq2_ref[0] * lk2_ref[0], axis=-1, keepdims=True)) + lam_init)
    _join_rows(k1_ref, k1l_ref, k1c_ref, has_lat)
    _join_rows(k2_ref, k2l_ref, k2c_ref, has_lat)
    _join_rows(v_ref, vl_ref, vc_ref, has_lat)

    def score_stage(b, slot):
        r0 = pl.multiple_of(b * tq, tq)
        p1 = _exp2_numerators(q1_ref[pl.ds(r0, tq), :], k1_ref[...])
        p2 = _exp2_numerators(q2_ref[pl.ds(r0, tq), :], k2_ref[...])
        w1 = 1.0 / jnp.sum(p1, axis=-1, keepdims=True)
        w2 = lam / jnp.sum(p2, axis=-1, keepdims=True)
        p_slots[slot][...] = (p1 * w1 - p2 * w2).astype(BF16)

    def value_stage(b, slot):
        r0 = pl.multiple_of(b * tq, tq)
        o = jnp.dot(p_slots[slot][...], v_ref[...], preferred_element_type=F32)
        ms = jnp.mean(o * o, axis=-1, keepdims=True)
        o = o * lax.rsqrt(ms + EPS) * sub_ref[...] * (1.0 - lam_init)
        o_ref[pl.ds(r0, tq), :] = o.astype(o_ref.dtype)

    _pipeline_blocks(q1_ref.shape[0] // tq, len(p_slots), score_stage, value_stage)


def _attn_b(z, zc, lq1, lk1, lq2, lk2, subln, *, batch, seq_len, n_ctx, q_from_ctx, col_q, col_k, col_v,
            lam_init):
    dv = 2 * HEAD_DIM
    if q_from_ctx:
        lq, bq, tq, qsrc, depth = n_ctx, n_ctx, n_ctx, zc, 1
    else:
        lq, bq, tq, qsrc, depth = seq_len, seq_len, 256, z, 4
    nqb = lq // bq
    lat_block = 16 if q_from_ctx else seq_len
    lat_tiles = seq_len // lat_block
    n_keys = n_ctx if q_from_ctx else seq_len + n_ctx
    lam_spec = pl.BlockSpec((1, 1, HEAD_DIM), lambda b, h, i: (h, 0, 0))
    in_specs = [
        pl.BlockSpec((bq, LANES), lambda b, h, i: (b * nqb + i, col_q + 2 * h)),
        pl.BlockSpec((bq, LANES), lambda b, h, i: (b * nqb + i, col_q + 2 * h + 1)),
        pl.BlockSpec((lat_block, LANES), lambda b, h, i: (b * lat_tiles, col_k + 2 * h)),
        pl.BlockSpec((lat_block, LANES), lambda b, h, i: (b * lat_tiles, col_k + 2 * h + 1)),
        pl.BlockSpec((lat_block, dv), lambda b, h, i: (b * lat_tiles, col_v // 2 + h)),
        pl.BlockSpec((n_ctx, LANES), lambda b, h, i: (b, col_k + 2 * h)),
        pl.BlockSpec((n_ctx, LANES), lambda b, h, i: (b, col_k + 2 * h + 1)),
        pl.BlockSpec((n_ctx, dv), lambda b, h, i: (b, col_v // 2 + h)),
        lam_spec, lam_spec, lam_spec, lam_spec,
        pl.BlockSpec((1, dv), lambda b, h, i: (0, 0)),
    ]
    return pl.pallas_call(
        functools.partial(_attn_b_kernel, tq=tq, has_lat=not q_from_ctx, lam_init=lam_init),
        out_shape=jax.ShapeDtypeStruct((batch * lq, B_HEADS * dv), BF16),
        grid=(batch, B_HEADS, nqb),
        in_specs=in_specs,
        out_specs=pl.BlockSpec((bq, dv), lambda b, h, i: (b * nqb + i, h)),
        scratch_shapes=[pltpu.VMEM((n_keys, LANES), BF16), pltpu.VMEM((n_keys, LANES), BF16),
                        pltpu.VMEM((n_keys, dv), BF16)] + [pltpu.VMEM((tq, n_keys), BF16)] * depth,
        compiler_params=_params(("arbitrary",) * 3, 2 * bq * LANES * 2, 4 * lat_block * LANES * 2,
                                4 * n_ctx * LANES * 2, bq * dv * 2, (4 * LANES + depth * tq) * n_keys,
                                4 * tq * n_keys * 4 // 2),
        name="attn_diff_ctx" if q_from_ctx else "attn_diff",
    )(qsrc, qsrc, z, z, z, zc, zc, zc,
      lq1.reshape(B_HEADS, 1, HEAD_DIM), lk1.reshape(B_HEADS, 1, HEAD_DIM),
      lq2.reshape(B_HEADS, 1, HEAD_DIM), lk2.reshape(B_HEADS, 1, HEAD_DIM), subln.reshape(1, dv))


def _attn_c_kernel(q_ref, kl_ref, vl_ref, kc_ref, vc_ref, o_ref, k_ref, v1_ref, *p_slots, tq, group):
    dh = vl_ref.shape[1]

    @pl.when(pl.program_id(1) % group == 0)
    def _():
        _join_rows(k_ref, kl_ref, kc_ref, True)
        _join_rows(v1_ref, vl_ref, vc_ref, True)
        v1_ref[:, dh:] = jnp.ones((v1_ref.shape[0], v1_ref.shape[1] - dh), v1_ref.dtype)

    def score_stage(b, slot):
        r0 = pl.multiple_of(b * tq, tq)
        p_slots[slot][...] = _exp2_numerators(q_ref[pl.ds(r0, tq), :], k_ref[...]).astype(BF16)

    def value_stage(b, slot):
        r0 = pl.multiple_of(b * tq, tq)
        acc = jnp.dot(p_slots[slot][...], v1_ref[...], preferred_element_type=F32)
        o_ref[pl.ds(r0, tq), :] = (acc[:, :dh] / acc[:, dh:dh + 1]).astype(o_ref.dtype)

    _pipeline_blocks(q_ref.shape[0] // tq, len(p_slots), score_stage, value_stage)


def _attn_c(z, zc, *, batch, seq_len, n_ctx, n_heads, col_k, col_v):
    g = n_heads // C_KV_HEADS
    tq, depth = 256, 8
    n_keys = seq_len + n_ctx
    return pl.pallas_call(
        functools.partial(_attn_c_kernel, tq=tq, group=g),
        out_shape=jax.ShapeDtypeStruct((batch * seq_len, n_heads * HEAD_DIM), BF16),
        grid=(batch, n_heads),
        scratch_shapes=[pltpu.VMEM((n_keys, LANES), BF16), pltpu.VMEM((n_keys, 2 * LANES), BF16)]
        + [pltpu.VMEM((tq, n_keys), BF16)] * depth,
        in_specs=[
            pl.BlockSpec((seq_len, LANES), lambda b, h: (b, h)),
            pl.BlockSpec((seq_len, LANES), lambda b, h: (b, col_k + h // g)),
            pl.BlockSpec((seq_len, LANES), lambda b, h: (b, col_v + h // g)),
            pl.BlockSpec((n_ctx, LANES), lambda b, h: (b, col_k + h // g)),
            pl.BlockSpec((n_ctx, LANES), lambda b, h: (b, col_v + h // g)),
        ],
        out_specs=pl.BlockSpec((seq_len, LANES), lambda b, h: (b, h)),
        compiler_params=_params(("arbitrary",) * 2, 4 * seq_len * LANES * 2, 2 * n_ctx * LANES * 2,
                                (3 * LANES + depth * tq) * n_keys, 3 * tq * n_keys * 4 // 2),
        name="attn_gqa",
    )(z, z, z, zc, zc)


def _outproj_kernel(*refs, n_lhs):
    lhs = refs[:n_lhs]
    w_ref, res_ref, gate_ref, o_ref = refs[n_lhs:]
    acc = None
    k0 = 0
    for a_ref in lhs:
        kw = a_ref.shape[1]
        part = jnp.dot(a_ref[...], w_ref[k0:k0 + kw, :], preferred_element_type=F32)
        acc = part if acc is None else acc + part
        k0 += kw
    o_ref[...] = res_ref[...] + gate_ref[0] * acc


def _out_proj(lhs_list, w_bf16, res2d, gate, *, tm, rows_per_mod):
    rows, n = res2d.shape
    kdim = w_bf16.shape[0]
    tn = 512
    in_specs = [pl.BlockSpec((tm, a.shape[1]), lambda i, j: (i, 0)) for a in lhs_list]
    in_specs += [
        pl.BlockSpec((kdim, tn), lambda i, j: (0, j)),
        pl.BlockSpec((tm, tn), lambda i, j: (i, j)),
        pl.BlockSpec((1, 1, tn), lambda i, j: ((i * tm) // rows_per_mod, 0, j)),
    ]
    return pl.pallas_call(
        functools.partial(_outproj_kernel, n_lhs=len(lhs_list)),
        out_shape=jax.ShapeDtypeStruct((rows, n), F32),
        grid=(rows // tm, n // tn),
        in_specs=in_specs,
        out_specs=pl.BlockSpec((tm, tn), lambda i, j: (i, j)),
        compiler_params=_params(("arbitrary", "arbitrary"), tm * kdim * 2, kdim * tn * 2, 2 * tm * tn * 4),
        name="out_proj",
    )(*lhs_list, w_bf16, res2d, gate)


def _ffn_kernel(x_ref, g_ref, sh_ref, sc_ref, gate_ref, wg_ref, wu_ref, wd_ref, o_ref, h_ref, acc_ref):
    f = pl.program_id(1)

    @pl.when(f == 0)
    def _():
        h_ref[...] = _rms_mod(x_ref[...], g_ref[...], sh_ref[0], sc_ref[0]).astype(BF16)
        acc_ref[...] = jnp.zeros_like(acc_ref)

    h = h_ref[...]
    g_ = jnp.dot(h, wg_ref[...], preferred_element_type=F32)
    u_ = jnp.dot(h, wu_ref[...], preferred_element_type=F32)
    a = (g_ * jax.nn.sigmoid(g_) * u_).astype(BF16)
    acc_ref[...] += jnp.dot(a, wd_ref[...], preferred_element_type=F32)

    @pl.when(f == pl.num_programs(1) - 1)
    def _():
        o_ref[...] = x_ref[...] + gate_ref[0] * acc_ref[...]


def _dense_ffn(x2d, norm_g, shift, scale, gate, w_in_bf16, w_down_bf16, *, tm, rows_per_mod):
    rows, d = x2d.shape
    d_ff = w_down_bf16.shape[0]
    tf = 512
    nf = d_ff // tf
    mod_spec = pl.BlockSpec((1, 1, d), lambda i, f: ((i * tm) // rows_per_mod, 0, 0))
    return pl.pallas_call(
        _ffn_kernel,
        out_shape=jax.ShapeDtypeStruct((rows, d), F32),
        grid=(rows // tm, nf),
        in_specs=[
            pl.BlockSpec((tm, d), lambda i, f: (i, 0)),
            pl.BlockSpec((1, d), lambda i, f: (0, 0)),
            mod_spec, mod_spec, mod_spec,
            pl.BlockSpec((d, tf), lambda i, f: (0, f)),
            pl.BlockSpec((d, tf), lambda i, f: (0, nf + f)),
            pl.BlockSpec((tf, d), lambda i, f: (f, 0)),
        ],
        out_specs=pl.BlockSpec((tm, d), lambda i, f: (i, 0)),
        scratch_shapes=[pltpu.VMEM((tm, d), BF16), pltpu.VMEM((tm, d), F32)],
        compiler_params=_params(("arbitrary", "arbitrary"), 2 * tm * d * 4, 3 * d * tf * 2, tm * d * 3),
        name="dense_swiglu",
    )(x2d, norm_g.reshape(1, d), shift, scale, gate, w_in_bf16, w_in_bf16, w_down_bf16)


def _split3(v):
    hi = v.astype(BF16)
    r = v - hi.astype(F32)
    mid = r.astype(BF16)
    lo = (r - mid.astype(F32)).astype(BF16)
    return hi, mid, lo


def _router_kernel(x_ref, g_ref, sh_ref, sc_ref, wr_ref, h_ref, ids_ref, gates_ref, cnt_ref, carry_ref):
    i = pl.program_id(0)
    tm = x_ref.shape[0]

    @pl.when(i == 0)
    def _():
        carry_ref[...] = jnp.zeros_like(carry_ref)

    h = _rms_mod(x_ref[...], g_ref[...], sh_ref[0], sc_ref[0])
    h_ref[...] = h
    h0, h1, h2 = _split3(h)
    w0, w1, w2 = _split3(wr_ref[...])
    dot = functools.partial(jnp.dot, preferred_element_type=F32)
    logits = (dot(h0, w0) + (dot(h0, w1) + dot(h1, w0))
              + (dot(h0, w2) + dot(h1, w1) + dot(h2, w0)))
    lane = lax.broadcasted_iota(jnp.int32, (tm, LANES), 1)
    logits = jnp.where(lane < N_EXPERTS, logits, -jnp.inf)
    v0 = jnp.max(logits, axis=-1, keepdims=True)
    i0 = jnp.min(jnp.where(logits == v0, lane, LANES), axis=-1, keepdims=True)
    rest = jnp.where(lane == i0, -jnp.inf, logits)
    v1 = jnp.max(rest, axis=-1, keepdims=True)
    i1 = jnp.min(jnp.where(rest == v1, lane, LANES), axis=-1, keepdims=True)
    e1 = jnp.exp(v1 - v0)
    g0 = 1.0 / (1.0 + e1)
    g1 = e1 / (1.0 + e1)

    sel = (lane == i0) | (lane == i1)
    row = lax.broadcasted_iota(jnp.int32, (tm, tm), 0)
    col = lax.broadcasted_iota(jnp.int32, (tm, tm), 1)
    tri = (col < row).astype(BF16)
    before = jnp.dot(tri, sel.astype(BF16), preferred_element_type=F32) + carry_ref[...]
    r0 = jnp.sum(jnp.where(lane == i0, before, 0.0), axis=-1, keepdims=True).astype(jnp.int32)
    r1 = jnp.sum(jnp.where(lane == i1, before, 0.0), axis=-1, keepdims=True).astype(jnp.int32)
    total = carry_ref[...] + jnp.sum(sel.astype(F32), axis=0, keepdims=True)
    carry_ref[...] = total

    ids_ref[...] = jnp.where(lane == 0, i0, jnp.where(lane == 1, i1, jnp.where(lane == 2, r0, r1)))
    gates_ref[...] = jnp.where(lane == 0, g0, g1)
    cnt_ref[...] = jnp.broadcast_to(total, cnt_ref.shape).astype(jnp.int32)


def _router(x2d, norm_g, shift, scale, w_router, *, tm, rows_per_mod):
    rows, d = x2d.shape
    wr = jnp.zeros((d, LANES), F32).at[:, :N_EXPERTS].set(w_router)
    mod_spec = pl.BlockSpec((1, 1, d), lambda i: ((i * tm) // rows_per_mod, 0, 0))
    return pl.pallas_call(
        _router_kernel,
        out_shape=(jax.ShapeDtypeStruct((rows, d), F32),
                   jax.ShapeDtypeStruct((rows, LANES), jnp.int32),
                   jax.ShapeDtypeStruct((rows, LANES), F32),
                   jax.ShapeDtypeStruct((8, LANES), jnp.int32)),
        grid=(rows // tm,),
        in_specs=[pl.BlockSpec((tm, d), lambda i: (i, 0)),
                  pl.BlockSpec((1, d), lambda i: (0, 0)),
                  mod_spec, mod_spec,
                  pl.BlockSpec((d, LANES), lambda i: (0, 0))],
        out_specs=(pl.BlockSpec((tm, d), lambda i: (i, 0)),
                   pl.BlockSpec((tm, LANES), lambda i: (i, 0)),
                   pl.BlockSpec((tm, LANES), lambda i: (i, 0)),
                   pl.BlockSpec((8, LANES), lambda i: (0, 0))),
        scratch_shapes=[pltpu.VMEM((1, LANES), F32)],
        compiler_params=_params(("arbitrary",), 2 * tm * d * 4, d * LANES * 4, 2 * tm * LANES * 4),
        name="router",
    )(x2d, norm_g.reshape(1, d), shift, scale, wr)


GATHER_UNROLL = 8


def _start_row_gather(src_hbm, idx_ref, base, dst_ref, sem):
    def body(r8, c):
        for u in range(GATHER_UNROLL):
            r = r8 * GATHER_UNROLL + u
            pltpu.make_async_copy(src_hbm.at[pl.ds(idx_ref[base + r], 1), :],
                                  dst_ref.at[pl.ds(r, 1), :], sem).start()
        return c

    lax.fori_loop(0, dst_ref.shape[0] // GATHER_UNROLL, body, 0)


def _wait_row_gather(src_hbm, dst_ref, sem):
    pltpu.make_async_copy(src_hbm.at[pl.ds(0, dst_ref.shape[0]), :], dst_ref, sem).wait()


def _dispatch_kernel(idx_ref, src_hbm, o_ref, buf_ref, sem):
    tr = o_ref.shape[0]
    i, n = pl.program_id(0), pl.num_programs(0)
    slot = i & 1

    @pl.when(i == 0)
    def _():
        _start_row_gather(src_hbm, idx_ref, 0, buf_ref.at[0], sem.at[0])

    @pl.when(i + 1 < n)
    def _():
        _start_row_gather(src_hbm, idx_ref, (i + 1) * tr, buf_ref.at[1 - slot], sem.at[1 - slot])

    _wait_row_gather(src_hbm, buf_ref.at[slot], sem.at[slot])
    o_ref[...] = buf_ref[slot].astype(o_ref.dtype)


def _dispatch(h2d, row_tok, *, tr):
    n_rows = row_tok.shape[0]
    d = h2d.shape[1]
    grid_spec = pltpu.PrefetchScalarGridSpec(
        num_scalar_prefetch=1,
        grid=(n_rows // tr,),
        in_specs=[pl.BlockSpec(memory_space=pl.ANY)],
        out_specs=pl.BlockSpec((tr, d), lambda i, idx: (i, 0)),
        scratch_shapes=[pltpu.VMEM((2, tr, d), F32), pltpu.SemaphoreType.DMA((2,))],
    )
    return pl.pallas_call(
        _dispatch_kernel,
        out_shape=jax.ShapeDtypeStruct((n_rows, d), BF16),
        grid_spec=grid_spec,
        compiler_params=_params(("arbitrary",), tr * d * 2, tr * d * 4),
        name="expert_dispatch",
    )(row_tok, h2d)


def _combine_kernel(d0_ref, d1_ref, ys_hbm, x_ref, gate_ref, rg_ref, o_ref, a_ref, b_ref, sem):
    tt = x_ref.shape[0]
    i, n = pl.program_id(0), pl.num_programs(0)
    slot = i & 1

    def start(tile, s):
        _start_row_gather(ys_hbm, d0_ref, tile * tt, a_ref.at[s], sem.at[s, 0])
        _start_row_gather(ys_hbm, d1_ref, tile * tt, b_ref.at[s], sem.at[s, 1])

    @pl.when(i == 0)
    def _():
        start(0, 0)

    @pl.when(i + 1 < n)
    def _():
        start(i + 1, 1 - slot)

    _wait_row_gather(ys_hbm, a_ref.at[slot], sem.at[slot, 0])
    _wait_row_gather(ys_hbm, b_ref.at[slot], sem.at[slot, 1])
    rg = rg_ref[...]
    mix = rg[:, 0:1] * a_ref[slot] + rg[:, 1:2] * b_ref[slot]
    o_ref[...] = x_ref[...] + gate_ref[0] * mix


def _combine(ys, dest0, dest1, x2d, gate, row_gates, *, tt, rows_per_mod):
    rows, d = x2d.shape
    grid_spec = pltpu.PrefetchScalarGridSpec(
        num_scalar_prefetch=2,
        grid=(rows // tt,),
        in_specs=[pl.BlockSpec(memory_space=pl.ANY),
                  pl.BlockSpec((tt, d), lambda i, a, b: (i, 0)),
                  pl.BlockSpec((1, 1, d), lambda i, a, b: ((i * tt) // rows_per_mod, 0, 0)),
                  pl.BlockSpec((tt, LANES), lambda i, a, b: (i, 0))],
        out_specs=pl.BlockSpec((tt, d), lambda i, a, b: (i, 0)),
        scratch_shapes=[pltpu.VMEM((2, tt, d), F32), pltpu.VMEM((2, tt, d), F32),
                        pltpu.SemaphoreType.DMA((2, 2))],
    )
    return pl.pallas_call(
        _combine_kernel,
        out_shape=jax.ShapeDtypeStruct((rows, d), F32),
        grid_spec=grid_spec,
        compiler_params=_params(("arbitrary",), 2 * tt * d * 4, 2 * tt * d * 4),
        name="expert_combine",
    )(dest0, dest1, ys, x2d, gate, row_gates)


WEIGHT_DMA_PRIORITY = 1


def _stream_group_weights(sched, copies, cast):
    te_ref, first_ref, grp_ref, ngrp_ref, nxt_ref = sched
    j, r = pl.program_id(0), pl.program_id(1)
    nj = pl.num_programs(0)

    @pl.when(first_ref[r] == 1)
    def _():
        g = grp_ref[r]
        ng = ngrp_ref[0]

        @pl.when((j == 0) & (g == 0))
        def _():
            for cp in copies(te_ref[r], j):
                cp.start(priority=WEIGHT_DMA_PRIORITY)

        for cp in copies(te_ref[r], j):
            cp.wait()
        cast()
        last = g == ng - 1

        @pl.when(jnp.logical_not(last & (j == nj - 1)))
        def _():
            for cp in copies(nxt_ref[r], jnp.where(last, j + 1, j)):
                cp.start(priority=WEIGHT_DMA_PRIORITY)


CAST_ROWS = 256


def _cast_rows(src_ref, dst_ref):
    def body(i, carry):
        rows = pl.ds(pl.multiple_of(i * CAST_ROWS, CAST_ROWS), CAST_ROWS)
        dst_ref[rows, :] = src_ref[rows, :].astype(dst_ref.dtype)
        return carry

    lax.fori_loop(0, src_ref.shape[0] // CAST_ROWS, body, 0)


def _gmm_up_kernel(te_ref, first_ref, grp_ref, ngrp_ref, nxt_ref, nv_ref, x_ref, w_hbm, o_ref,
                   stage_ref, wbf_ref, sem):
    tn = o_ref.shape[1]
    nj = pl.num_programs(0)

    def copies(e, j):
        return [pltpu.make_async_copy(w_hbm.at[e, :, pl.ds(pl.multiple_of((half * nj + j) * tn, tn), tn)],
                                      stage_ref.at[half], sem.at[half]) for half in range(2)]

    def cast():
        _cast_rows(stage_ref.at[0], wbf_ref.at[0])
        _cast_rows(stage_ref.at[1], wbf_ref.at[1])

    _stream_group_weights((te_ref, first_ref, grp_ref, ngrp_ref, nxt_ref), copies, cast)
    valid = pl.program_id(1) < nv_ref[0]

    @pl.when(valid)
    def _():
        x = x_ref[...]
        g_ = jnp.dot(x, wbf_ref[0], preferred_element_type=F32)
        u_ = jnp.dot(x, wbf_ref[1], preferred_element_type=F32)
        o_ref[...] = (g_ * jax.nn.sigmoid(g_) * u_).astype(o_ref.dtype)

    @pl.when(jnp.logical_not(valid))
    def _():
        o_ref[...] = jnp.zeros_like(o_ref)


def _gmm_down_kernel(te_ref, first_ref, grp_ref, ngrp_ref, nxt_ref, nv_ref, h_ref, w_hbm, o_ref,
                     stage_ref, wbf_ref, sem):
    tn = o_ref.shape[1]

    def copies(e, j):
        return [pltpu.make_async_copy(w_hbm.at[e, :, pl.ds(pl.multiple_of(j * tn, tn), tn)], stage_ref, sem)]

    def cast():
        _cast_rows(stage_ref, wbf_ref)

    _stream_group_weights((te_ref, first_ref, grp_ref, ngrp_ref, nxt_ref), copies, cast)
    valid = pl.program_id(1) < nv_ref[0]

    @pl.when(valid)
    def _():
        o_ref[...] = jnp.dot(h_ref[...], wbf_ref[...], preferred_element_type=F32)

    @pl.when(jnp.logical_not(valid))
    def _():
        o_ref[...] = jnp.zeros_like(o_ref)


def _expert_ffn(xs, w_in, w_down, sched, n_valid, *, tr):
    n_rows, d = xs.shape
    d_ff = w_down.shape[1]
    n_tiles = n_rows // tr
    n_sched = len(sched) + 1
    tn1 = d_ff // 4
    nj1 = d_ff // tn1
    small_temps = 4 * 1024 * 1024

    def row_map(*args):
        r, nv = args[1], args[-1]
        return (jnp.minimum(r, nv[0] - 1), 0)

    def out_map(*args):
        return (args[1], args[0])

    up_spec = pltpu.PrefetchScalarGridSpec(
        num_scalar_prefetch=n_sched,
        grid=(nj1, n_tiles),
        in_specs=[pl.BlockSpec((tr, d), row_map), pl.BlockSpec(memory_space=pl.ANY)],
        out_specs=pl.BlockSpec((tr, tn1), out_map),
        scratch_shapes=[pltpu.VMEM((2, d, tn1), F32), pltpu.VMEM((2, d, tn1), BF16),
                        pltpu.SemaphoreType.DMA((2,))],
    )
    stage1 = 2 * d * tn1 * 4
    hidden = pl.pallas_call(
        _gmm_up_kernel,
        out_shape=jax.ShapeDtypeStruct((n_rows, d_ff), BF16),
        grid_spec=up_spec,
        compiler_params=pltpu.CompilerParams(
            dimension_semantics=("arbitrary", "arbitrary"),
            vmem_limit_bytes=stage1 + 2 * d * tn1 * 2 + 2 * (tr * d * 2 + tr * tn1 * 2) + 2 * small_temps),
        name="expert_up",
    )(*sched, n_valid, xs, w_in)

    tn2 = d // 2
    down_spec = pltpu.PrefetchScalarGridSpec(
        num_scalar_prefetch=n_sched,
        grid=(d // tn2, n_tiles),
        in_specs=[pl.BlockSpec((tr, d_ff), row_map), pl.BlockSpec(memory_space=pl.ANY)],
        out_specs=pl.BlockSpec((tr, tn2), out_map),
        scratch_shapes=[pltpu.VMEM((d_ff, tn2), F32), pltpu.VMEM((d_ff, tn2), BF16),
                        pltpu.SemaphoreType.DMA(())],
    )
    stage2 = d_ff * tn2 * 4
    return pl.pallas_call(
        _gmm_down_kernel,
        out_shape=jax.ShapeDtypeStruct((n_rows, d), F32),
        grid_spec=down_spec,
        compiler_params=pltpu.CompilerParams(
            dimension_semantics=("arbitrary", "arbitrary"),
            vmem_limit_bytes=stage2 + d_ff * tn2 * 2 + 2 * (tr * d_ff * 2 + tr * tn2 * 4) + small_temps),
        name="expert_down",
    )(*sched, n_valid, hidden, w_down)


def _moe(x2d, norm_g, shift, scale, gate, w_router, w_in, w_down, *, rows_per_mod):
    rows, d = x2d.shape
    tr = 256
    h2d, ids, gates, counts = _router(x2d, norm_g, shift, scale, w_router, tm=512, rows_per_mod=rows_per_mod)
    counts = counts[0, :N_EXPERTS]
    padded = (counts + tr - 1) // tr * tr
    pad_end = jnp.cumsum(padded)
    pad_start = pad_end - padded
    dest0 = pad_start[ids[:, 0]] + ids[:, 2]
    dest1 = pad_start[ids[:, 1]] + ids[:, 3]
    n_rows = 2 * rows + N_EXPERTS * tr
    n_tiles = n_rows // tr
    tok = jnp.arange(rows, dtype=jnp.int32)
    row_tok = jnp.zeros((n_rows,), jnp.int32).at[jnp.concatenate([dest0, dest1])].set(
        jnp.concatenate([tok, tok]))
    tile_expert = jnp.minimum(
        jnp.sum(jnp.arange(n_tiles, dtype=jnp.int32)[:, None] * tr >= pad_end[None, :], axis=1),
        N_EXPERTS - 1).astype(jnp.int32)
    n_valid = (pad_end[-1:] // tr).astype(jnp.int32)
    tile_id = jnp.arange(n_tiles, dtype=jnp.int32)
    prev_expert = jnp.concatenate([jnp.full((1,), -1, jnp.int32), tile_expert[:-1]])
    first = ((tile_id < n_valid[0]) & (tile_expert != prev_expert)).astype(jnp.int32)
    grp = jnp.cumsum(first).astype(jnp.int32) - 1
    n_grp = jnp.sum(first, keepdims=True).astype(jnp.int32)
    expert_id = jnp.arange(N_EXPERTS, dtype=jnp.int32)
    grp_expert = jnp.sort(jnp.where(counts > 0, expert_id, N_EXPERTS))
    nxt = grp_expert[(grp + 1) % n_grp[0]].astype(jnp.int32)
    sched = (tile_expert, first, grp, n_grp, nxt)

    xs = _dispatch(h2d, row_tok, tr=tr)
    ys = _expert_ffn(xs, w_in, w_down, sched, n_valid, tr=tr)
    return _combine(ys, dest0, dest1, x2d, gate, gates, tt=256, rows_per_mod=rows_per_mod)


def _rope_tables(seq_len):
    t = np.arange(seq_len)
    row = (t // GRID_W).astype(np.float64)[:, None]
    col = (t % GRID_W).astype(np.float64)[:, None]
    inv = ROPE_THETA ** (-np.arange(0, ROT_AXIS, 2, dtype=np.float64) / ROT_AXIS)
    ar, ac = row * inv, col * inv
    zero = np.zeros_like(ar)
    cos = np.concatenate([np.cos(ar), np.cos(ar), np.cos(ac), np.cos(ac)], axis=-1)
    s_next = np.concatenate([-np.sin(ar), zero, -np.sin(ac), zero], axis=-1)
    s_prev = np.concatenate([zero, np.sin(ar), zero, np.sin(ac)], axis=-1)
    return tuple(jnp.asarray(a.astype(np.float32)) for a in (cos, s_next, s_prev))


def _mod_vectors(c, c_ctx, w_mod, b_mod):
    batch, d = c.shape
    cvec = jnp.zeros((8, d), F32).at[:batch].set(c).at[batch].set(c_ctx)
    m = _modulation(jnp.concatenate([cvec, cvec], axis=0), w_mod, b_mod)
    lat = m[:batch].reshape(batch, N_MOD, 1, d)
    ctx = m[batch].reshape(N_MOD, 1, 1, d)
    return [lat[:, k] for k in range(N_MOD)], [ctx[k] for k in range(N_MOD)]


def kernel(x, c, ctx, c_ctx, e_norm1, e_norm2, e_w_mod, e_b_mod, e_w_in, e_w_out, e_a_qnorm, e_a_knorm,
           e_a_sink, e_b_qnorm, e_b_knorm, e_b_lam_q1, e_b_lam_k1, e_b_lam_q2, e_b_lam_k2, e_b_subln,
           e_ffn_w_in, e_ffn_w_down, o_norm1, o_norm2, o_w_mod, o_b_mod, o_w_in, o_w_out, o_c_qnorm,
           o_c_knorm, o_router, o_exp_w_in, o_exp_w_down):
    batch, seq_len, d = x.shape
    n_ctx = ctx.shape[1]
    qk_scale = HEAD_DIM ** -0.5 * LOG2_E
    tables = _rope_tables(seq_len)
    ctx_tables = tuple(t[:batch * n_ctx] for t in tables)
    x2d = x.reshape(batch * seq_len, d)
    xc2d = ctx.reshape(batch * n_ctx, d)
    tm = 1024

    (sh1, sc1, g1, sh2, sc2, g2), (csh1, csc1, cg1, csh2, csc2, cg2) = _mod_vectors(
        c, c_ctx, e_w_mod[0], e_b_mod[0])
    a_q, a_kv = A_HEADS * HEAD_DIM, A_KV_HEADS * HEAD_DIM
    b_qk, b_v = B_HEADS * 2 * HEAD_DIM, B_HEADS * 2 * HEAD_DIM
    ones = lambda n: jnp.ones((n,), F32)
    tile = lambda v, n: jnp.tile(v, n // HEAD_DIM)
    col_gain = jnp.concatenate([tile(e_a_qnorm[0] * qk_scale, a_q), tile(e_a_knorm[0], a_kv), ones(a_kv),
                                tile(e_b_qnorm[0] * qk_scale, b_qk), tile(e_b_knorm[0], b_qk), ones(b_v)])
    kinds = lambda *pairs: jnp.concatenate([jnp.full((n // HEAD_DIM,), k, jnp.int32) for k, n in pairs])
    col_kind = kinds((1, a_q), (1, a_kv), (0, a_kv), (1, b_qk), (1, b_qk), (0, b_v))
    w_in0 = e_w_in[0].astype(BF16)
    z = _in_proj(x2d, e_norm1[0], sh1, sc1, w_in0, col_gain, col_kind, tables,
                 tm=tm, rows_per_mod=seq_len, rope=True)
    zc = _in_proj(xc2d, e_norm1[0], csh1, csc1, w_in0, col_gain, col_kind, ctx_tables,
                  tm=batch * n_ctx, rows_per_mod=batch * n_ctx, rope=False)

    ca_q, ca_k, ca_v = 0, a_q // LANES, (a_q + a_kv) // LANES
    cb_q = (a_q + 2 * a_kv) // LANES
    cb_k, cb_v = cb_q + b_qk // LANES, cb_q + 2 * b_qk // LANES
    lam_init = 0.8 - 0.6 * math.exp(-0.3 * 0)
    dims = dict(batch=batch, seq_len=seq_len, n_ctx=n_ctx)
    attn_a = functools.partial(_attn_a, z, zc, e_a_sink[0], col_q=ca_q, col_k=ca_k, col_v=ca_v, **dims)
    attn_b = functools.partial(_attn_b, z, zc, e_b_lam_q1[0], e_b_lam_k1[0], e_b_lam_q2[0], e_b_lam_k2[0],
                               e_b_subln[0], col_q=cb_q, col_k=cb_k, col_v=cb_v, lam_init=lam_init, **dims)
    w_out0 = e_w_out[0].astype(BF16)
    x2d = _out_proj([attn_a(q_from_ctx=False), attn_b(q_from_ctx=False)], w_out0, x2d, g1,
                    tm=tm, rows_per_mod=seq_len)
    xc2d = _out_proj([attn_a(q_from_ctx=True), attn_b(q_from_ctx=True)], w_out0, xc2d, cg1,
                     tm=batch * n_ctx, rows_per_mod=batch * n_ctx)
    ffn_in, ffn_down = e_ffn_w_in[0].astype(BF16), e_ffn_w_down[0].astype(BF16)
    x2d = _dense_ffn(x2d, e_norm2[0], sh2, sc2, g2, ffn_in, ffn_down, tm=512, rows_per_mod=seq_len)
    xc2d = _dense_ffn(xc2d, e_norm2[0], csh2, csc2, cg2, ffn_in, ffn_down, tm=batch * n_ctx,
                      rows_per_mod=batch * n_ctx)

    (sh1, sc1, g1, sh2, sc2, g2), (csh1, csc1, _, _, _, _) = _mod_vectors(c, c_ctx, o_w_mod[0], o_b_mod[0])
    c_q = d
    c_kv = C_KV_HEADS * HEAD_DIM
    col_gain = jnp.concatenate([tile(o_c_qnorm[0] * qk_scale, c_q), tile(o_c_knorm[0], c_kv), ones(c_kv)])
    col_kind = kinds((1, c_q), (1, c_kv), (0, c_kv))
    w_in1 = o_w_in[0].astype(BF16)
    z = _in_proj(x2d, o_norm1[0], sh1, sc1, w_in1, col_gain, col_kind, tables,
                 tm=tm, rows_per_mod=seq_len, rope=True)
    zc = _in_proj(xc2d, o_norm1[0], csh1, csc1, w_in1, col_gain, col_kind, ctx_tables,
                  tm=batch * n_ctx, rows_per_mod=batch * n_ctx, rope=False)
    o = _attn_c(z, zc, n_heads=c_q // HEAD_DIM, col_k=c_q // LANES, col_v=(c_q + c_kv) // LANES, **dims)
    x2d = _out_proj([o], o_w_out[0].astype(BF16), x2d, g1, tm=tm, rows_per_mod=seq_len)
    x2d = _moe(x2d, o_norm2[0], sh2, sc2, g2, o_router[0], o_exp_w_in.reshape(o_exp_w_in.shape[1:]),
               o_exp_w_down.reshape(o_exp_w_down.shape[1:]), rows_per_mod=seq_len)
    return x2d.reshape(batch, seq_len, d)
```

```python
import functools
import math

import jax
import jax.numpy as jnp
import numpy as np
from jax import lax
from jax.experimental import pallas as pl
from jax.experimental.pallas import tpu as pltpu

F32 = jnp.float32
BF16 = jnp.bfloat16

HEAD_DIM = 128
GRID_W = 64
ROT_AXIS = HEAD_DIM // 2
ROPE_THETA = 10000.0
EPS = 1e-6
NEG_INF = -1e30
LOG2_E = math.log2(math.e)
N_MOD = 6
WINDOW = 128
A_HEADS, A_KV_HEADS = 8, 2
B_HEADS = 4
C_KV_HEADS = 4
N_EXPERTS = 8
LANES = 128
V7X_VMEM_BYTES = 64 * 1024 * 1024
VMEM_TEMP_ALLOWANCE = 16 * 1024 * 1024


def _vmem_limit(*block_bytes):
    need = 2 * sum(block_bytes) + VMEM_TEMP_ALLOWANCE
    return int(min(need, V7X_VMEM_BYTES - 6 * 1024 * 1024))


def _params(sem, *block_bytes):
    return pltpu.CompilerParams(dimension_semantics=sem, vmem_limit_bytes=_vmem_limit(*block_bytes))


def _nt_dot(a, b):
    return lax.dot_general(a, b, (((1,), (1,)), ((), ())), preferred_element_type=F32)


def _rms_mod(x, g, shift, scale):
    ms = jnp.mean(x * x, axis=-1, keepdims=True)
    y = x * lax.rsqrt(ms + EPS) * g
    return y * (1.0 + scale) + shift


def _mod_kernel(c_ref, w_ref, b_ref, o_ref):
    c = c_ref[...]
    s = c * jax.nn.sigmoid(c)
    s_hi = s.astype(BF16).astype(F32)
    top = lax.broadcasted_iota(jnp.int32, s.shape, 0) < 8
    lhs = jnp.where(top, s_hi, s - s_hi).astype(BF16)
    acc = jnp.dot(lhs, w_ref[...].astype(BF16), preferred_element_type=F32)
    o_ref[...] = acc[0:8] + acc[8:16] + b_ref[...]


def _modulation(cvec, w_mod, b_mod):
    d, n = w_mod.shape
    tn = 1024
    return pl.pallas_call(
        _mod_kernel,
        out_shape=jax.ShapeDtypeStruct((8, n), F32),
        grid=(n // tn,),
        in_specs=[pl.BlockSpec((16, d), lambda j: (0, 0)),
                  pl.BlockSpec((d, tn), lambda j: (0, j)),
                  pl.BlockSpec((1, tn), lambda j: (0, j))],
        out_specs=pl.BlockSpec((8, tn), lambda j: (0, j)),
        compiler_params=_params(("arbitrary",), d * tn * 4, d * tn * 2),
        name="modulation",
    )(cvec, w_mod, b_mod.reshape(1, n))


def _inproj_kernel(kind_ref, x_ref, g_ref, sh_ref, sc_ref, w_ref, gc_ref, cos_ref, sa_ref, sb_ref,
                   o_ref, h_ref, *, rope):
    j = pl.program_id(1)
    tn = o_ref.shape[1]
    nch = tn // LANES

    @pl.when(j == 0)
    def _():
        h_ref[...] = _rms_mod(x_ref[...], g_ref[...], sh_ref[0], sc_ref[0]).astype(BF16)

    acc = jnp.dot(h_ref[...], w_ref[...], preferred_element_type=F32)
    for c in range(nch):
        a = acc[:, c * LANES:(c + 1) * LANES]
        kind = kind_ref[j * nch + c]

        @pl.when(kind == 0)
        def _():
            o_ref[:, c * LANES:(c + 1) * LANES] = a.astype(o_ref.dtype)

        @pl.when(kind == 1)
        def _():
            ms = jnp.mean(a * a, axis=-1, keepdims=True)
            y = a * lax.rsqrt(ms + EPS) * gc_ref[:, c * LANES:(c + 1) * LANES]
            if rope:
                y = (y * cos_ref[...] + pltpu.roll(y, LANES - 32, 1) * sa_ref[...]
                     + pltpu.roll(y, 32, 1) * sb_ref[...])
            o_ref[:, c * LANES:(c + 1) * LANES] = y.astype(o_ref.dtype)


def _in_proj(x2d, norm_g, shift, scale, w_bf16, col_gain, col_kind, tables, *, tm, rows_per_mod, rope):
    rows, d = x2d.shape
    n = w_bf16.shape[1]
    tn = 512
    cos_t, sa_t, sb_t = tables
    pos_tiles = cos_t.shape[0] // tm

    grid_spec = pltpu.PrefetchScalarGridSpec(
        num_scalar_prefetch=1,
        grid=(rows // tm, n // tn),
        in_specs=[
            pl.BlockSpec((tm, d), lambda i, j, k: (i, 0)),
            pl.BlockSpec((1, d), lambda i, j, k: (0, 0)),
            pl.BlockSpec((1, 1, d), lambda i, j, k: ((i * tm) // rows_per_mod, 0, 0)),
            pl.BlockSpec((1, 1, d), lambda i, j, k: ((i * tm) // rows_per_mod, 0, 0)),
            pl.BlockSpec((d, tn), lambda i, j, k: (0, j)),
            pl.BlockSpec((1, tn), lambda i, j, k: (0, j)),
            pl.BlockSpec((tm, LANES), lambda i, j, k: (i % pos_tiles, 0)),
            pl.BlockSpec((tm, LANES), lambda i, j, k: (i % pos_tiles, 0)),
            pl.BlockSpec((tm, LANES), lambda i, j, k: (i % pos_tiles, 0)),
        ],
        out_specs=pl.BlockSpec((tm, tn), lambda i, j, k: (i, j)),
        scratch_shapes=[pltpu.VMEM((tm, d), BF16)],
    )
    return pl.pallas_call(
        functools.partial(_inproj_kernel, rope=rope),
        out_shape=jax.ShapeDtypeStruct((rows, n), BF16),
        grid_spec=grid_spec,
        compiler_params=_params(("arbitrary", "arbitrary"), tm * d * 4, d * tn * 2, tm * tn * 2,
                                3 * tm * LANES * 4, tm * d),
        name="in_proj_rope" if rope else "in_proj_ctx",
    )(col_kind, x2d, norm_g.reshape(1, d), shift, scale, w_bf16, col_gain.reshape(1, n), cos_t, sa_t, sb_t)


def _window_mask(q0, ws, tq, nk):
    qpos = q0 + lax.broadcasted_iota(jnp.int32, (tq, nk), 0)
    kpos = ws + lax.broadcasted_iota(jnp.int32, (tq, nk), 1)
    return jnp.abs(qpos - kpos) <= WINDOW


def _attn_a_kernel(sink_ref, q_ref, kl_ref, vl_ref, kc_ref, vc_ref, o_ref, vl1_ref, vc1_ref, *slots,
                   tq, seq_len, group):
    depth = len(slots) // 2
    p_slots, sink_slots = slots[:depth], slots[depth:]
    sink = sink_ref[pl.program_id(1)] * LOG2_E
    dh = vc_ref.shape[1]
    nk = tq + 2 * WINDOW if seq_len else 0

    @pl.when(pl.program_id(1) % group == 0)
    def _():
        if seq_len:
            vl1_ref[:, :dh] = vl_ref[...]
            vl1_ref[:, dh:] = jnp.ones_like(vl_ref)
        vc1_ref[:, :dh] = vc_ref[...]
        vc1_ref[:, dh:] = jnp.ones_like(vc_ref)

    def window_start(r0):
        return pl.multiple_of(jnp.clip(r0 - WINDOW, 0, seq_len - nk), WINDOW)

    def score_stage(b, slot):
        r0 = pl.multiple_of(b * tq, tq)
        q = q_ref[pl.ds(r0, tq), :]
        s_ctx = _nt_dot(q, kc_ref[...])
        m = jnp.maximum(jnp.max(s_ctx, axis=-1, keepdims=True), sink)
        if seq_len:
            ws = window_start(r0)
            s_loc = _nt_dot(q, kl_ref[pl.ds(ws, nk), :])
            s_loc = jnp.where(_window_mask(r0, ws, tq, nk), s_loc, NEG_INF)
            m = jnp.maximum(m, jnp.max(s_loc, axis=-1, keepdims=True))
            p_slots[slot][:, :nk] = jnp.exp2(s_loc - m).astype(BF16)
        p_slots[slot][:, nk:] = jnp.exp2(s_ctx - m).astype(BF16)
        sink_slots[slot][...] = jnp.broadcast_to(jnp.exp2(sink - m), sink_slots[slot].shape)

    def value_stage(b, slot):
        r0 = pl.multiple_of(b * tq, tq)
        acc = jnp.dot(p_slots[slot][:, nk:], vc1_ref[...], preferred_element_type=F32)
        if seq_len:
            acc = acc + jnp.dot(p_slots[slot][:, :nk], vl1_ref[pl.ds(window_start(r0), nk), :],
                                preferred_element_type=F32)
        den = acc[:, dh:dh + 1] + sink_slots[slot][:, 0:1]
        o_ref[pl.ds(r0, tq), :] = (acc[:, :dh] / den).astype(o_ref.dtype)

    _pipeline_blocks(q_ref.shape[0] // tq, depth, score_stage, value_stage)


def _attn_a(z, zc, sink, *, batch, seq_len, n_ctx, q_from_ctx, col_q, col_k, col_v):
    g = A_HEADS // A_KV_HEADS
    if q_from_ctx:
        lq, tq, qsrc, depth = n_ctx, n_ctx, zc, 1
    else:
        lq, tq, qsrc, depth = seq_len, 256, z, 8
    lat_len = 0 if q_from_ctx else seq_len
    lat_block = 16 if q_from_ctx else seq_len
    lat_tiles = seq_len // lat_block
    n_keys = n_ctx + (0 if q_from_ctx else tq + 2 * WINDOW)
    grid_spec = pltpu.PrefetchScalarGridSpec(
        num_scalar_prefetch=1,
        grid=(batch, A_HEADS),
        in_specs=[
            pl.BlockSpec((lq, LANES), lambda b, h, s: (b, col_q + h)),
            pl.BlockSpec((lat_block, LANES), lambda b, h, s: (b * lat_tiles, col_k + h // g)),
            pl.BlockSpec((lat_block, LANES), lambda b, h, s: (b * lat_tiles, col_v + h // g)),
            pl.BlockSpec((n_ctx, LANES), lambda b, h, s: (b, col_k + h // g)),
            pl.BlockSpec((n_ctx, LANES), lambda b, h, s: (b, col_v + h // g)),
        ],
        out_specs=pl.BlockSpec((lq, LANES), lambda b, h, s: (b, h)),
        scratch_shapes=[pltpu.VMEM((lat_block, 2 * LANES), BF16), pltpu.VMEM((n_ctx, 2 * LANES), BF16)]
        + [pltpu.VMEM((tq, n_keys), BF16)] * depth + [pltpu.VMEM((tq, LANES), F32)] * depth,
    )
    return pl.pallas_call(
        functools.partial(_attn_a_kernel, tq=tq, seq_len=lat_len, group=g),
        out_shape=jax.ShapeDtypeStruct((batch * lq, A_HEADS * HEAD_DIM), BF16),
        grid_spec=grid_spec,
        compiler_params=_params(("arbitrary",) * 2, 2 * lq * LANES * 2, 4 * lat_block * LANES * 2,
                                4 * n_ctx * LANES * 2, depth * tq * (n_keys + 2 * LANES)),
        name="attn_window_ctx" if q_from_ctx else "attn_window",
    )(sink, qsrc, z, z, zc, zc)


def _pipeline_blocks(n_blocks, depth, score_stage, value_stage):
    score_stage(0, 0)

    def body(u, carry):
        for i in range(depth):
            b = u * depth + i
            score_stage(jnp.minimum(b + 1, n_blocks - 1), (i + 1) % depth)
            value_stage(b, i)
        return carry

    lax.fori_loop(0, n_blocks // depth, body, 0)


def _join_rows(dst_ref, lat_ref, ctx_ref, has_lat):
    n_lat = lat_ref.shape[0] if has_lat else 0
    if has_lat:
        dst_ref[0:n_lat, 0:lat_ref.shape[1]] = lat_ref[...]
    dst_ref[n_lat:n_lat + ctx_ref.shape[0], 0:ctx_ref.shape[1]] = ctx_ref[...]


def _exp2_numerators(q, k_all):
    s = _nt_dot(q, k_all)
    return jnp.exp2(s - jnp.max(s, axis=-1, keepdims=True))


def _attn_b_kernel(q1_ref, q2_ref, k1l_ref, k2l_ref, vl_ref, k1c_ref, k2c_ref, vc_ref,
                   lq1_ref, lk1_ref, lq2_ref, lk2_ref, sub_ref, o_ref, k1_ref, k2_ref, v_ref, *p_slots,
                   tq, has_lat, lam_init):
    lam = (jnp.exp(jnp.sum(lq1_ref[0] * lk1_ref[0], axis=-1, keepdims=True))
           - jnp.exp(jnp.sum(lq2_ref[0] * lk2_ref[0], axis=-1, keepdims=True)) + lam_init)
    _join_rows(k1_ref, k1l_ref, k1c_ref, has_lat)
    _join_rows(k2_ref, k2l_ref, k2c_ref, has_lat)
    _join_rows(v_ref, vl_ref, vc_ref, has_lat)

    def score_stage(b, slot):
        r0 = pl.multiple_of(b * tq, tq)
        p1 = _exp2_numerators(q1_ref[pl.ds(r0, tq), :], k1_ref[...])
        p2 = _exp2_numerators(q2_ref[pl.ds(r0, tq), :], k2_ref[...])
        w1 = 1.0 / jnp.sum(p1, axis=-1, keepdims=True)
        w2 = lam / jnp.sum(p2, axis=-1, keepdims=True)
        p_slots[slot][...] = (p1 * w1 - p2 * w2).astype(BF16)

    def value_stage(b, slot):
        r0 = pl.multiple_of(b * tq, tq)
        o = jnp.dot(p_slots[slot][...], v_ref[...], preferred_element_type=F32)
        ms = jnp.mean(o * o, axis=-1, keepdims=True)
        o = o * lax.rsqrt(ms + EPS) * sub_ref[...] * (1.0 - lam_init)
        o_ref[pl.ds(r0, tq), :] = o.astype(o_ref.dtype)

    _pipeline_blocks(q1_ref.shape[0] // tq, len(p_slots), score_stage, value_stage)


def _attn_b(z, zc, lq1, lk1, lq2, lk2, subln, *, batch, seq_len, n_ctx, q_from_ctx, col_q, col_k, col_v,
            lam_init):
    dv = 2 * HEAD_DIM
    if q_from_ctx:
        lq, bq, tq, qsrc, depth = n_ctx, n_ctx, n_ctx, zc, 1
    else:
        lq, bq, tq, qsrc, depth = seq_len, seq_len, 256, z, 4
    nqb = lq // bq
    lat_block = 16 if q_from_ctx else seq_len
    lat_tiles = seq_len // lat_block
    n_keys = n_ctx if q_from_ctx else seq_len + n_ctx
    lam_spec = pl.BlockSpec((1, 1, HEAD_DIM), lambda b, h, i: (h, 0, 0))
    in_specs = [
        pl.BlockSpec((bq, LANES), lambda b, h, i: (b * nqb + i, col_q + 2 * h)),
        pl.BlockSpec((bq, LANES), lambda b, h, i: (b * nqb + i, col_q + 2 * h + 1)),
        pl.BlockSpec((lat_block, LANES), lambda b, h, i: (b * lat_tiles, col_k + 2 * h)),
        pl.BlockSpec((lat_block, LANES), lambda b, h, i: (b * lat_tiles, col_k + 2 * h + 1)),
        pl.BlockSpec((lat_block, dv), lambda b, h, i: (b * lat_tiles, col_v // 2 + h)),
        pl.BlockSpec((n_ctx, LANES), lambda b, h, i: (b, col_k + 2 * h)),
        pl.BlockSpec((n_ctx, LANES), lambda b, h, i: (b, col_k + 2 * h + 1)),
        pl.BlockSpec((n_ctx, dv), lambda b, h, i: (b, col_v // 2 + h)),
        lam_spec, lam_spec, lam_spec, lam_spec,
        pl.BlockSpec((1, dv), lambda b, h, i: (0, 0)),
    ]
    return pl.pallas_call(
        functools.partial(_attn_b_kernel, tq=tq, has_lat=not q_from_ctx, lam_init=lam_init),
        out_shape=jax.ShapeDtypeStruct((batch * lq, B_HEADS * dv), BF16),
        grid=(batch, B_HEADS, nqb),
        in_specs=in_specs,
        out_specs=pl.BlockSpec((bq, dv), lambda b, h, i: (b * nqb + i, h)),
        scratch_shapes=[pltpu.VMEM((n_keys, LANES), BF16), pltpu.VMEM((n_keys, LANES), BF16),
                        pltpu.VMEM((n_keys, dv), BF16)] + [pltpu.VMEM((tq, n_keys), BF16)] * depth,
        compiler_params=_params(("arbitrary",) * 3, 2 * bq * LANES * 2, 4 * lat_block * LANES * 2,
                                4 * n_ctx * LANES * 2, bq * dv * 2, (4 * LANES + depth * tq) * n_keys,
                                4 * tq * n_keys * 4 // 2),
        name="attn_diff_ctx" if q_from_ctx else "attn_diff",
    )(qsrc, qsrc, z, z, z, zc, zc, zc,
      lq1.reshape(B_HEADS, 1, HEAD_DIM), lk1.reshape(B_HEADS, 1, HEAD_DIM),
      lq2.reshape(B_HEADS, 1, HEAD_DIM), lk2.reshape(B_HEADS, 1, HEAD_DIM), subln.reshape(1, dv))


def _attn_c_kernel(q_ref, kl_ref, vl_ref, kc_ref, vc_ref, o_ref, k_ref, v1_ref, *p_slots, tq, group):
    dh = vl_ref.shape[1]

    @pl.when(pl.program_id(1) % group == 0)
    def _():
        _join_rows(k_ref, kl_ref, kc_ref, True)
        _join_rows(v1_ref, vl_ref, vc_ref, True)
        v1_ref[:, dh:] = jnp.ones((v1_ref.shape[0], v1_ref.shape[1] - dh), v1_ref.dtype)

    def score_stage(b, slot):
        r0 = pl.multiple_of(b * tq, tq)
        p_slots[slot][...] = _exp2_numerators(q_ref[pl.ds(r0, tq), :], k_ref[...]).astype(BF16)

    def value_stage(b, slot):
        r0 = pl.multiple_of(b * tq, tq)
        acc = jnp.dot(p_slots[slot][...], v1_ref[...], preferred_element_type=F32)
        o_ref[pl.ds(r0, tq), :] = (acc[:, :dh] / acc[:, dh:dh + 1]).astype(o_ref.dtype)

    _pipeline_blocks(q_ref.shape[0] // tq, len(p_slots), score_stage, value_stage)


def _attn_c(z, zc, *, batch, seq_len, n_ctx, n_heads, col_k, col_v):
    g = n_heads // C_KV_HEADS
    tq, depth = 256, 8
    n_keys = seq_len + n_ctx
    return pl.pallas_call(
        functools.partial(_attn_c_kernel, tq=tq, group=g),
        out_shape=jax.ShapeDtypeStruct((batch * seq_len, n_heads * HEAD_DIM), BF16),
        grid=(batch, n_heads),
        scratch_shapes=[pltpu.VMEM((n_keys, LANES), BF16), pltpu.VMEM((n_keys, 2 * LANES), BF16)]
        + [pltpu.VMEM((tq, n_keys), BF16)] * depth,
        in_specs=[
            pl.BlockSpec((seq_len, LANES), lambda b, h: (b, h)),
            pl.BlockSpec((seq_len, LANES), lambda b, h: (b, col_k + h // g)),
            pl.BlockSpec((seq_len, LANES), lambda b, h: (b, col_v + h // g)),
            pl.BlockSpec((n_ctx, LANES), lambda b, h: (b, col_k + h // g)),
            pl.BlockSpec((n_ctx, LANES), lambda b, h: (b, col_v + h // g)),
        ],
        out_specs=pl.BlockSpec((seq_len, LANES), lambda b, h: (b, h)),
        compiler_params=_params(("arbitrary",) * 2, 4 * seq_len * LANES * 2, 2 * n_ctx * LANES * 2,
                                (3 * LANES + depth * tq) * n_keys, 3 * tq * n_keys * 4 // 2),
        name="attn_gqa",
    )(z, z, z, zc, zc)


def _outproj_kernel(*refs, n_lhs):
    lhs = refs[:n_lhs]
    w_ref, res_ref, gate_ref, o_ref = refs[n_lhs:]
    acc = None
    k0 = 0
    for a_ref in lhs:
        kw = a_ref.shape[1]
        part = jnp.dot(a_ref[...], w_ref[k0:k0 + kw, :], preferred_element_type=F32)
        acc = part if acc is None else acc + part
        k0 += kw
    o_ref[...] = res_ref[...] + gate_ref[0] * acc


def _out_proj(lhs_list, w_bf16, res2d, gate, *, tm, rows_per_mod):
    rows, n = res2d.shape
    kdim = w_bf16.shape[0]
    tn = 512
    in_specs = [pl.BlockSpec((tm, a.shape[1]), lambda i, j: (i, 0)) for a in lhs_list]
    in_specs += [
        pl.BlockSpec((kdim, tn), lambda i, j: (0, j)),
        pl.BlockSpec((tm, tn), lambda i, j: (i, j)),
        pl.BlockSpec((1, 1, tn), lambda i, j: ((i * tm) // rows_per_mod, 0, j)),
    ]
    return pl.pallas_call(
        functools.partial(_outproj_kernel, n_lhs=len(lhs_list)),
        out_shape=jax.ShapeDtypeStruct((rows, n), F32),
        grid=(rows // tm, n // tn),
        in_specs=in_specs,
        out_specs=pl.BlockSpec((tm, tn), lambda i, j: (i, j)),
        compiler_params=_params(("arbitrary", "arbitrary"), tm * kdim * 2, kdim * tn * 2, 2 * tm * tn * 4),
        name="out_proj",
    )(*lhs_list, w_bf16, res2d, gate)


def _ffn_kernel(x_ref, g_ref, sh_ref, sc_ref, gate_ref, wg_ref, wu_ref, wd_ref, o_ref, h_ref, acc_ref):
    f = pl.program_id(1)

    @pl.when(f == 0)
    def _():
        h_ref[...] = _rms_mod(x_ref[...], g_ref[...], sh_ref[0], sc_ref[0]).astype(BF16)
        acc_ref[...] = jnp.zeros_like(acc_ref)

    h = h_ref[...]
    g_ = jnp.dot(h, wg_ref[...], preferred_element_type=F32)
    u_ = jnp.dot(h, wu_ref[...], preferred_element_type=F32)
    a = (g_ * jax.nn.sigmoid(g_) * u_).astype(BF16)
    acc_ref[...] += jnp.dot(a, wd_ref[...], preferred_element_type=F32)

    @pl.when(f == pl.num_programs(1) - 1)
    def _():
        o_ref[...] = x_ref[...] + gate_ref[0] * acc_ref[...]


def _dense_ffn(x2d, norm_g, shift, scale, gate, w_in_bf16, w_down_bf16, *, tm, rows_per_mod):
    rows, d = x2d.shape
    d_ff = w_down_bf16.shape[0]
    tf = 512
    nf = d_ff // tf
    mod_spec = pl.BlockSpec((1, 1, d), lambda i, f: ((i * tm) // rows_per_mod, 0, 0))
    return pl.pallas_call(
        _ffn_kernel,
        out_shape=jax.ShapeDtypeStruct((rows, d), F32),
        grid=(rows // tm, nf),
        in_specs=[
            pl.BlockSpec((tm, d), lambda i, f: (i, 0)),
            pl.BlockSpec((1, d), lambda i, f: (0, 0)),
            mod_spec, mod_spec, mod_spec,
            pl.BlockSpec((d, tf), lambda i, f: (0, f)),
            pl.BlockSpec((d, tf), lambda i, f: (0, nf + f)),
            pl.BlockSpec((tf, d), lambda i, f: (f, 0)),
        ],
        out_specs=pl.BlockSpec((tm, d), lambda i, f: (i, 0)),
        scratch_shapes=[pltpu.VMEM((tm, d), BF16), pltpu.VMEM((tm, d), F32)],
        compiler_params=_params(("arbitrary", "arbitrary"), 2 * tm * d * 4, 3 * d * tf * 2, tm * d * 3),
        name="dense_swiglu",
    )(x2d, norm_g.reshape(1, d), shift, scale, gate, w_in_bf16, w_in_bf16, w_down_bf16)


def _split3(v):
    hi = v.astype(BF16)
    r = v - hi.astype(F32)
    mid = r.astype(BF16)
    lo = (r - mid.astype(F32)).astype(BF16)
    return hi, mid, lo


def _router_kernel(x_ref, g_ref, sh_ref, sc_ref, wr_ref, h_ref, ids_ref, gates_ref, cnt_ref, carry_ref):
    i = pl.program_id(0)
    tm = x_ref.shape[0]

    @pl.when(i == 0)
    def _():
        carry_ref[...] = jnp.zeros_like(carry_ref)

    h = _rms_mod(x_ref[...], g_ref[...], sh_ref[0], sc_ref[0])
    h_ref[...] = h
    h0, h1, h2 = _split3(h)
    w0, w1, w2 = _split3(wr_ref[...])
    dot = functools.partial(jnp.dot, preferred_element_type=F32)
    logits = (dot(h0, w0) + (dot(h0, w1) + dot(h1, w0))
              + (dot(h0, w2) + dot(h1, w1) + dot(h2, w0)))
    lane = lax.broadcasted_iota(jnp.int32, (tm, LANES), 1)
    logits = jnp.where(lane < N_EXPERTS, logits, -jnp.inf)
    v0 = jnp.max(logits, axis=-1, keepdims=True)
    i0 = jnp.min(jnp.where(logits == v0, lane, LANES), axis=-1, keepdims=True)
    rest = jnp.where(lane == i0, -jnp.inf, logits)
    v1 = jnp.max(rest, axis=-1, keepdims=True)
    i1 = jnp.min(jnp.where(rest == v1, lane, LANES), axis=-1, keepdims=True)
    e1 = jnp.exp(v1 - v0)
    g0 = 1.0 / (1.0 + e1)
    g1 = e1 / (1.0 + e1)

    sel = (lane == i0) | (lane == i1)
    row = lax.broadcasted_iota(jnp.int32, (tm, tm), 0)
    col = lax.broadcasted_iota(jnp.int32, (tm, tm), 1)
    tri = (col < row).astype(BF16)
    before = jnp.dot(tri, sel.astype(BF16), preferred_element_type=F32) + carry_ref[...]
    r0 = jnp.sum(jnp.where(lane == i0, before, 0.0), axis=-1, keepdims=True).astype(jnp.int32)
    r1 = jnp.sum(jnp.where(lane == i1, before, 0.0), axis=-1, keepdims=True).astype(jnp.int32)
    total = carry_ref[...] + jnp.sum(sel.astype(F32), axis=0, keepdims=True)
    carry_ref[...] = total

    ids_ref[...] = jnp.where(lane == 0, i0, jnp.where(lane == 1, i1, jnp.where(lane == 2, r0, r1)))
    gates_ref[...] = jnp.where(lane == 0, g0, g1)
    cnt_ref[...] = jnp.broadcast_to(total, cnt_ref.shape).astype(jnp.int32)


def _router(x2d, norm_g, shift, scale, w_router, *, tm, rows_per_mod):
    rows, d = x2d.shape
    wr = jnp.zeros((d, LANES), F32).at[:, :N_EXPERTS].set(w_router)
    mod_spec = pl.BlockSpec((1, 1, d), lambda i: ((i * tm) // rows_per_mod, 0, 0))
    return pl.pallas_call(
        _router_kernel,
        out_shape=(jax.ShapeDtypeStruct((rows, d), F32),
                   jax.ShapeDtypeStruct((rows, LANES), jnp.int32),
                   jax.ShapeDtypeStruct((rows, LANES), F32),
                   jax.ShapeDtypeStruct((8, LANES), jnp.int32)),
        grid=(rows // tm,),
        in_specs=[pl.BlockSpec((tm, d), lambda i: (i, 0)),
                  pl.BlockSpec((1, d), lambda i: (0, 0)),
                  mod_spec, mod_spec,
                  pl.BlockSpec((d, LANES), lambda i: (0, 0))],
        out_specs=(pl.BlockSpec((tm, d), lambda i: (i, 0)),
                   pl.BlockSpec((tm, LANES), lambda i: (i, 0)),
                   pl.BlockSpec((tm, LANES), lambda i: (i, 0)),
                   pl.BlockSpec((8, LANES), lambda i: (0, 0))),
        scratch_shapes=[pltpu.VMEM((1, LANES), F32)],
        compiler_params=_params(("arbitrary",), 2 * tm * d * 4, d * LANES * 4, 2 * tm * LANES * 4),
        name="router",
    )(x2d, norm_g.reshape(1, d), shift, scale, wr)


GATHER_UNROLL = 8


def _start_row_gather(src_hbm, idx_ref, base, dst_ref, sem):
    def body(r8, c):
        for u in range(GATHER_UNROLL):
            r = r8 * GATHER_UNROLL + u
            pltpu.make_async_copy(src_hbm.at[pl.ds(idx_ref[base + r], 1), :],
                                  dst_ref.at[pl.ds(r, 1), :], sem).start(priority=u % 2)
        return c

    lax.fori_loop(0, dst_ref.shape[0] // GATHER_UNROLL, body, 0)


def _wait_row_gather(src_hbm, dst_ref, sem):
    pltpu.make_async_copy(src_hbm.at[pl.ds(0, dst_ref.shape[0]), :], dst_ref, sem).wait()


def _dispatch_kernel(idx_ref, src_hbm, o_ref, buf_ref, sem):
    tr = o_ref.shape[0]
    i, n = pl.program_id(0), pl.num_programs(0)
    slot = i & 1

    @pl.when(i == 0)
    def _():
        _start_row_gather(src_hbm, idx_ref, 0, buf_ref.at[0], sem.at[0])

    @pl.when(i + 1 < n)
    def _():
        _start_row_gather(src_hbm, idx_ref, (i + 1) * tr, buf_ref.at[1 - slot], sem.at[1 - slot])

    _wait_row_gather(src_hbm, buf_ref.at[slot], sem.at[slot])
    o_ref[...] = buf_ref[slot].astype(o_ref.dtype)


def _dispatch(h2d, row_tok, *, tr):
    n_rows = row_tok.shape[0]
    d = h2d.shape[1]
    grid_spec = pltpu.PrefetchScalarGridSpec(
        num_scalar_prefetch=1,
        grid=(n_rows // tr,),
        in_specs=[pl.BlockSpec(memory_space=pl.ANY)],
        out_specs=pl.BlockSpec((tr, d), lambda i, idx: (i, 0)),
        scratch_shapes=[pltpu.VMEM((2, tr, d), F32), pltpu.SemaphoreType.DMA((2,))],
    )
    return pl.pallas_call(
        _dispatch_kernel,
        out_shape=jax.ShapeDtypeStruct((n_rows, d), BF16),
        grid_spec=grid_spec,
        compiler_params=_params(("arbitrary",), tr * d * 2, tr * d * 4),
        name="expert_dispatch",
    )(row_tok, h2d)


def _combine_kernel(d0_ref, d1_ref, ys_hbm, x_ref, gate_ref, rg_ref, o_ref, a_ref, b_ref, sem):
    tt = x_ref.shape[0]
    i, n = pl.program_id(0), pl.num_programs(0)
    slot = i & 1

    def start(tile, s):
        _start_row_gather(ys_hbm, d0_ref, tile * tt, a_ref.at[s], sem.at[s, 0])
        _start_row_gather(ys_hbm, d1_ref, tile * tt, b_ref.at[s], sem.at[s, 1])

    @pl.when(i == 0)
    def _():
        start(0, 0)

    @pl.when(i + 1 < n)
    def _():
        start(i + 1, 1 - slot)

    _wait_row_gather(ys_hbm, a_ref.at[slot], sem.at[slot, 0])
    _wait_row_gather(ys_hbm, b_ref.at[slot], sem.at[slot, 1])
    rg = rg_ref[...]
    mix = rg[:, 0:1] * a_ref[slot] + rg[:, 1:2] * b_ref[slot]
    o_ref[...] = x_ref[...] + gate_ref[0] * mix


def _combine(ys, dest0, dest1, x2d, gate, row_gates, *, tt, rows_per_mod):
    rows, d = x2d.shape
    grid_spec = pltpu.PrefetchScalarGridSpec(
        num_scalar_prefetch=2,
        grid=(rows // tt,),
        in_specs=[pl.BlockSpec(memory_space=pl.ANY),
                  pl.BlockSpec((tt, d), lambda i, a, b: (i, 0)),
                  pl.BlockSpec((1, 1, d), lambda i, a, b: ((i * tt) // rows_per_mod, 0, 0)),
                  pl.BlockSpec((tt, LANES), lambda i, a, b: (i, 0))],
        out_specs=pl.BlockSpec((tt, d), lambda i, a, b: (i, 0)),
        scratch_shapes=[pltpu.VMEM((2, tt, d), F32), pltpu.VMEM((2, tt, d), F32),
                        pltpu.SemaphoreType.DMA((2, 2))],
    )
    return pl.pallas_call(
        _combine_kernel,
        out_shape=jax.ShapeDtypeStruct((rows, d), F32),
        grid_spec=grid_spec,
        compiler_params=_params(("arbitrary",), 2 * tt * d * 4, 2 * tt * d * 4),
        name="expert_combine",
    )(dest0, dest1, ys, x2d, gate, row_gates)


WEIGHT_DMA_PRIORITY = 1


def _stream_group_weights(sched, copies, cast):
    te_ref, first_ref, grp_ref, ngrp_ref, nxt_ref = sched
    j, r = pl.program_id(0), pl.program_id(1)
    nj = pl.num_programs(0)

    @pl.when(first_ref[r] == 1)
    def _():
        g = grp_ref[r]
        ng = ngrp_ref[0]

        @pl.when((j == 0) & (g == 0))
        def _():
            for cp in copies(te_ref[r], j):
                cp.start(priority=WEIGHT_DMA_PRIORITY)

        for cp in copies(te_ref[r], j):
            cp.wait()
        cast()
        last = g == ng - 1

        @pl.when(jnp.logical_not(last & (j == nj - 1)))
        def _():
            for cp in copies(nxt_ref[r], jnp.where(last, j + 1, j)):
                cp.start(priority=WEIGHT_DMA_PRIORITY)


CAST_ROWS = 256


def _cast_rows(src_ref, dst_ref):
    def body(i, carry):
        rows = pl.ds(pl.multiple_of(i * CAST_ROWS, CAST_ROWS), CAST_ROWS)
        dst_ref[rows, :] = src_ref[rows, :].astype(dst_ref.dtype)
        return carry

    lax.fori_loop(0, src_ref.shape[0] // CAST_ROWS, body, 0)


def _gmm_up_kernel(te_ref, first_ref, grp_ref, ngrp_ref, nxt_ref, nv_ref, x_ref, w_hbm, o_ref,
                   stage_ref, wbf_ref, sem):
    tn = o_ref.shape[1]
    nj = pl.num_programs(0)

    def copies(e, j):
        return [pltpu.make_async_copy(w_hbm.at[e, :, pl.ds(pl.multiple_of((half * nj + j) * tn, tn), tn)],
                                      stage_ref.at[half], sem.at[half]) for half in range(2)]

    def cast():
        _cast_rows(stage_ref.at[0], wbf_ref.at[0])
        _cast_rows(stage_ref.at[1], wbf_ref.at[1])

    _stream_group_weights((te_ref, first_ref, grp_ref, ngrp_ref, nxt_ref), copies, cast)
    valid = pl.program_id(1) < nv_ref[0]

    @pl.when(valid)
    def _():
        x = x_ref[...]
        g_ = jnp.dot(x, wbf_ref[0], preferred_element_type=F32)
        u_ = jnp.dot(x, wbf_ref[1], preferred_element_type=F32)
        o_ref[...] = (g_ * jax.nn.sigmoid(g_) * u_).astype(o_ref.dtype)

    @pl.when(jnp.logical_not(valid))
    def _():
        o_ref[...] = jnp.zeros_like(o_ref)


def _gmm_down_kernel(te_ref, first_ref, grp_ref, ngrp_ref, nxt_ref, nv_ref, h_ref, w_hbm, o_ref,
                     stage_ref, wbf_ref, sem):
    tn = o_ref.shape[1]

    def copies(e, j):
        return [pltpu.make_async_copy(w_hbm.at[e, :, pl.ds(pl.multiple_of(j * tn, tn), tn)], stage_ref, sem)]

    def cast():
        _cast_rows(stage_ref, wbf_ref)

    _stream_group_weights((te_ref, first_ref, grp_ref, ngrp_ref, nxt_ref), copies, cast)
    valid = pl.program_id(1) < nv_ref[0]

    @pl.when(valid)
    def _():
        o_ref[...] = jnp.dot(h_ref[...], wbf_ref[...], preferred_element_type=F32)

    @pl.when(jnp.logical_not(valid))
    def _():
        o_ref[...] = jnp.zeros_like(o_ref)


def _expert_ffn(xs, w_in, w_down, sched, n_valid, *, tr):
    n_rows, d = xs.shape
    d_ff = w_down.shape[1]
    n_tiles = n_rows // tr
    n_sched = len(sched) + 1
    tn1 = d_ff // 4
    nj1 = d_ff // tn1
    small_temps = 4 * 1024 * 1024

    def row_map(*args):
        r, nv = args[1], args[-1]
        return (jnp.minimum(r, nv[0] - 1), 0)

    def out_map(*args):
        return (args[1], args[0])

    up_spec = pltpu.PrefetchScalarGridSpec(
        num_scalar_prefetch=n_sched,
        grid=(nj1, n_tiles),
        in_specs=[pl.BlockSpec((tr, d), row_map), pl.BlockSpec(memory_space=pl.ANY)],
        out_specs=pl.BlockSpec((tr, tn1), out_map),
        scratch_shapes=[pltpu.VMEM((2, d, tn1), F32), pltpu.VMEM((2, d, tn1), BF16),
                        pltpu.SemaphoreType.DMA((2,))],
    )
    stage1 = 2 * d * tn1 * 4
    hidden = pl.pallas_call(
        _gmm_up_kernel,
        out_shape=jax.ShapeDtypeStruct((n_rows, d_ff), BF16),
        grid_spec=up_spec,
        compiler_params=pltpu.CompilerParams(
            dimension_semantics=("arbitrary", "arbitrary"),
            vmem_limit_bytes=stage1 + 2 * d * tn1 * 2 + 2 * (tr * d * 2 + tr * tn1 * 2) + 2 * small_temps),
        name="expert_up",
    )(*sched, n_valid, xs, w_in)

    tn2 = d // 2
    down_spec = pltpu.PrefetchScalarGridSpec(
        num_scalar_prefetch=n_sched,
        grid=(d // tn2, n_tiles),
        in_specs=[pl.BlockSpec((tr, d_ff), row_map), pl.BlockSpec(memory_space=pl.ANY)],
        out_specs=pl.BlockSpec((tr, tn2), out_map),
        scratch_shapes=[pltpu.VMEM((d_ff, tn2), F32), pltpu.VMEM((d_ff, tn2), BF16),
                        pltpu.SemaphoreType.DMA(())],
    )
    stage2 = d_ff * tn2 * 4
    return pl.pallas_call(
        _gmm_down_kernel,
        out_shape=jax.ShapeDtypeStruct((n_rows, d), F32),
        grid_spec=down_spec,
        compiler_params=pltpu.CompilerParams(
            dimension_semantics=("arbitrary", "arbitrary"),
            vmem_limit_bytes=stage2 + d_ff * tn2 * 2 + 2 * (tr * d_ff * 2 + tr * tn2 * 4) + small_temps),
        name="expert_down",
    )(*sched, n_valid, hidden, w_down)


def _moe(x2d, norm_g, shift, scale, gate, w_router, w_in, w_down, *, rows_per_mod):
    rows, d = x2d.shape
    tr = 256
    h2d, ids, gates, counts = _router(x2d, norm_g, shift, scale, w_router, tm=512, rows_per_mod=rows_per_mod)
    counts = counts[0, :N_EXPERTS]
    padded = (counts + tr - 1) // tr * tr
    pad_end = jnp.cumsum(padded)
    pad_start = pad_end - padded
    dest0 = pad_start[ids[:, 0]] + ids[:, 2]
    dest1 = pad_start[ids[:, 1]] + ids[:, 3]
    n_rows = 2 * rows + N_EXPERTS * tr
    n_tiles = n_rows // tr
    tok = jnp.arange(rows, dtype=jnp.int32)
    row_tok = jnp.zeros((n_rows,), jnp.int32).at[jnp.concatenate([dest0, dest1])].set(
        jnp.concatenate([tok, tok]))
    tile_expert = jnp.minimum(
        jnp.sum(jnp.arange(n_tiles, dtype=jnp.int32)[:, None] * tr >= pad_end[None, :], axis=1),
        N_EXPERTS - 1).astype(jnp.int32)
    n_valid = (pad_end[-1:] // tr).astype(jnp.int32)
    tile_id = jnp.arange(n_tiles, dtype=jnp.int32)
    prev_expert = jnp.concatenate([jnp.full((1,), -1, jnp.int32), tile_expert[:-1]])
    first = ((tile_id < n_valid[0]) & (tile_expert != prev_expert)).astype(jnp.int32)
    grp = jnp.cumsum(first).astype(jnp.int32) - 1
    n_grp = jnp.sum(first, keepdims=True).astype(jnp.int32)
    expert_id = jnp.arange(N_EXPERTS, dtype=jnp.int32)
    grp_expert = jnp.sort(jnp.where(counts > 0, expert_id, N_EXPERTS))
    nxt = grp_expert[(grp + 1) % n_grp[0]].astype(jnp.int32)
    sched = (tile_expert, first, grp, n_grp, nxt)

    xs = _dispatch(h2d, row_tok, tr=tr)
    ys = _expert_ffn(xs, w_in, w_down, sched, n_valid, tr=tr)
    return _combine(ys, dest0, dest1, x2d, gate, gates, tt=256, rows_per_mod=rows_per_mod)


def _rope_tables(seq_len):
    t = np.arange(seq_len)
    row = (t // GRID_W).astype(np.float64)[:, None]
    col = (t % GRID_W).astype(np.float64)[:, None]
    inv = ROPE_THETA ** (-np.arange(0, ROT_AXIS, 2, dtype=np.float64) / ROT_AXIS)
    ar, ac = row * inv, col * inv
    zero = np.zeros_like(ar)
    cos = np.concatenate([np.cos(ar), np.cos(ar), np.cos(ac), np.cos(ac)], axis=-1)
    s_next = np.concatenate([-np.sin(ar), zero, -np.sin(ac), zero], axis=-1)
    s_prev = np.concatenate([zero, np.sin(ar), zero, np.sin(ac)], axis=-1)
    return tuple(jnp.asarray(a.astype(np.float32)) for a in (cos, s_next, s_prev))


def _mod_vectors(c, c_ctx, w_mod, b_mod):
    batch, d = c.shape
    cvec = jnp.zeros((8, d), F32).at[:batch].set(c).at[batch].set(c_ctx)
    m = _modulation(jnp.concatenate([cvec, cvec], axis=0), w_mod, b_mod)
    lat = m[:batch].reshape(batch, N_MOD, 1, d)
    ctx = m[batch].reshape(N_MOD, 1, 1, d)
    return [lat[:, k] for k in range(N_MOD)], [ctx[k] for k in range(N_MOD)]


def kernel(x, c, ctx, c_ctx, e_norm1, e_norm2, e_w_mod, e_b_mod, e_w_in, e_w_out, e_a_qnorm, e_a_knorm,
           e_a_sink, e_b_qnorm, e_b_knorm, e_b_lam_q1, e_b_lam_k1, e_b_lam_q2, e_b_lam_k2, e_b_subln,
           e_ffn_w_in, e_ffn_w_down, o_norm1, o_norm2, o_w_mod, o_b_mod, o_w_in, o_w_out, o_c_qnorm,
           o_c_knorm, o_router, o_exp_w_in, o_exp_w_down):
    batch, seq_len, d = x.shape
    n_ctx = ctx.shape[1]
    qk_scale = HEAD_DIM ** -0.5 * LOG2_E
    tables = _rope_tables(seq_len)
    ctx_tables = tuple(t[:batch * n_ctx] for t in tables)
    x2d = x.reshape(batch * seq_len, d)
    xc2d = ctx.reshape(batch * n_ctx, d)
    tm = 1024

    (sh1, sc1, g1, sh2, sc2, g2), (csh1, csc1, cg1, csh2, csc2, cg2) = _mod_vectors(
        c, c_ctx, e_w_mod[0], e_b_mod[0])
    a_q, a_kv = A_HEADS * HEAD_DIM, A_KV_HEADS * HEAD_DIM
    b_qk, b_v = B_HEADS * 2 * HEAD_DIM, B_HEADS * 2 * HEAD_DIM
    ones = lambda n: jnp.ones((n,), F32)
    tile = lambda v, n: jnp.tile(v, n // HEAD_DIM)
    col_gain = jnp.concatenate([tile(e_a_qnorm[0] * qk_scale, a_q), tile(e_a_knorm[0], a_kv), ones(a_kv),
                                tile(e_b_qnorm[0] * qk_scale, b_qk), tile(e_b_knorm[0], b_qk), ones(b_v)])
    kinds = lambda *pairs: jnp.concatenate([jnp.full((n // HEAD_DIM,), k, jnp.int32) for k, n in pairs])
    col_kind = kinds((1, a_q), (1, a_kv), (0, a_kv), (1, b_qk), (1, b_qk), (0, b_v))
    w_in0 = e_w_in[0].astype(BF16)
    z = _in_proj(x2d, e_norm1[0], sh1, sc1, w_in0, col_gain, col_kind, tables,
                 tm=tm, rows_per_mod=seq_len, rope=True)
    zc = _in_proj(xc2d, e_norm1[0], csh1, csc1, w_in0, col_gain, col_kind, ctx_tables,
                  tm=batch * n_ctx, rows_per_mod=batch * n_ctx, rope=False)

    ca_q, ca_k, ca_v = 0, a_q // LANES, (a_q + a_kv) // LANES
    cb_q = (a_q + 2 * a_kv) // LANES
    cb_k, cb_v = cb_q + b_qk // LANES, cb_q + 2 * b_qk // LANES
    lam_init = 0.8 - 0.6 * math.exp(-0.3 * 0)
    dims = dict(batch=batch, seq_len=seq_len, n_ctx=n_ctx)
    attn_a = functools.partial(_attn_a, z, zc, e_a_sink[0], col_q=ca_q, col_k=ca_k, col_v=ca_v, **dims)
    attn_b = functools.partial(_attn_b, z, zc, e_b_lam_q1[0], e_b_lam_k1[0], e_b_lam_q2[0], e_b_lam_k2[0],
                               e_b_subln[0], col_q=cb_q, col_k=cb_k, col_v=cb_v, lam_init=lam_init, **dims)
    w_out0 = e_w_out[0].astype(BF16)
    x2d = _out_proj([attn_a(q_from_ctx=False), attn_b(q_from_ctx=False)], w_out0, x2d, g1,
                    tm=tm, rows_per_mod=seq_len)
    xc2d = _out_proj([attn_a(q_from_ctx=True), attn_b(q_from_ctx=True)], w_out0, xc2d, cg1,
                     tm=batch * n_ctx, rows_per_mod=batch * n_ctx)
    ffn_in, ffn_down = e_ffn_w_in[0].astype(BF16), e_ffn_w_down[0].astype(BF16)
    x2d = _dense_ffn(x2d, e_norm2[0], sh2, sc2, g2, ffn_in, ffn_down, tm=512, rows_per_mod=seq_len)
    xc2d = _dense_ffn(xc2d, e_norm2[0], csh2, csc2, cg2, ffn_in, ffn_down, tm=batch * n_ctx,
                      rows_per_mod=batch * n_ctx)

    (sh1, sc1, g1, sh2, sc2, g2), (csh1, csc1, _, _, _, _) = _mod_vectors(c, c_ctx, o_w_mod[0], o_b_mod[0])
    c_q = d
    c_kv = C_KV_HEADS * HEAD_DIM
    col_gain = jnp.concatenate([tile(o_c_qnorm[0] * qk_scale, c_q), tile(o_c_knorm[0], c_kv), ones(c_kv)])
    col_kind = kinds((1, c_q), (1, c_kv), (0, c_kv))
    w_in1 = o_w_in[0].astype(BF16)
    z = _in_proj(x2d, o_norm1[0], sh1, sc1, w_in1, col_gain, col_kind, tables,
                 tm=tm, rows_per_mod=seq_len, rope=True)
    zc = _in_proj(xc2d, o_norm1[0], csh1, csc1, w_in1, col_gain, col_kind, ctx_tables,
                  tm=batch * n_ctx, rows_per_mod=batch * n_ctx, rope=False)
    o = _attn_c(z, zc, n_heads=c_q // HEAD_DIM, col_k=c_q // LANES, col_v=(c_q + c_kv) // LANES, **dims)
    x2d = _out_proj([o], o_w_out[0].astype(BF16), x2d, g1, tm=tm, rows_per_mod=seq_len)
    x2d = _moe(x2d, o_norm2[0], sh2, sc2, g2, o_router[0], o_exp_w_in.reshape(o_exp_w_in.shape[1:]),
               o_exp_w_down.reshape(o_exp_w_down.shape[1:]), rows_per_mod=seq_len)
    return x2d.reshape(batch, seq_len, d)
```

```python
import functools
import math

import jax
import jax.numpy as jnp
import numpy as np
from jax import lax
from jax.experimental import pallas as pl
from jax.experimental.pallas import tpu as pltpu

F32 = jnp.float32
BF16 = jnp.bfloat16

HEAD_DIM = 128
GRID_W = 64
ROT_AXIS = HEAD_DIM // 2
ROPE_THETA = 10000.0
EPS = 1e-6
NEG_INF = -1e30
LOG2_E = math.log2(math.e)
N_MOD = 6
WINDOW = 128
A_HEADS, A_KV_HEADS = 8, 2
B_HEADS = 4
C_KV_HEADS = 4
N_EXPERTS = 8
LANES = 128
V7X_VMEM_BYTES = 64 * 1024 * 1024
VMEM_TEMP_ALLOWANCE = 16 * 1024 * 1024


def _vmem_limit(*block_bytes):
    need = 2 * sum(block_bytes) + VMEM_TEMP_ALLOWANCE
    return int(min(need, V7X_VMEM_BYTES - 6 * 1024 * 1024))


def _params(sem, *block_bytes):
    return pltpu.CompilerParams(dimension_semantics=sem, vmem_limit_bytes=_vmem_limit(*block_bytes))


def _nt_dot(a, b):
    return lax.dot_general(a, b, (((1,), (1,)), ((), ())), preferred_element_type=F32)


def _rms_mod(x, g, shift, scale):
    ms = jnp.mean(x * x, axis=-1, keepdims=True)
    y = x * lax.rsqrt(ms + EPS) * g
    return y * (1.0 + scale) + shift


def _mod_kernel(c_ref, w_ref, b_ref, o_ref):
    c = c_ref[...]
    s = c * jax.nn.sigmoid(c)
    s_hi = s.astype(BF16).astype(F32)
    top = lax.broadcasted_iota(jnp.int32, s.shape, 0) < 8
    lhs = jnp.where(top, s_hi, s - s_hi).astype(BF16)
    acc = jnp.dot(lhs, w_ref[...].astype(BF16), preferred_element_type=F32)
    o_ref[...] = acc[0:8] + acc[8:16] + b_ref[...]


def _modulation(cvec, w_mod, b_mod):
    d, n = w_mod.shape
    tn = 1024
    return pl.pallas_call(
        _mod_kernel,
        out_shape=jax.ShapeDtypeStruct((8, n), F32),
        grid=(n // tn,),
        in_specs=[pl.BlockSpec((16, d), lambda j: (0, 0)),
                  pl.BlockSpec((d, tn), lambda j: (0, j)),
                  pl.BlockSpec((1, tn), lambda j: (0, j))],
        out_specs=pl.BlockSpec((8, tn), lambda j: (0, j)),
        compiler_params=_params(("arbitrary",), d * tn * 4, d * tn * 2),
        name="modulation",
    )(cvec, w_mod, b_mod.reshape(1, n))


def _inproj_kernel(kind_ref, x_ref, g_ref, sh_ref, sc_ref, w_ref, gc_ref, cos_ref, sa_ref, sb_ref,
                   o_ref, h_ref, *, rope):
    j = pl.program_id(1)
    tn = o_ref.shape[1]
    nch = tn // LANES

    @pl.when(j == 0)
    def _():
        h_ref[...] = _rms_mod(x_ref[...], g_ref[...], sh_ref[0], sc_ref[0]).astype(BF16)

    acc = jnp.dot(h_ref[...], w_ref[...], preferred_element_type=F32)
    for c in range(nch):
        a = acc[:, c * LANES:(c + 1) * LANES]
        kind = kind_ref[j * nch + c]

        @pl.when(kind == 0)
        def _():
            o_ref[:, c * LANES:(c + 1) * LANES] = a.astype(o_ref.dtype)

        @pl.when(kind == 1)
        def _():
            ms = jnp.mean(a * a, axis=-1, keepdims=True)
            y = a * lax.rsqrt(ms + EPS) * gc_ref[:, c * LANES:(c + 1) * LANES]
            if rope:
                y = (y * cos_ref[...] + pltpu.roll(y, LANES - 32, 1) * sa_ref[...]
                     + pltpu.roll(y, 32, 1) * sb_ref[...])
            o_ref[:, c * LANES:(c + 1) * LANES] = y.astype(o_ref.dtype)


def _in_proj(x2d, norm_g, shift, scale, w_bf16, col_gain, col_kind, tables, *, tm, rows_per_mod, rope):
    rows, d = x2d.shape
    n = w_bf16.shape[1]
    tn = 512
    cos_t, sa_t, sb_t = tables
    pos_tiles = cos_t.shape[0] // tm

    grid_spec = pltpu.PrefetchScalarGridSpec(
        num_scalar_prefetch=1,
        grid=(rows // tm, n // tn),
        in_specs=[
            pl.BlockSpec((tm, d), lambda i, j, k: (i, 0)),
            pl.BlockSpec((1, d), lambda i, j, k: (0, 0)),
            pl.BlockSpec((1, 1, d), lambda i, j, k: ((i * tm) // rows_per_mod, 0, 0)),
            pl.BlockSpec((1, 1, d), lambda i, j, k: ((i * tm) // rows_per_mod, 0, 0)),
            pl.BlockSpec((d, tn), lambda i, j, k: (0, j)),
            pl.BlockSpec((1, tn), lambda i, j, k: (0, j)),
            pl.BlockSpec((tm, LANES), lambda i, j, k: (i % pos_tiles, 0)),
            pl.BlockSpec((tm, LANES), lambda i, j, k: (i % pos_tiles, 0)),
            pl.BlockSpec((tm, LANES), lambda i, j, k: (i % pos_tiles, 0)),
        ],
        out_specs=pl.BlockSpec((tm, tn), lambda i, j, k: (i, j)),
        scratch_shapes=[pltpu.VMEM((tm, d), BF16)],
    )
    return pl.pallas_call(
        functools.partial(_inproj_kernel, rope=rope),
        out_shape=jax.ShapeDtypeStruct((rows, n), BF16),
        grid_spec=grid_spec,
        compiler_params=_params(("arbitrary", "arbitrary"), tm * d * 4, d * tn * 2, tm * tn * 2,
                                3 * tm * LANES * 4, tm * d),
        name="in_proj_rope" if rope else "in_proj_ctx",
    )(col_kind, x2d, norm_g.reshape(1, d), shift, scale, w_bf16, col_gain.reshape(1, n), cos_t, sa_t, sb_t)


def _window_mask(q0, ws, tq, nk):
    qpos = q0 + lax.broadcasted_iota(jnp.int32, (tq, nk), 0)
    kpos = ws + lax.broadcasted_iota(jnp.int32, (tq, nk), 1)
    return jnp.abs(qpos - kpos) <= WINDOW


def _attn_a_kernel(sink_ref, q_ref, kl_ref, vl_ref, kc_ref, vc_ref, o_ref, vl1_ref, vc1_ref, *slots,
                   tq, seq_len, group):
    depth = len(slots) // 2
    p_slots, sink_slots = slots[:depth], slots[depth:]
    sink = sink_ref[pl.program_id(1)] * LOG2_E
    dh = vc_ref.shape[1]
    nk = tq + 2 * WINDOW if seq_len else 0

    @pl.when(pl.program_id(1) % group == 0)
    def _():
        if seq_len:
            vl1_ref[:, :dh] = vl_ref[...]
            vl1_ref[:, dh:] = jnp.ones_like(vl_ref)
        vc1_ref[:, :dh] = vc_ref[...]
        vc1_ref[:, dh:] = jnp.ones_like(vc_ref)

    def window_start(r0):
        return pl.multiple_of(jnp.clip(r0 - WINDOW, 0, seq_len - nk), WINDOW)

    def score_stage(b, slot):
        r0 = pl.multiple_of(b * tq, tq)
        q = q_ref[pl.ds(r0, tq), :]
        s_ctx = _nt_dot(q, kc_ref[...])
        m = jnp.maximum(jnp.max(s_ctx, axis=-1, keepdims=True), sink)
        if seq_len:
            ws = window_start(r0)
            s_loc = _nt_dot(q, kl_ref[pl.ds(ws, nk), :])
            s_loc = jnp.where(_window_mask(r0, ws, tq, nk), s_loc, NEG_INF)
            m = jnp.maximum(m, jnp.max(s_loc, axis=-1, keepdims=True))
            p_slots[slot][:, :nk] = jnp.exp2(s_loc - m).astype(BF16)
        p_slots[slot][:, nk:] = jnp.exp2(s_ctx - m).astype(BF16)
        sink_slots[slot][...] = jnp.broadcast_to(jnp.exp2(sink - m), sink_slots[slot].shape)

    def value_stage(b, slot):
        r0 = pl.multiple_of(b * tq, tq)
        acc = jnp.dot(p_slots[slot][:, nk:], vc1_ref[...], preferred_element_type=F32)
        if seq_len:
            acc = acc + jnp.dot(p_slots[slot][:, :nk], vl1_ref[pl.ds(window_start(r0), nk), :],
                                preferred_element_type=F32)
        den = acc[:, dh:dh + 1] + sink_slots[slot][:, 0:1]
        o_ref[pl.ds(r0, tq), :] = (acc[:, :dh] / den).astype(o_ref.dtype)

    _pipeline_blocks(q_ref.shape[0] // tq, depth, score_stage, value_stage)


def _attn_a(z, zc, sink, *, batch, seq_len, n_ctx, q_from_ctx, col_q, col_k, col_v):
    g = A_HEADS // A_KV_HEADS
    if q_from_ctx:
        lq, tq, qsrc, depth = n_ctx, n_ctx, zc, 1
    else:
        lq, tq, qsrc, depth = seq_len, 256, z, 8
    lat_len = 0 if q_from_ctx else seq_len
    lat_block = 16 if q_from_ctx else seq_len
    lat_tiles = seq_len // lat_block
    n_keys = n_ctx + (0 if q_from_ctx else tq + 2 * WINDOW)
    grid_spec = pltpu.PrefetchScalarGridSpec(
        num_scalar_prefetch=1,
        grid=(batch, A_HEADS),
        in_specs=[
            pl.BlockSpec((lq, LANES), lambda b, h, s: (b, col_q + h)),
            pl.BlockSpec((lat_block, LANES), lambda b, h, s: (b * lat_tiles, col_k + h // g)),
            pl.BlockSpec((lat_block, LANES), lambda b, h, s: (b * lat_tiles, col_v + h // g)),
            pl.BlockSpec((n_ctx, LANES), lambda b, h, s: (b, col_k + h // g)),
            pl.BlockSpec((n_ctx, LANES), lambda b, h, s: (b, col_v + h // g)),
        ],
        out_specs=pl.BlockSpec((lq, LANES), lambda b, h, s: (b, h)),
        scratch_shapes=[pltpu.VMEM((lat_block, 2 * LANES), BF16), pltpu.VMEM((n_ctx, 2 * LANES), BF16)]
        + [pltpu.VMEM((tq, n_keys), BF16)] * depth + [pltpu.VMEM((tq, LANES), F32)] * depth,
    )
    return pl.pallas_call(
        functools.partial(_attn_a_kernel, tq=tq, seq_len=lat_len, group=g),
        out_shape=jax.ShapeDtypeStruct((batch * lq, A_HEADS * HEAD_DIM), BF16),
        grid_spec=grid_spec,
        compiler_params=_params(("arbitrary",) * 2, 2 * lq * LANES * 2, 4 * lat_block * LANES * 2,
                                4 * n_ctx * LANES * 2, depth * tq * (n_keys + 2 * LANES)),
        name="attn_window_ctx" if q_from_ctx else "attn_window",
    )(sink, qsrc, z, z, zc, zc)


def _pipeline_blocks(n_blocks, depth, score_stage, value_stage, unroll=None):
    unroll = unroll or depth
    score_stage(0, 0)

    def body(u, carry):
        for i in range(unroll):
            b = u * unroll + i
            score_stage(jnp.minimum(b + 1, n_blocks - 1), (i + 1) % depth)
            value_stage(b, i % depth)
        return carry

    lax.fori_loop(0, n_blocks // unroll, body, 0)


def _join_rows(dst_ref, lat_ref, ctx_ref, has_lat):
    n_lat = lat_ref.shape[0] if has_lat else 0
    if has_lat:
        dst_ref[0:n_lat, 0:lat_ref.shape[1]] = lat_ref[...]
    dst_ref[n_lat:n_lat + ctx_ref.shape[0], 0:ctx_ref.shape[1]] = ctx_ref[...]


def _exp2_numerators(q, k_all):
    s = _nt_dot(q, k_all)
    return jnp.exp2(s - jnp.max(s, axis=-1, keepdims=True))


def _attn_b_kernel(q1_ref, q2_ref, k1l_ref, k2l_ref, vl_ref, k1c_ref, k2c_ref, vc_ref,
                   lq1_ref, lk1_ref, lq2_ref, lk2_ref, sub_ref, o_ref, k1_ref, k2_ref, v_ref, *p_slots,
                   tq, has_lat, lam_init):
    lam = (jnp.exp(jnp.sum(lq1_ref[0] * lk1_ref[0], axis=-1, keepdims=True))
           - jnp.exp(jnp.sum(lq2_ref[0] * lk2_ref[0], axis=-1, keepdims=True)) + lam_init)
    _join_rows(k1_ref, k1l_ref, k1c_ref, has_lat)
    _join_rows(k2_ref, k2l_ref, k2c_ref, has_lat)
    _join_rows(v_ref, vl_ref, vc_ref, has_lat)

    def score_stage(b, slot):
        r0 = pl.multiple_of(b * tq, tq)
        p1 = _exp2_numerators(q1_ref[pl.ds(r0, tq), :], k1_ref[...])
        p2 = _exp2_numerators(q2_ref[pl.ds(r0, tq), :], k2_ref[...])
        w1 = 1.0 / jnp.sum(p1, axis=-1, keepdims=True)
        w2 = lam / jnp.sum(p2, axis=-1, keepdims=True)
        p_slots[slot][...] = (p1 * w1 - p2 * w2).astype(BF16)

    def value_stage(b, slot):
        r0 = pl.multiple_of(b * tq, tq)
        o = jnp.dot(p_slots[slot][...], v_ref[...], preferred_element_type=F32)
        ms = jnp.mean(o * o, axis=-1, keepdims=True)
        o = o * lax.rsqrt(ms + EPS) * sub_ref[...] * (1.0 - lam_init)
        o_ref[pl.ds(r0, tq), :] = o.astype(o_ref.dtype)

    _pipeline_blocks(q1_ref.shape[0] // tq, len(p_slots), score_stage, value_stage)


def _attn_b(z, zc, lq1, lk1, lq2, lk2, subln, *, batch, seq_len, n_ctx, q_from_ctx, col_q, col_k, col_v,
            lam_init):
    dv = 2 * HEAD_DIM
    if q_from_ctx:
        lq, bq, tq, qsrc, depth = n_ctx, n_ctx, n_ctx, zc, 1
    else:
        lq, bq, tq, qsrc, depth = seq_len, seq_len, 256, z, 4
    nqb = lq // bq
    lat_block = 16 if q_from_ctx else seq_len
    lat_tiles = seq_len // lat_block
    n_keys = n_ctx if q_from_ctx else seq_len + n_ctx
    lam_spec = pl.BlockSpec((1, 1, HEAD_DIM), lambda b, h, i: (h, 0, 0))
    in_specs = [
        pl.BlockSpec((bq, LANES), lambda b, h, i: (b * nqb + i, col_q + 2 * h)),
        pl.BlockSpec((bq, LANES), lambda b, h, i: (b * nqb + i, col_q + 2 * h + 1)),
        pl.BlockSpec((lat_block, LANES), lambda b, h, i: (b * lat_tiles, col_k + 2 * h)),
        pl.BlockSpec((lat_block, LANES), lambda b, h, i: (b * lat_tiles, col_k + 2 * h + 1)),
        pl.BlockSpec((lat_block, dv), lambda b, h, i: (b * lat_tiles, col_v // 2 + h)),
        pl.BlockSpec((n_ctx, LANES), lambda b, h, i: (b, col_k + 2 * h)),
        pl.BlockSpec((n_ctx, LANES), lambda b, h, i: (b, col_k + 2 * h + 1)),
        pl.BlockSpec((n_ctx, dv), lambda b, h, i: (b, col_v // 2 + h)),
        lam_spec, lam_spec, lam_spec, lam_spec,
        pl.BlockSpec((1, dv), lambda b, h, i: (0, 0)),
    ]
    return pl.pallas_call(
        functools.partial(_attn_b_kernel, tq=tq, has_lat=not q_from_ctx, lam_init=lam_init),
        out_shape=jax.ShapeDtypeStruct((batch * lq, B_HEADS * dv), BF16),
        grid=(batch, B_HEADS, nqb),
        in_specs=in_specs,
        out_specs=pl.BlockSpec((bq, dv), lambda b, h, i: (b * nqb + i, h)),
        scratch_shapes=[pltpu.VMEM((n_keys, LANES), BF16), pltpu.VMEM((n_keys, LANES), BF16),
                        pltpu.VMEM((n_keys, dv), BF16)] + [pltpu.VMEM((tq, n_keys), BF16)] * depth,
        compiler_params=_params(("arbitrary",) * 3, 2 * bq * LANES * 2, 4 * lat_block * LANES * 2,
                                4 * n_ctx * LANES * 2, bq * dv * 2, (4 * LANES + depth * tq) * n_keys,
                                4 * tq * n_keys * 4 // 2),
        name="attn_diff_ctx" if q_from_ctx else "attn_diff",
    )(qsrc, qsrc, z, z, z, zc, zc, zc,
      lq1.reshape(B_HEADS, 1, HEAD_DIM), lk1.reshape(B_HEADS, 1, HEAD_DIM),
      lq2.reshape(B_HEADS, 1, HEAD_DIM), lk2.reshape(B_HEADS, 1, HEAD_DIM), subln.reshape(1, dv))


def _attn_c_kernel(q_ref, kl_ref, vl_ref, kc_ref, vc_ref, o_ref, k_ref, v1_ref, *p_slots, tq, group):
    dh = vl_ref.shape[1]

    @pl.when(pl.program_id(1) % group == 0)
    def _():
        _join_rows(k_ref, kl_ref, kc_ref, True)
        _join_rows(v1_ref, vl_ref, vc_ref, True)
        v1_ref[:, dh:] = jnp.ones((v1_ref.shape[0], v1_ref.shape[1] - dh), v1_ref.dtype)

    def score_stage(b, slot):
        r0 = pl.multiple_of(b * tq, tq)
        p_slots[slot][...] = _exp2_numerators(q_ref[pl.ds(r0, tq), :], k_ref[...]).astype(BF16)

    def value_stage(b, slot):
        r0 = pl.multiple_of(b * tq, tq)
        acc = jnp.dot(p_slots[slot][...], v1_ref[...], preferred_element_type=F32)
        o_ref[pl.ds(r0, tq), :] = (acc[:, :dh] / acc[:, dh:dh + 1]).astype(o_ref.dtype)

    n_blocks = q_ref.shape[0] // tq
    _pipeline_blocks(n_blocks, len(p_slots), score_stage, value_stage, unroll=n_blocks)


def _attn_c(z, zc, *, batch, seq_len, n_ctx, n_heads, col_k, col_v):
    g = n_heads // C_KV_HEADS
    tq, depth = 256, 8
    n_keys = seq_len + n_ctx
    return pl.pallas_call(
        functools.partial(_attn_c_kernel, tq=tq, group=g),
        out_shape=jax.ShapeDtypeStruct((batch * seq_len, n_heads * HEAD_DIM), BF16),
        grid=(batch, n_heads),
        scratch_shapes=[pltpu.VMEM((n_keys, LANES), BF16), pltpu.VMEM((n_keys, 2 * LANES), BF16)]
        + [pltpu.VMEM((tq, n_keys), BF16)] * depth,
        in_specs=[
            pl.BlockSpec((seq_len, LANES), lambda b, h: (b, h)),
            pl.BlockSpec((seq_len, LANES), lambda b, h: (b, col_k + h // g)),
            pl.BlockSpec((seq_len, LANES), lambda b, h: (b, col_v + h // g)),
            pl.BlockSpec((n_ctx, LANES), lambda b, h: (b, col_k + h // g)),
            pl.BlockSpec((n_ctx, LANES), lambda b, h: (b, col_v + h // g)),
        ],
        out_specs=pl.BlockSpec((seq_len, LANES), lambda b, h: (b, h)),
        compiler_params=_params(("arbitrary",) * 2, 4 * seq_len * LANES * 2, 2 * n_ctx * LANES * 2,
                                (3 * LANES + depth * tq) * n_keys, 3 * tq * n_keys * 4 // 2),
        name="attn_gqa",
    )(z, z, z, zc, zc)


def _outproj_kernel(*refs, n_lhs):
    lhs = refs[:n_lhs]
    w_ref, res_ref, gate_ref, o_ref = refs[n_lhs:]
    acc = None
    k0 = 0
    for a_ref in lhs:
        kw = a_ref.shape[1]
        part = jnp.dot(a_ref[...], w_ref[k0:k0 + kw, :], preferred_element_type=F32)
        acc = part if acc is None else acc + part
        k0 += kw
    o_ref[...] = res_ref[...] + gate_ref[0] * acc


def _out_proj(lhs_list, w_bf16, res2d, gate, *, tm, rows_per_mod):
    rows, n = res2d.shape
    kdim = w_bf16.shape[0]
    tn = 512
    in_specs = [pl.BlockSpec((tm, a.shape[1]), lambda i, j: (i, 0)) for a in lhs_list]
    in_specs += [
        pl.BlockSpec((kdim, tn), lambda i, j: (0, j)),
        pl.BlockSpec((tm, tn), lambda i, j: (i, j)),
        pl.BlockSpec((1, 1, tn), lambda i, j: ((i * tm) // rows_per_mod, 0, j)),
    ]
    return pl.pallas_call(
        functools.partial(_outproj_kernel, n_lhs=len(lhs_list)),
        out_shape=jax.ShapeDtypeStruct((rows, n), F32),
        grid=(rows // tm, n // tn),
        in_specs=in_specs,
        out_specs=pl.BlockSpec((tm, tn), lambda i, j: (i, j)),
        compiler_params=_params(("arbitrary", "arbitrary"), tm * kdim * 2, kdim * tn * 2, 2 * tm * tn * 4),
        name="out_proj",
    )(*lhs_list, w_bf16, res2d, gate)


def _ffn_kernel(x_ref, g_ref, sh_ref, sc_ref, gate_ref, wg_ref, wu_ref, wd_ref, o_ref, h_ref, acc_ref):
    f = pl.program_id(1)

    @pl.when(f == 0)
    def _():
        h_ref[...] = _rms_mod(x_ref[...], g_ref[...], sh_ref[0], sc_ref[0]).astype(BF16)
        acc_ref[...] = jnp.zeros_like(acc_ref)

    h = h_ref[...]
    g_ = jnp.dot(h, wg_ref[...], preferred_element_type=F32)
    u_ = jnp.dot(h, wu_ref[...], preferred_element_type=F32)
    a = (g_ * jax.nn.sigmoid(g_) * u_).astype(BF16)
    acc_ref[...] += jnp.dot(a, wd_ref[...], preferred_element_type=F32)

    @pl.when(f == pl.num_programs(1) - 1)
    def _():
        o_ref[...] = x_ref[...] + gate_ref[0] * acc_ref[...]


def _dense_ffn(x2d, norm_g, shift, scale, gate, w_in_bf16, w_down_bf16, *, tm, rows_per_mod):
    rows, d = x2d.shape
    d_ff = w_down_bf16.shape[0]
    tf = 512
    nf = d_ff // tf
    mod_spec = pl.BlockSpec((1, 1, d), lambda i, f: ((i * tm) // rows_per_mod, 0, 0))
    return pl.pallas_call(
        _ffn_kernel,
        out_shape=jax.ShapeDtypeStruct((rows, d), F32),
        grid=(rows // tm, nf),
        in_specs=[
            pl.BlockSpec((tm, d), lambda i, f: (i, 0)),
            pl.BlockSpec((1, d), lambda i, f: (0, 0)),
            mod_spec, mod_spec, mod_spec,
            pl.BlockSpec((d, tf), lambda i, f: (0, f)),
            pl.BlockSpec((d, tf), lambda i, f: (0, nf + f)),
            pl.BlockSpec((tf, d), lambda i, f: (f, 0)),
        ],
        out_specs=pl.BlockSpec((tm, d), lambda i, f: (i, 0)),
        scratch_shapes=[pltpu.VMEM((tm, d), BF16), pltpu.VMEM((tm, d), F32)],
        compiler_params=_params(("arbitrary", "arbitrary"), 2 * tm * d * 4, 3 * d * tf * 2, tm * d * 3),
        name="dense_swiglu",
    )(x2d, norm_g.reshape(1, d), shift, scale, gate, w_in_bf16, w_in_bf16, w_down_bf16)


def _split3(v):
    hi = v.astype(BF16)
    r = v - hi.astype(F32)
    mid = r.astype(BF16)
    lo = (r - mid.astype(F32)).astype(BF16)
    return hi, mid, lo


def _router_kernel(x_ref, g_ref, sh_ref, sc_ref, wr_ref, h_ref, ids_ref, gates_ref, cnt_ref, carry_ref):
    i = pl.program_id(0)
    tm = x_ref.shape[0]

    @pl.when(i == 0)
    def _():
        carry_ref[...] = jnp.zeros_like(carry_ref)

    h = _rms_mod(x_ref[...], g_ref[...], sh_ref[0], sc_ref[0])
    h_ref[...] = h
    h0, h1, h2 = _split3(h)
    w0, w1, w2 = _split3(wr_ref[...])
    dot = functools.partial(jnp.dot, preferred_element_type=F32)
    logits = (dot(h0, w0) + (dot(h0, w1) + dot(h1, w0))
              + (dot(h0, w2) + dot(h1, w1) + dot(h2, w0)))
    lane = lax.broadcasted_iota(jnp.int32, (tm, LANES), 1)
    logits = jnp.where(lane < N_EXPERTS, logits, -jnp.inf)
    v0 = jnp.max(logits, axis=-1, keepdims=True)
    i0 = jnp.min(jnp.where(logits == v0, lane, LANES), axis=-1, keepdims=True)
    rest = jnp.where(lane == i0, -jnp.inf, logits)
    v1 = jnp.max(rest, axis=-1, keepdims=True)
    i1 = jnp.min(jnp.where(rest == v1, lane, LANES), axis=-1, keepdims=True)
    e1 = jnp.exp(v1 - v0)
    g0 = 1.0 / (1.0 + e1)
    g1 = e1 / (1.0 + e1)

    sel = (lane == i0) | (lane == i1)
    row = lax.broadcasted_iota(jnp.int32, (tm, tm), 0)
    col = lax.broadcasted_iota(jnp.int32, (tm, tm), 1)
    tri = (col < row).astype(BF16)
    before = jnp.dot(tri, sel.astype(BF16), preferred_element_type=F32) + carry_ref[...]
    r0 = jnp.sum(jnp.where(lane == i0, before, 0.0), axis=-1, keepdims=True).astype(jnp.int32)
    r1 = jnp.sum(jnp.where(lane == i1, before, 0.0), axis=-1, keepdims=True).astype(jnp.int32)
    total = carry_ref[...] + jnp.sum(sel.astype(F32), axis=0, keepdims=True)
    carry_ref[...] = total

    ids_ref[...] = jnp.where(lane == 0, i0, jnp.where(lane == 1, i1, jnp.where(lane == 2, r0, r1)))
    gates_ref[...] = jnp.where(lane == 0, g0, g1)
    cnt_ref[...] = jnp.broadcast_to(total, cnt_ref.shape).astype(jnp.int32)


def _router(x2d, norm_g, shift, scale, w_router, *, tm, rows_per_mod):
    rows, d = x2d.shape
    wr = jnp.zeros((d, LANES), F32).at[:, :N_EXPERTS].set(w_router)
    mod_spec = pl.BlockSpec((1, 1, d), lambda i: ((i * tm) // rows_per_mod, 0, 0))
    return pl.pallas_call(
        _router_kernel,
        out_shape=(jax.ShapeDtypeStruct((rows, d), F32),
                   jax.ShapeDtypeStruct((rows, LANES), jnp.int32),
                   jax.ShapeDtypeStruct((rows, LANES), F32),
                   jax.ShapeDtypeStruct((8, LANES), jnp.int32)),
        grid=(rows // tm,),
        in_specs=[pl.BlockSpec((tm, d), lambda i: (i, 0)),
                  pl.BlockSpec((1, d), lambda i: (0, 0)),
                  mod_spec, mod_spec,
                  pl.BlockSpec((d, LANES), lambda i: (0, 0))],
        out_specs=(pl.BlockSpec((tm, d), lambda i: (i, 0)),
                   pl.BlockSpec((tm, LANES), lambda i: (i, 0)),
                   pl.BlockSpec((tm, LANES), lambda i: (i, 0)),
                   pl.BlockSpec((8, LANES), lambda i: (0, 0))),
        scratch_shapes=[pltpu.VMEM((1, LANES), F32)],
        compiler_params=_params(("arbitrary",), 2 * tm * d * 4, d * LANES * 4, 2 * tm * LANES * 4),
        name="router",
    )(x2d, norm_g.reshape(1, d), shift, scale, wr)


GATHER_UNROLL = 8


def _start_row_gather(src_hbm, idx_ref, base, dst_ref, sem):
    def body(r8, c):
        for u in range(GATHER_UNROLL):
            r = r8 * GATHER_UNROLL + u
            pltpu.make_async_copy(src_hbm.at[pl.ds(idx_ref[base + r], 1), :],
                                  dst_ref.at[pl.ds(r, 1), :], sem).start()
        return c

    lax.fori_loop(0, dst_ref.shape[0] // GATHER_UNROLL, body, 0)


def _wait_row_gather(src_hbm, dst_ref, sem):
    pltpu.make_async_copy(src_hbm.at[pl.ds(0, dst_ref.shape[0]), :], dst_ref, sem).wait()


def _dispatch_kernel(idx_ref, src_hbm, o_ref, buf_ref, sem):
    tr = o_ref.shape[0]
    i, n = pl.program_id(0), pl.num_programs(0)
    slot = i & 1

    @pl.when(i == 0)
    def _():
        _start_row_gather(src_hbm, idx_ref, 0, buf_ref.at[0], sem.at[0])

    @pl.when(i + 1 < n)
    def _():
        _start_row_gather(src_hbm, idx_ref, (i + 1) * tr, buf_ref.at[1 - slot], sem.at[1 - slot])

    _wait_row_gather(src_hbm, buf_ref.at[slot], sem.at[slot])
    o_ref[...] = buf_ref[slot].astype(o_ref.dtype)


def _dispatch(h2d, row_tok, *, tr):
    n_rows = row_tok.shape[0]
    d = h2d.shape[1]
    grid_spec = pltpu.PrefetchScalarGridSpec(
        num_scalar_prefetch=1,
        grid=(n_rows // tr,),
        in_specs=[pl.BlockSpec(memory_space=pl.ANY)],
        out_specs=pl.BlockSpec((tr, d), lambda i, idx: (i, 0)),
        scratch_shapes=[pltpu.VMEM((2, tr, d), F32), pltpu.SemaphoreType.DMA((2,))],
    )
    return pl.pallas_call(
        _dispatch_kernel,
        out_shape=jax.ShapeDtypeStruct((n_rows, d), BF16),
        grid_spec=grid_spec,
        compiler_params=_params(("arbitrary",), tr * d * 2, tr * d * 4),
        name="expert_dispatch",
    )(row_tok, h2d)


def _combine_kernel(d0_ref, d1_ref, ys_hbm, x_ref, gate_ref, rg_ref, o_ref, a_ref, b_ref, sem):
    tt = x_ref.shape[0]
    i, n = pl.program_id(0), pl.num_programs(0)
    slot = i & 1

    def start(tile, s):
        _start_row_gather(ys_hbm, d0_ref, tile * tt, a_ref.at[s], sem.at[s, 0])
        _start_row_gather(ys_hbm, d1_ref, tile * tt, b_ref.at[s], sem.at[s, 1])

    @pl.when(i == 0)
    def _():
        start(0, 0)

    @pl.when(i + 1 < n)
    def _():
        start(i + 1, 1 - slot)

    _wait_row_gather(ys_hbm, a_ref.at[slot], sem.at[slot, 0])
    _wait_row_gather(ys_hbm, b_ref.at[slot], sem.at[slot, 1])
    rg = rg_ref[...]
    mix = rg[:, 0:1] * a_ref[slot] + rg[:, 1:2] * b_ref[slot]
    o_ref[...] = x_ref[...] + gate_ref[0] * mix


def _combine(ys, dest0, dest1, x2d, gate, row_gates, *, tt, rows_per_mod):
    rows, d = x2d.shape
    grid_spec = pltpu.PrefetchScalarGridSpec(
        num_scalar_prefetch=2,
        grid=(rows // tt,),
        in_specs=[pl.BlockSpec(memory_space=pl.ANY),
                  pl.BlockSpec((tt, d), lambda i, a, b: (i, 0)),
                  pl.BlockSpec((1, 1, d), lambda i, a, b: ((i * tt) // rows_per_mod, 0, 0)),
                  pl.BlockSpec((tt, LANES), lambda i, a, b: (i, 0))],
        out_specs=pl.BlockSpec((tt, d), lambda i, a, b: (i, 0)),
        scratch_shapes=[pltpu.VMEM((2, tt, d), F32), pltpu.VMEM((2, tt, d), F32),
                        pltpu.SemaphoreType.DMA((2, 2))],
    )
    return pl.pallas_call(
        _combine_kernel,
        out_shape=jax.ShapeDtypeStruct((rows, d), F32),
        grid_spec=grid_spec,
        compiler_params=_params(("arbitrary",), 2 * tt * d * 4, 2 * tt * d * 4),
        name="expert_combine",
    )(dest0, dest1, ys, x2d, gate, row_gates)


WEIGHT_DMA_PRIORITY = 1


def _stream_group_weights(sched, copies, cast):
    te_ref, first_ref, grp_ref, ngrp_ref, nxt_ref = sched
    j, r = pl.program_id(0), pl.program_id(1)
    nj = pl.num_programs(0)

    @pl.when(first_ref[r] == 1)
    def _():
        g = grp_ref[r]
        ng = ngrp_ref[0]

        @pl.when((j == 0) & (g == 0))
        def _():
            for cp in copies(te_ref[r], j):
                cp.start(priority=WEIGHT_DMA_PRIORITY)

        for cp in copies(te_ref[r], j):
            cp.wait()
        cast()
        last = g == ng - 1

        @pl.when(jnp.logical_not(last & (j == nj - 1)))
        def _():
            for cp in copies(nxt_ref[r], jnp.where(last, j + 1, j)):
                cp.start(priority=WEIGHT_DMA_PRIORITY)


CAST_ROWS = 256


def _cast_rows(src_ref, dst_ref):
    def body(i, carry):
        rows = pl.ds(pl.multiple_of(i * CAST_ROWS, CAST_ROWS), CAST_ROWS)
        dst_ref[rows, :] = src_ref[rows, :].astype(dst_ref.dtype)
        return carry

    lax.fori_loop(0, src_ref.shape[0] // CAST_ROWS, body, 0)


def _gmm_up_kernel(te_ref, first_ref, grp_ref, ngrp_ref, nxt_ref, nv_ref, x_ref, w_hbm, o_ref,
                   stage_ref, wbf_ref, sem):
    tn = o_ref.shape[1]
    nj = pl.num_programs(0)

    def copies(e, j):
        return [pltpu.make_async_copy(w_hbm.at[e, :, pl.ds(pl.multiple_of((half * nj + j) * tn, tn), tn)],
                                      stage_ref.at[half], sem.at[half]) for half in range(2)]

    def cast():
        _cast_rows(stage_ref.at[0], wbf_ref.at[0])
        _cast_rows(stage_ref.at[1], wbf_ref.at[1])

    _stream_group_weights((te_ref, first_ref, grp_ref, ngrp_ref, nxt_ref), copies, cast)
    valid = pl.program_id(1) < nv_ref[0]

    @pl.when(valid)
    def _():
        x = x_ref[...]
        g_ = jnp.dot(x, wbf_ref[0], preferred_element_type=F32)
        u_ = jnp.dot(x, wbf_ref[1], preferred_element_type=F32)
        o_ref[...] = (g_ * jax.nn.sigmoid(g_) * u_).astype(o_ref.dtype)

    @pl.when(jnp.logical_not(valid))
    def _():
        o_ref[...] = jnp.zeros_like(o_ref)


def _gmm_down_kernel(te_ref, first_ref, grp_ref, ngrp_ref, nxt_ref, nv_ref, h_ref, w_hbm, o_ref,
                     stage_ref, wbf_ref, sem):
    tn = o_ref.shape[1]

    def copies(e, j):
        return [pltpu.make_async_copy(w_hbm.at[e, :, pl.ds(pl.multiple_of(j * tn, tn), tn)], stage_ref, sem)]

    def cast():
        _cast_rows(stage_ref, wbf_ref)

    _stream_group_weights((te_ref, first_ref, grp_ref, ngrp_ref, nxt_ref), copies, cast)
    valid = pl.program_id(1) < nv_ref[0]

    @pl.when(valid)
    def _():
        o_ref[...] = jnp.dot(h_ref[...], wbf_ref[...], preferred_element_type=F32)

    @pl.when(jnp.logical_not(valid))
    def _():
        o_ref[...] = jnp.zeros_like(o_ref)


def _expert_ffn(xs, w_in, w_down, sched, n_valid, *, tr):
    n_rows, d = xs.shape
    d_ff = w_down.shape[1]
    n_tiles = n_rows // tr
    n_sched = len(sched) + 1
    tn1 = d_ff // 4
    nj1 = d_ff // tn1
    small_temps = 4 * 1024 * 1024

    def row_map(*args):
        r, nv = args[1], args[-1]
        return (jnp.minimum(r, nv[0] - 1), 0)

    def out_map(*args):
        return (args[1], args[0])

    up_spec = pltpu.PrefetchScalarGridSpec(
        num_scalar_prefetch=n_sched,
        grid=(nj1, n_tiles),
        in_specs=[pl.BlockSpec((tr, d), row_map), pl.BlockSpec(memory_space=pl.ANY)],
        out_specs=pl.BlockSpec((tr, tn1), out_map),
        scratch_shapes=[pltpu.VMEM((2, d, tn1), F32), pltpu.VMEM((2, d, tn1), BF16),
                        pltpu.SemaphoreType.DMA((2,))],
    )
    stage1 = 2 * d * tn1 * 4
    hidden = pl.pallas_call(
        _gmm_up_kernel,
        out_shape=jax.ShapeDtypeStruct((n_rows, d_ff), BF16),
        grid_spec=up_spec,
        compiler_params=pltpu.CompilerParams(
            dimension_semantics=("arbitrary", "arbitrary"),
            vmem_limit_bytes=stage1 + 2 * d * tn1 * 2 + 2 * (tr * d * 2 + tr * tn1 * 2) + 2 * small_temps),
        name="expert_up",
    )(*sched, n_valid, xs, w_in)

    tn2 = d // 2
    down_spec = pltpu.PrefetchScalarGridSpec(
        num_scalar_prefetch=n_sched,
        grid=(d // tn2, n_tiles),
        in_specs=[pl.BlockSpec((tr, d_ff), row_map), pl.BlockSpec(memory_space=pl.ANY)],
        out_specs=pl.BlockSpec((tr, tn2), out_map),
        scratch_shapes=[pltpu.VMEM((d_ff, tn2), F32), pltpu.VMEM((d_ff, tn2), BF16),
                        pltpu.SemaphoreType.DMA(())],
    )
    stage2 = d_ff * tn2 * 4
    return pl.pallas_call(
        _gmm_down_kernel,
        out_shape=jax.ShapeDtypeStruct((n_rows, d), F32),
        grid_spec=down_spec,
        compiler_params=pltpu.CompilerParams(
            dimension_semantics=("arbitrary", "arbitrary"),
            vmem_limit_bytes=stage2 + d_ff * tn2 * 2 + 2 * (tr * d_ff * 2 + tr * tn2 * 4) + small_temps),
        name="expert_down",
    )(*sched, n_valid, hidden, w_down)


def _moe(x2d, norm_g, shift, scale, gate, w_router, w_in, w_down, *, rows_per_mod):
    rows, d = x2d.shape
    tr = 256
    h2d, ids, gates, counts = _router(x2d, norm_g, shift, scale, w_router, tm=512, rows_per_mod=rows_per_mod)
    counts = counts[0, :N_EXPERTS]
    padded = (counts + tr - 1) // tr * tr
    pad_end = jnp.cumsum(padded)
    pad_start = pad_end - padded
    dest0 = pad_start[ids[:, 0]] + ids[:, 2]
    dest1 = pad_start[ids[:, 1]] + ids[:, 3]
    n_rows = 2 * rows + N_EXPERTS * tr
    n_tiles = n_rows // tr
    tok = jnp.arange(rows, dtype=jnp.int32)
    row_tok = jnp.zeros((n_rows,), jnp.int32).at[jnp.concatenate([dest0, dest1])].set(
        jnp.concatenate([tok, tok]))
    tile_expert = jnp.minimum(
        jnp.sum(jnp.arange(n_tiles, dtype=jnp.int32)[:, None] * tr >= pad_end[None, :], axis=1),
        N_EXPERTS - 1).astype(jnp.int32)
    n_valid = (pad_end[-1:] // tr).astype(jnp.int32)
    tile_id = jnp.arange(n_tiles, dtype=jnp.int32)
    prev_expert = jnp.concatenate([jnp.full((1,), -1, jnp.int32), tile_expert[:-1]])
    first = ((tile_id < n_valid[0]) & (tile_expert != prev_expert)).astype(jnp.int32)
    grp = jnp.cumsum(first).astype(jnp.int32) - 1
    n_grp = jnp.sum(first, keepdims=True).astype(jnp.int32)
    expert_id = jnp.arange(N_EXPERTS, dtype=jnp.int32)
    grp_expert = jnp.sort(jnp.where(counts > 0, expert_id, N_EXPERTS))
    nxt = grp_expert[(grp + 1) % n_grp[0]].astype(jnp.int32)
    sched = (tile_expert, first, grp, n_grp, nxt)

    xs = _dispatch(h2d, row_tok, tr=tr)
    ys = _expert_ffn(xs, w_in, w_down, sched, n_valid, tr=tr)
    return _combine(ys, dest0, dest1, x2d, gate, gates, tt=256, rows_per_mod=rows_per_mod)


def _rope_tables(seq_len):
    t = np.arange(seq_len)
    row = (t // GRID_W).astype(np.float64)[:, None]
    col = (t % GRID_W).astype(np.float64)[:, None]
    inv = ROPE_THETA ** (-np.arange(0, ROT_AXIS, 2, dtype=np.float64) / ROT_AXIS)
    ar, ac = row * inv, col * inv
    zero = np.zeros_like(ar)
    cos = np.concatenate([np.cos(ar), np.cos(ar), np.cos(ac), np.cos(ac)], axis=-1)
    s_next = np.concatenate([-np.sin(ar), zero, -np.sin(ac), zero], axis=-1)
    s_prev = np.concatenate([zero, np.sin(ar), zero, np.sin(ac)], axis=-1)
    return tuple(jnp.asarray(a.astype(np.float32)) for a in (cos, s_next, s_prev))


def _mod_vectors(c, c_ctx, w_mod, b_mod):
    batch, d = c.shape
    cvec = jnp.zeros((8, d), F32).at[:batch].set(c).at[batch].set(c_ctx)
    m = _modulation(jnp.concatenate([cvec, cvec], axis=0), w_mod, b_mod)
    lat = m[:batch].reshape(batch, N_MOD, 1, d)
    ctx = m[batch].reshape(N_MOD, 1, 1, d)
    return [lat[:, k] for k in range(N_MOD)], [ctx[k] for k in range(N_MOD)]


def kernel(x, c, ctx, c_ctx, e_norm1, e_norm2, e_w_mod, e_b_mod, e_w_in, e_w_out, e_a_qnorm, e_a_knorm,
           e_a_sink, e_b_qnorm, e_b_knorm, e_b_lam_q1, e_b_lam_k1, e_b_lam_q2, e_b_lam_k2, e_b_subln,
           e_ffn_w_in, e_ffn_w_down, o_norm1, o_norm2, o_w_mod, o_b_mod, o_w_in, o_w_out, o_c_qnorm,
           o_c_knorm, o_router, o_exp_w_in, o_exp_w_down):
    batch, seq_len, d = x.shape
    n_ctx = ctx.shape[1]
    qk_scale = HEAD_DIM ** -0.5 * LOG2_E
    tables = _rope_tables(seq_len)
    ctx_tables = tuple(t[:batch * n_ctx] for t in tables)
    x2d = x.reshape(batch * seq_len, d)
    xc2d = ctx.reshape(batch * n_ctx, d)
    tm = 1024

    (sh1, sc1, g1, sh2, sc2, g2), (csh1, csc1, cg1, csh2, csc2, cg2) = _mod_vectors(
        c, c_ctx, e_w_mod[0], e_b_mod[0])
    a_q, a_kv = A_HEADS * HEAD_DIM, A_KV_HEADS * HEAD_DIM
    b_qk, b_v = B_HEADS * 2 * HEAD_DIM, B_HEADS * 2 * HEAD_DIM
    ones = lambda n: jnp.ones((n,), F32)
    tile = lambda v, n: jnp.tile(v, n // HEAD_DIM)
    col_gain = jnp.concatenate([tile(e_a_qnorm[0] * qk_scale, a_q), tile(e_a_knorm[0], a_kv), ones(a_kv),
                                tile(e_b_qnorm[0] * qk_scale, b_qk), tile(e_b_knorm[0], b_qk), ones(b_v)])
    kinds = lambda *pairs: jnp.concatenate([jnp.full((n // HEAD_DIM,), k, jnp.int32) for k, n in pairs])
    col_kind = kinds((1, a_q), (1, a_kv), (0, a_kv), (1, b_qk), (1, b_qk), (0, b_v))
    w_in0 = e_w_in[0].astype(BF16)
    z = _in_proj(x2d, e_norm1[0], sh1, sc1, w_in0, col_gain, col_kind, tables,
                 tm=tm, rows_per_mod=seq_len, rope=True)
    zc = _in_proj(xc2d, e_norm1[0], csh1, csc1, w_in0, col_gain, col_kind, ctx_tables,
                  tm=batch * n_ctx, rows_per_mod=batch * n_ctx, rope=False)

    ca_q, ca_k, ca_v = 0, a_q // LANES, (a_q + a_kv) // LANES
    cb_q = (a_q + 2 * a_kv) // LANES
    cb_k, cb_v = cb_q + b_qk // LANES, cb_q + 2 * b_qk // LANES
    lam_init = 0.8 - 0.6 * math.exp(-0.3 * 0)
    dims = dict(batch=batch, seq_len=seq_len, n_ctx=n_ctx)
    attn_a = functools.partial(_attn_a, z, zc, e_a_sink[0], col_q=ca_q, col_k=ca_k, col_v=ca_v, **dims)
    attn_b = functools.partial(_attn_b, z, zc, e_b_lam_q1[0], e_b_lam_k1[0], e_b_lam_q2[0], e_b_lam_k2[0],
                               e_b_subln[0], col_q=cb_q, col_k=cb_k, col_v=cb_v, lam_init=lam_init, **dims)
    w_out0 = e_w_out[0].astype(BF16)
    x2d = _out_proj([attn_a(q_from_ctx=False), attn_b(q_from_ctx=False)], w_out0, x2d, g1,
                    tm=tm, rows_per_mod=seq_len)
    xc2d = _out_proj([attn_a(q_from_ctx=True), attn_b(q_from_ctx=True)], w_out0, xc2d, cg1,
                     tm=batch * n_ctx, rows_per_mod=batch * n_ctx)
    ffn_in, ffn_down = e_ffn_w_in[0].astype(BF16), e_ffn_w_down[0].astype(BF16)
    x2d = _dense_ffn(x2d, e_norm2[0], sh2, sc2, g2, ffn_in, ffn_down, tm=512, rows_per_mod=seq_len)
    xc2d = _dense_ffn(xc2d, e_norm2[0], csh2, csc2, cg2, ffn_in, ffn_down, tm=batch * n_ctx,
                      rows_per_mod=batch * n_ctx)

    (sh1, sc1, g1, sh2, sc2, g2), (csh1, csc1, _, _, _, _) = _mod_vectors(c, c_ctx, o_w_mod[0], o_b_mod[0])
    c_q = d
    c_kv = C_KV_HEADS * HEAD_DIM
    col_gain = jnp.concatenate([tile(o_c_qnorm[0] * qk_scale, c_q), tile(o_c_knorm[0], c_kv), ones(c_kv)])
    col_kind = kinds((1, c_q), (1, c_kv), (0, c_kv))
    w_in1 = o_w_in[0].astype(BF16)
    z = _in_proj(x2d, o_norm1[0], sh1, sc1, w_in1, col_gain, col_kind, tables,
                 tm=tm, rows_per_mod=seq_len, rope=True)
    zc = _in_proj(xc2d, o_norm1[0], csh1, csc1, w_in1, col_gain, col_kind, ctx_tables,
                  tm=batch * n_ctx, rows_per_mod=batch * n_ctx, rope=False)
    o = _attn_c(z, zc, n_heads=c_q // HEAD_DIM, col_k=c_q // LANES, col_v=(c_q + c_kv) // LANES, **dims)
    x2d = _out_proj([o], o_w_out[0].astype(BF16), x2d, g1, tm=tm, rows_per_mod=seq_len)
    x2d = _moe(x2d, o_norm2[0], sh2, sc2, g2, o_router[0], o_exp_w_in.reshape(o_exp_w_in.shape[1:]),
               o_exp_w_down.reshape(o_exp_w_down.shape[1:]), rows_per_mod=seq_len)
    return x2d.reshape(batch, seq_len, d)
```

```python
import functools
import math

import jax
import jax.numpy as jnp
import numpy as np
from jax import lax
from jax.experimental import pallas as pl
from jax.experimental.pallas import tpu as pltpu

F32 = jnp.float32
BF16 = jnp.bfloat16

HEAD_DIM = 128
GRID_W = 64
ROT_AXIS = HEAD_DIM // 2
ROPE_THETA = 10000.0
EPS = 1e-6
NEG_INF = -1e30
LOG2_E = math.log2(math.e)
N_MOD = 6
WINDOW = 128
A_HEADS, A_KV_HEADS = 8, 2
B_HEADS = 4
C_KV_HEADS = 4
N_EXPERTS = 8
LANES = 128
V7X_VMEM_BYTES = 64 * 1024 * 1024
VMEM_TEMP_ALLOWANCE = 16 * 1024 * 1024


def _vmem_limit(*block_bytes):
    need = 2 * sum(block_bytes) + VMEM_TEMP_ALLOWANCE
    return int(min(need, V7X_VMEM_BYTES - 6 * 1024 * 1024))


def _params(sem, *block_bytes):
    return pltpu.CompilerParams(dimension_semantics=sem, vmem_limit_bytes=_vmem_limit(*block_bytes))


def _nt_dot(a, b):
    return lax.dot_general(a, b, (((1,), (1,)), ((), ())), preferred_element_type=F32)


def _rms_mod(x, g, shift, scale):
    ms = jnp.mean(x * x, axis=-1, keepdims=True)
    y = x * lax.rsqrt(ms + EPS) * g
    return y * (1.0 + scale) + shift


def _mod_kernel(c_ref, w_ref, b_ref, o_ref):
    c = c_ref[...]
    s = c * jax.nn.sigmoid(c)
    s_hi = s.astype(BF16).astype(F32)
    top = lax.broadcasted_iota(jnp.int32, s.shape, 0) < 8
    lhs = jnp.where(top, s_hi, s - s_hi).astype(BF16)
    acc = jnp.dot(lhs, w_ref[...].astype(BF16), preferred_element_type=F32)
    o_ref[...] = acc[0:8] + acc[8:16] + b_ref[...]


def _modulation(cvec, w_mod, b_mod):
    d, n = w_mod.shape
    tn = 1024
    return pl.pallas_call(
        _mod_kernel,
        out_shape=jax.ShapeDtypeStruct((8, n), F32),
        grid=(n // tn,),
        in_specs=[pl.BlockSpec((16, d), lambda j: (0, 0)),
                  pl.BlockSpec((d, tn), lambda j: (0, j)),
                  pl.BlockSpec((1, tn), lambda j: (0, j))],
        out_specs=pl.BlockSpec((8, tn), lambda j: (0, j)),
        compiler_params=_params(("arbitrary",), d * tn * 4, d * tn * 2),
        name="modulation",
    )(cvec, w_mod, b_mod.reshape(1, n))


def _inproj_kernel(kind_ref, x_ref, g_ref, sh_ref, sc_ref, w_ref, gc_ref, cos_ref, sa_ref, sb_ref,
                   o_ref, h_ref, *, rope):
    j = pl.program_id(1)
    tn = o_ref.shape[1]
    nch = tn // LANES

    @pl.when(j == 0)
    def _():
        h_ref[...] = _rms_mod(x_ref[...], g_ref[...], sh_ref[0], sc_ref[0]).astype(BF16)

    acc = jnp.dot(h_ref[...], w_ref[...], preferred_element_type=F32)
    for c in range(nch):
        a = acc[:, c * LANES:(c + 1) * LANES]
        kind = kind_ref[j * nch + c]

        @pl.when(kind == 0)
        def _():
            o_ref[:, c * LANES:(c + 1) * LANES] = a.astype(o_ref.dtype)

        @pl.when(kind == 1)
        def _():
            ms = jnp.mean(a * a, axis=-1, keepdims=True)
            y = a * lax.rsqrt(ms + EPS) * gc_ref[:, c * LANES:(c + 1) * LANES]
            if rope:
                y = (y * cos_ref[...] + pltpu.roll(y, LANES - 32, 1) * sa_ref[...]
                     + pltpu.roll(y, 32, 1) * sb_ref[...])
            o_ref[:, c * LANES:(c + 1) * LANES] = y.astype(o_ref.dtype)


def _in_proj(x2d, norm_g, shift, scale, w_bf16, col_gain, col_kind, tables, *, tm, rows_per_mod, rope):
    rows, d = x2d.shape
    n = w_bf16.shape[1]
    tn = 512
    cos_t, sa_t, sb_t = tables
    pos_tiles = cos_t.shape[0] // tm

    grid_spec = pltpu.PrefetchScalarGridSpec(
        num_scalar_prefetch=1,
        grid=(rows // tm, n // tn),
        in_specs=[
            pl.BlockSpec((tm, d), lambda i, j, k: (i, 0)),
            pl.BlockSpec((1, d), lambda i, j, k: (0, 0)),
            pl.BlockSpec((1, 1, d), lambda i, j, k: ((i * tm) // rows_per_mod, 0, 0)),
            pl.BlockSpec((1, 1, d), lambda i, j, k: ((i * tm) // rows_per_mod, 0, 0)),
            pl.BlockSpec((d, tn), lambda i, j, k: (0, j)),
            pl.BlockSpec((1, tn), lambda i, j, k: (0, j)),
            pl.BlockSpec((tm, LANES), lambda i, j, k: (i % pos_tiles, 0)),
            pl.BlockSpec((tm, LANES), lambda i, j, k: (i % pos_tiles, 0)),
            pl.BlockSpec((tm, LANES), lambda i, j, k: (i % pos_tiles, 0)),
        ],
        out_specs=pl.BlockSpec((tm, tn), lambda i, j, k: (i, j)),
        scratch_shapes=[pltpu.VMEM((tm, d), BF16)],
    )
    return pl.pallas_call(
        functools.partial(_inproj_kernel, rope=rope),
        out_shape=jax.ShapeDtypeStruct((rows, n), BF16),
        grid_spec=grid_spec,
        compiler_params=_params(("arbitrary", "arbitrary"), tm * d * 4, d * tn * 2, tm * tn * 2,
                                3 * tm * LANES * 4, tm * d),
        name="in_proj_rope" if rope else "in_proj_ctx",
    )(col_kind, x2d, norm_g.reshape(1, d), shift, scale, w_bf16, col_gain.reshape(1, n), cos_t, sa_t, sb_t)


def _window_mask(q0, ws, tq, nk):
    qpos = q0 + lax.broadcasted_iota(jnp.int32, (tq, nk), 0)
    kpos = ws + lax.broadcasted_iota(jnp.int32, (tq, nk), 1)
    return jnp.abs(qpos - kpos) <= WINDOW


def _attn_a_kernel(sink_ref, q_ref, kl_ref, vl_ref, kc_ref, vc_ref, o_ref, vl1_ref, vc1_ref, *slots,
                   tq, seq_len, group):
    depth = len(slots) // 2
    p_slots, sink_slots = slots[:depth], slots[depth:]
    sink = sink_ref[pl.program_id(1)] * LOG2_E
    dh = vc_ref.shape[1]
    nk = tq + 2 * WINDOW if seq_len else 0

    @pl.when(pl.program_id(1) % group == 0)
    def _():
        if seq_len:
            vl1_ref[:, :dh] = vl_ref[...]
            vl1_ref[:, dh:] = jnp.ones_like(vl_ref)
        vc1_ref[:, :dh] = vc_ref[...]
        vc1_ref[:, dh:] = jnp.ones_like(vc_ref)

    def window_start(r0):
        return pl.multiple_of(jnp.clip(r0 - WINDOW, 0, seq_len - nk), WINDOW)

    def score_stage(b, slot):
        r0 = pl.multiple_of(b * tq, tq)
        q = q_ref[pl.ds(r0, tq), :]
        s_ctx = _nt_dot(q, kc_ref[...])
        m = jnp.maximum(jnp.max(s_ctx, axis=-1, keepdims=True), sink)
        if seq_len:
            ws = window_start(r0)
            s_loc = _nt_dot(q, kl_ref[pl.ds(ws, nk), :])
            s_loc = jnp.where(_window_mask(r0, ws, tq, nk), s_loc, NEG_INF)
            m = jnp.maximum(m, jnp.max(s_loc, axis=-1, keepdims=True))
            p_slots[slot][:, :nk] = jnp.exp2(s_loc - m).astype(BF16)
        p_slots[slot][:, nk:] = jnp.exp2(s_ctx - m).astype(BF16)
        sink_slots[slot][...] = jnp.broadcast_to(jnp.exp2(sink - m), sink_slots[slot].shape)

    def value_stage(b, slot):
        r0 = pl.multiple_of(b * tq, tq)
        acc = jnp.dot(p_slots[slot][:, nk:], vc1_ref[...], preferred_element_type=F32)
        if seq_len:
            acc = acc + jnp.dot(p_slots[slot][:, :nk], vl1_ref[pl.ds(window_start(r0), nk), :],
                                preferred_element_type=F32)
        den = acc[:, dh:dh + 1] + sink_slots[slot][:, 0:1]
        o_ref[pl.ds(r0, tq), :] = (acc[:, :dh] / den).astype(o_ref.dtype)

    _pipeline_blocks(q_ref.shape[0] // tq, depth, score_stage, value_stage)


def _attn_a(z, zc, sink, *, batch, seq_len, n_ctx, q_from_ctx, col_q, col_k, col_v):
    g = A_HEADS // A_KV_HEADS
    if q_from_ctx:
        lq, tq, qsrc, depth = n_ctx, n_ctx, zc, 1
    else:
        lq, tq, qsrc, depth = seq_len, 256, z, 8
    lat_len = 0 if q_from_ctx else seq_len
    lat_block = 16 if q_from_ctx else seq_len
    lat_tiles = seq_len // lat_block
    n_keys = n_ctx + (0 if q_from_ctx else tq + 2 * WINDOW)
    grid_spec = pltpu.PrefetchScalarGridSpec(
        num_scalar_prefetch=1,
        grid=(batch, A_HEADS),
        in_specs=[
            pl.BlockSpec((lq, LANES), lambda b, h, s: (b, col_q + h)),
            pl.BlockSpec((lat_block, LANES), lambda b, h, s: (b * lat_tiles, col_k + h // g)),
            pl.BlockSpec((lat_block, LANES), lambda b, h, s: (b * lat_tiles, col_v + h // g)),
            pl.BlockSpec((n_ctx, LANES), lambda b, h, s: (b, col_k + h // g)),
            pl.BlockSpec((n_ctx, LANES), lambda b, h, s: (b, col_v + h // g)),
        ],
        out_specs=pl.BlockSpec((lq, LANES), lambda b, h, s: (b, h)),
        scratch_shapes=[pltpu.VMEM((lat_block, 2 * LANES), BF16), pltpu.VMEM((n_ctx, 2 * LANES), BF16)]
        + [pltpu.VMEM((tq, n_keys), BF16)] * depth + [pltpu.VMEM((tq, LANES), F32)] * depth,
    )
    return pl.pallas_call(
        functools.partial(_attn_a_kernel, tq=tq, seq_len=lat_len, group=g),
        out_shape=jax.ShapeDtypeStruct((batch * lq, A_HEADS * HEAD_DIM), BF16),
        grid_spec=grid_spec,
        compiler_params=_params(("arbitrary",) * 2, 2 * lq * LANES * 2, 4 * lat_block * LANES * 2,
                                4 * n_ctx * LANES * 2, depth * tq * (n_keys + 2 * LANES)),
        name="attn_window_ctx" if q_from_ctx else "attn_window",
    )(sink, qsrc, z, z, zc, zc)


def _pipeline_blocks(n_blocks, depth, score_stage, value_stage, unroll=None):
    unroll = unroll or depth
    score_stage(0, 0)

    def body(u, carry):
        for i in range(unroll):
            b = u * unroll + i
            score_stage(jnp.minimum(b + 1, n_blocks - 1), (i + 1) % depth)
            value_stage(b, i % depth)
        return carry

    lax.fori_loop(0, n_blocks // unroll, body, 0)


def _join_rows(dst_ref, lat_ref, ctx_ref, has_lat):
    n_lat = lat_ref.shape[0] if has_lat else 0
    if has_lat:
        dst_ref[0:n_lat, 0:lat_ref.shape[1]] = lat_ref[...]
    dst_ref[n_lat:n_lat + ctx_ref.shape[0], 0:ctx_ref.shape[1]] = ctx_ref[...]


def _exp2_numerators(q, k_all):
    s = _nt_dot(q, k_all)
    return jnp.exp2(s - jnp.max(s, axis=-1, keepdims=True))


def _attn_b_kernel(q1_ref, q2_ref, k1l_ref, k2l_ref, vl_ref, k1c_ref, k2c_ref, vc_ref,
                   lq1_ref, lk1_ref, lq2_ref, lk2_ref, sub_ref, o_ref, k1_ref, k2_ref, v_ref, *p_slots,
                   tq, has_lat, lam_init):
    lam = (jnp.exp(jnp.sum(lq1_ref[0] * lk1_ref[0], axis=-1, keepdims=True))
           - jnp.exp(jnp.sum(lq2_ref[0] * lk2_ref[0], axis=-1, keepdims=True)) + lam_init)
    _join_rows(k1_ref, k1l_ref, k1c_ref, has_lat)
    _join_rows(k2_ref, k2l_ref, k2c_ref, has_lat)
    _join_rows(v_ref, vl_ref, vc_ref, has_lat)

    def score_stage(b, slot):
        r0 = pl.multiple_of(b * tq, tq)
        p1 = _exp2_numerators(q1_ref[pl.ds(r0, tq), :], k1_ref[...])
        p2 = _exp2_numerators(q2_ref[pl.ds(r0, tq), :], k2_ref[...])
        w1 = 1.0 / jnp.sum(p1, axis=-1, keepdims=True)
        w2 = lam / jnp.sum(p2, axis=-1, keepdims=True)
        p_slots[slot][...] = (p1 * w1 - p2 * w2).astype(BF16)

    def value_stage(b, slot):
        r0 = pl.multiple_of(b * tq, tq)
        o = jnp.dot(p_slots[slot][...], v_ref[...], preferred_element_type=F32)
        ms = jnp.mean(o * o, axis=-1, keepdims=True)
        o = o * lax.rsqrt(ms + EPS) * sub_ref[...] * (1.0 - lam_init)
        o_ref[pl.ds(r0, tq), :] = o.astype(o_ref.dtype)

    _pipeline_blocks(q1_ref.shape[0] // tq, len(p_slots), score_stage, value_stage)


def _attn_b(z, zc, lq1, lk1, lq2, lk2, subln, *, batch, seq_len, n_ctx, q_from_ctx, col_q, col_k, col_v,
            lam_init):
    dv = 2 * HEAD_DIM
    if q_from_ctx:
        lq, bq, tq, qsrc, depth = n_ctx, n_ctx, n_ctx, zc, 1
    else:
        lq, bq, tq, qsrc, depth = seq_len, seq_len, 256, z, 4
    nqb = lq // bq
    lat_block = 16 if q_from_ctx else seq_len
    lat_tiles = seq_len // lat_block
    n_keys = n_ctx if q_from_ctx else seq_len + n_ctx
    lam_spec = pl.BlockSpec((1, 1, HEAD_DIM), lambda b, h, i: (h, 0, 0))
    in_specs = [
        pl.BlockSpec((bq, LANES), lambda b, h, i: (b * nqb + i, col_q + 2 * h)),
        pl.BlockSpec((bq, LANES), lambda b, h, i: (b * nqb + i, col_q + 2 * h + 1)),
        pl.BlockSpec((lat_block, LANES), lambda b, h, i: (b * lat_tiles, col_k + 2 * h)),
        pl.BlockSpec((lat_block, LANES), lambda b, h, i: (b * lat_tiles, col_k + 2 * h + 1)),
        pl.BlockSpec((lat_block, dv), lambda b, h, i: (b * lat_tiles, col_v // 2 + h)),
        pl.BlockSpec((n_ctx, LANES), lambda b, h, i: (b, col_k + 2 * h)),
        pl.BlockSpec((n_ctx, LANES), lambda b, h, i: (b, col_k + 2 * h + 1)),
        pl.BlockSpec((n_ctx, dv), lambda b, h, i: (b, col_v // 2 + h)),
        lam_spec, lam_spec, lam_spec, lam_spec,
        pl.BlockSpec((1, dv), lambda b, h, i: (0, 0)),
    ]
    return pl.pallas_call(
        functools.partial(_attn_b_kernel, tq=tq, has_lat=not q_from_ctx, lam_init=lam_init),
        out_shape=jax.ShapeDtypeStruct((batch * lq, B_HEADS * dv), BF16),
        grid=(batch, B_HEADS, nqb),
        in_specs=in_specs,
        out_specs=pl.BlockSpec((bq, dv), lambda b, h, i: (b * nqb + i, h)),
        scratch_shapes=[pltpu.VMEM((n_keys, LANES), BF16), pltpu.VMEM((n_keys, LANES), BF16),
                        pltpu.VMEM((n_keys, dv), BF16)] + [pltpu.VMEM((tq, n_keys), BF16)] * depth,
        compiler_params=_params(("arbitrary",) * 3, 2 * bq * LANES * 2, 4 * lat_block * LANES * 2,
                                4 * n_ctx * LANES * 2, bq * dv * 2, (4 * LANES + depth * tq) * n_keys,
                                4 * tq * n_keys * 4 // 2),
        name="attn_diff_ctx" if q_from_ctx else "attn_diff",
    )(qsrc, qsrc, z, z, z, zc, zc, zc,
      lq1.reshape(B_HEADS, 1, HEAD_DIM), lk1.reshape(B_HEADS, 1, HEAD_DIM),
      lq2.reshape(B_HEADS, 1, HEAD_DIM), lk2.reshape(B_HEADS, 1, HEAD_DIM), subln.reshape(1, dv))


def _attn_c_kernel(q_ref, kl_ref, vl_ref, kc_ref, vc_ref, o_ref, k_ref, v1_ref, *p_slots, tq, group):
    dh = vl_ref.shape[1]

    @pl.when(pl.program_id(1) % group == 0)
    def _():
        _join_rows(k_ref, kl_ref, kc_ref, True)
        _join_rows(v1_ref, vl_ref, vc_ref, True)
        v1_ref[:, dh:] = jnp.ones((v1_ref.shape[0], v1_ref.shape[1] - dh), v1_ref.dtype)

    def score_stage(b, slot):
        r0 = pl.multiple_of(b * tq, tq)
        p_slots[slot][...] = _exp2_numerators(q_ref[pl.ds(r0, tq), :], k_ref[...]).astype(BF16)

    def value_stage(b, slot):
        r0 = pl.multiple_of(b * tq, tq)
        acc = jnp.dot(p_slots[slot][...], v1_ref[...], preferred_element_type=F32)
        o_ref[pl.ds(r0, tq), :] = (acc[:, :dh] / acc[:, dh:dh + 1]).astype(o_ref.dtype)

    n_blocks = q_ref.shape[0] // tq
    _pipeline_blocks(n_blocks, len(p_slots), score_stage, value_stage, unroll=n_blocks)


def _attn_c(z, zc, *, batch, seq_len, n_ctx, n_heads, col_k, col_v):
    g = n_heads // C_KV_HEADS
    tq, depth = 256, 8
    n_keys = seq_len + n_ctx
    return pl.pallas_call(
        functools.partial(_attn_c_kernel, tq=tq, group=g),
        out_shape=jax.ShapeDtypeStruct((batch * seq_len, n_heads * HEAD_DIM), BF16),
        grid=(batch, n_heads),
        scratch_shapes=[pltpu.VMEM((n_keys, LANES), BF16), pltpu.VMEM((n_keys, 2 * LANES), BF16)]
        + [pltpu.VMEM((tq, n_keys), BF16)] * depth,
        in_specs=[
            pl.BlockSpec((seq_len, LANES), lambda b, h: (b, h)),
            pl.BlockSpec((seq_len, LANES), lambda b, h: (b, col_k + h // g)),
            pl.BlockSpec((seq_len, LANES), lambda b, h: (b, col_v + h // g)),
            pl.BlockSpec((n_ctx, LANES), lambda b, h: (b, col_k + h // g)),
            pl.BlockSpec((n_ctx, LANES), lambda b, h: (b, col_v + h // g)),
        ],
        out_specs=pl.BlockSpec((seq_len, LANES), lambda b, h: (b, h)),
        compiler_params=_params(("arbitrary",) * 2, 4 * seq_len * LANES * 2, 2 * n_ctx * LANES * 2,
                                (3 * LANES + depth * tq) * n_keys, 3 * tq * n_keys * 4 // 2),
        name="attn_gqa",
    )(z, z, z, zc, zc)


def _outproj_kernel(*refs, n_lhs):
    lhs = refs[:n_lhs]
    w_ref, res_ref, gate_ref, o_ref, wbf_ref = refs[n_lhs:]

    @pl.when(pl.program_id(1) == 0)
    def _():
        _cast_rows(w_ref, wbf_ref)

    acc = None
    k0 = 0
    for a_ref in lhs:
        kw = a_ref.shape[1]
        part = jnp.dot(a_ref[...], wbf_ref[k0:k0 + kw, :], preferred_element_type=F32)
        acc = part if acc is None else acc + part
        k0 += kw
    o_ref[...] = res_ref[...] + gate_ref[0] * acc


def _out_proj(lhs_list, w_f32, res2d, gate, *, tm, rows_per_mod):
    rows, n = res2d.shape
    kdim = w_f32.shape[0]
    tn = 512
    in_specs = [pl.BlockSpec((tm, a.shape[1]), lambda j, i: (i, 0)) for a in lhs_list]
    in_specs += [
        pl.BlockSpec((kdim, tn), lambda j, i: (0, j)),
        pl.BlockSpec((tm, tn), lambda j, i: (i, j)),
        pl.BlockSpec((1, 1, tn), lambda j, i: ((i * tm) // rows_per_mod, 0, j)),
    ]
    return pl.pallas_call(
        functools.partial(_outproj_kernel, n_lhs=len(lhs_list)),
        out_shape=jax.ShapeDtypeStruct((rows, n), F32),
        grid=(n // tn, rows // tm),
        in_specs=in_specs,
        out_specs=pl.BlockSpec((tm, tn), lambda j, i: (i, j)),
        scratch_shapes=[pltpu.VMEM((kdim, tn), BF16)],
        compiler_params=_params(("arbitrary", "arbitrary"), tm * kdim * 2, kdim * tn * 4, 2 * tm * tn * 4,
                                kdim * tn),
        name="out_proj",
    )(*lhs_list, w_f32, res2d, gate)


def _ffn_kernel(x_ref, g_ref, sh_ref, sc_ref, gate_ref, wg_ref, wu_ref, wd_ref, o_ref, h_ref, acc_ref):
    f = pl.program_id(1)

    @pl.when(f == 0)
    def _():
        h_ref[...] = _rms_mod(x_ref[...], g_ref[...], sh_ref[0], sc_ref[0]).astype(BF16)
        acc_ref[...] = jnp.zeros_like(acc_ref)

    h = h_ref[...]
    g_ = jnp.dot(h, wg_ref[...], preferred_element_type=F32)
    u_ = jnp.dot(h, wu_ref[...], preferred_element_type=F32)
    a = (g_ * jax.nn.sigmoid(g_) * u_).astype(BF16)
    acc_ref[...] += jnp.dot(a, wd_ref[...], preferred_element_type=F32)

    @pl.when(f == pl.num_programs(1) - 1)
    def _():
        o_ref[...] = x_ref[...] + gate_ref[0] * acc_ref[...]


def _dense_ffn(x2d, norm_g, shift, scale, gate, w_in_bf16, w_down_bf16, *, tm, rows_per_mod):
    rows, d = x2d.shape
    d_ff = w_down_bf16.shape[0]
    tf = 512
    nf = d_ff // tf
    mod_spec = pl.BlockSpec((1, 1, d), lambda i, f: ((i * tm) // rows_per_mod, 0, 0))
    return pl.pallas_call(
        _ffn_kernel,
        out_shape=jax.ShapeDtypeStruct((rows, d), F32),
        grid=(rows // tm, nf),
        in_specs=[
            pl.BlockSpec((tm, d), lambda i, f: (i, 0)),
            pl.BlockSpec((1, d), lambda i, f: (0, 0)),
            mod_spec, mod_spec, mod_spec,
            pl.BlockSpec((d, tf), lambda i, f: (0, f)),
            pl.BlockSpec((d, tf), lambda i, f: (0, nf + f)),
            pl.BlockSpec((tf, d), lambda i, f: (f, 0)),
        ],
        out_specs=pl.BlockSpec((tm, d), lambda i, f: (i, 0)),
        scratch_shapes=[pltpu.VMEM((tm, d), BF16), pltpu.VMEM((tm, d), F32)],
        compiler_params=_params(("arbitrary", "arbitrary"), 2 * tm * d * 4, 3 * d * tf * 2, tm * d * 3),
        name="dense_swiglu",
    )(x2d, norm_g.reshape(1, d), shift, scale, gate, w_in_bf16, w_in_bf16, w_down_bf16)


def _split3(v):
    hi = v.astype(BF16)
    r = v - hi.astype(F32)
    mid = r.astype(BF16)
    lo = (r - mid.astype(F32)).astype(BF16)
    return hi, mid, lo


def _router_kernel(x_ref, g_ref, sh_ref, sc_ref, wr_ref, h_ref, ids_ref, gates_ref, cnt_ref, carry_ref):
    i = pl.program_id(0)
    tm = x_ref.shape[0]

    @pl.when(i == 0)
    def _():
        carry_ref[...] = jnp.zeros_like(carry_ref)

    h = _rms_mod(x_ref[...], g_ref[...], sh_ref[0], sc_ref[0])
    h_ref[...] = h
    h0, h1, h2 = _split3(h)
    w0, w1, w2 = _split3(wr_ref[...])
    dot = functools.partial(jnp.dot, preferred_element_type=F32)
    logits = (dot(h0, w0) + (dot(h0, w1) + dot(h1, w0))
              + (dot(h0, w2) + dot(h1, w1) + dot(h2, w0)))
    lane = lax.broadcasted_iota(jnp.int32, (tm, LANES), 1)
    logits = jnp.where(lane < N_EXPERTS, logits, -jnp.inf)
    v0 = jnp.max(logits, axis=-1, keepdims=True)
    i0 = jnp.min(jnp.where(logits == v0, lane, LANES), axis=-1, keepdims=True)
    rest = jnp.where(lane == i0, -jnp.inf, logits)
    v1 = jnp.max(rest, axis=-1, keepdims=True)
    i1 = jnp.min(jnp.where(rest == v1, lane, LANES), axis=-1, keepdims=True)
    e1 = jnp.exp(v1 - v0)
    g0 = 1.0 / (1.0 + e1)
    g1 = e1 / (1.0 + e1)

    sel = (lane == i0) | (lane == i1)
    row = lax.broadcasted_iota(jnp.int32, (tm, tm), 0)
    col = lax.broadcasted_iota(jnp.int32, (tm, tm), 1)
    tri = (col < row).astype(BF16)
    before = jnp.dot(tri, sel.astype(BF16), preferred_element_type=F32) + carry_ref[...]
    r0 = jnp.sum(jnp.where(lane == i0, before, 0.0), axis=-1, keepdims=True).astype(jnp.int32)
    r1 = jnp.sum(jnp.where(lane == i1, before, 0.0), axis=-1, keepdims=True).astype(jnp.int32)
    total = carry_ref[...] + jnp.sum(sel.astype(F32), axis=0, keepdims=True)
    carry_ref[...] = total

    ids_ref[...] = jnp.where(lane == 0, i0, jnp.where(lane == 1, i1, jnp.where(lane == 2, r0, r1)))
    gates_ref[...] = jnp.where(lane == 0, g0, g1)
    cnt_ref[...] = jnp.broadcast_to(total, cnt_ref.shape).astype(jnp.int32)


def _router(x2d, norm_g, shift, scale, w_router, *, tm, rows_per_mod):
    rows, d = x2d.shape
    wr = jnp.zeros((d, LANES), F32).at[:, :N_EXPERTS].set(w_router)
    mod_spec = pl.BlockSpec((1, 1, d), lambda i: ((i * tm) // rows_per_mod, 0, 0))
    return pl.pallas_call(
        _router_kernel,
        out_shape=(jax.ShapeDtypeStruct((rows, d), F32),
                   jax.ShapeDtypeStruct((rows, LANES), jnp.int32),
                   jax.ShapeDtypeStruct((rows, LANES), F32),
                   jax.ShapeDtypeStruct((8, LANES), jnp.int32)),
        grid=(rows // tm,),
        in_specs=[pl.BlockSpec((tm, d), lambda i: (i, 0)),
                  pl.BlockSpec((1, d), lambda i: (0, 0)),
                  mod_spec, mod_spec,
                  pl.BlockSpec((d, LANES), lambda i: (0, 0))],
        out_specs=(pl.BlockSpec((tm, d), lambda i: (i, 0)),
                   pl.BlockSpec((tm, LANES), lambda i: (i, 0)),
                   pl.BlockSpec((tm, LANES), lambda i: (i, 0)),
                   pl.BlockSpec((8, LANES), lambda i: (0, 0))),
        scratch_shapes=[pltpu.VMEM((1, LANES), F32)],
        compiler_params=_params(("arbitrary",), 2 * tm * d * 4, d * LANES * 4, 2 * tm * LANES * 4),
        name="router",
    )(x2d, norm_g.reshape(1, d), shift, scale, wr)


GATHER_UNROLL = 8


def _start_row_gather(src_hbm, idx_ref, base, dst_ref, sem):
    def body(r8, c):
        for u in range(GATHER_UNROLL):
            r = r8 * GATHER_UNROLL + u
            pltpu.make_async_copy(src_hbm.at[pl.ds(idx_ref[base + r], 1), :],
                                  dst_ref.at[pl.ds(r, 1), :], sem).start()
        return c

    lax.fori_loop(0, dst_ref.shape[0] // GATHER_UNROLL, body, 0)


def _wait_row_gather(src_hbm, dst_ref, sem):
    pltpu.make_async_copy(src_hbm.at[pl.ds(0, dst_ref.shape[0]), :], dst_ref, sem).wait()


def _dispatch_kernel(idx_ref, src_hbm, o_ref, buf_ref, sem):
    tr = o_ref.shape[0]
    i, n = pl.program_id(0), pl.num_programs(0)
    slot = i & 1

    @pl.when(i == 0)
    def _():
        _start_row_gather(src_hbm, idx_ref, 0, buf_ref.at[0], sem.at[0])

    @pl.when(i + 1 < n)
    def _():
        _start_row_gather(src_hbm, idx_ref, (i + 1) * tr, buf_ref.at[1 - slot], sem.at[1 - slot])

    _wait_row_gather(src_hbm, buf_ref.at[slot], sem.at[slot])
    o_ref[...] = buf_ref[slot].astype(o_ref.dtype)


def _dispatch(h2d, row_tok, *, tr):
    n_rows = row_tok.shape[0]
    d = h2d.shape[1]
    grid_spec = pltpu.PrefetchScalarGridSpec(
        num_scalar_prefetch=1,
        grid=(n_rows // tr,),
        in_specs=[pl.BlockSpec(memory_space=pl.ANY)],
        out_specs=pl.BlockSpec((tr, d), lambda i, idx: (i, 0)),
        scratch_shapes=[pltpu.VMEM((2, tr, d), F32), pltpu.SemaphoreType.DMA((2,))],
    )
    return pl.pallas_call(
        _dispatch_kernel,
        out_shape=jax.ShapeDtypeStruct((n_rows, d), BF16),
        grid_spec=grid_spec,
        compiler_params=_params(("arbitrary",), tr * d * 2, tr * d * 4),
        name="expert_dispatch",
    )(row_tok, h2d)


def _combine_kernel(d0_ref, d1_ref, ys_hbm, x_ref, gate_ref, rg_ref, o_ref, a_ref, b_ref, sem):
    tt = x_ref.shape[0]
    i, n = pl.program_id(0), pl.num_programs(0)
    slot = i & 1

    def start(tile, s):
        _start_row_gather(ys_hbm, d0_ref, tile * tt, a_ref.at[s], sem.at[s, 0])
        _start_row_gather(ys_hbm, d1_ref, tile * tt, b_ref.at[s], sem.at[s, 1])

    @pl.when(i == 0)
    def _():
        start(0, 0)

    @pl.when(i + 1 < n)
    def _():
        start(i + 1, 1 - slot)

    _wait_row_gather(ys_hbm, a_ref.at[slot], sem.at[slot, 0])
    _wait_row_gather(ys_hbm, b_ref.at[slot], sem.at[slot, 1])
    rg = rg_ref[...]
    mix = rg[:, 0:1] * a_ref[slot] + rg[:, 1:2] * b_ref[slot]
    o_ref[...] = x_ref[...] + gate_ref[0] * mix


def _combine(ys, dest0, dest1, x2d, gate, row_gates, *, tt, rows_per_mod):
    rows, d = x2d.shape
    grid_spec = pltpu.PrefetchScalarGridSpec(
        num_scalar_prefetch=2,
        grid=(rows // tt,),
        in_specs=[pl.BlockSpec(memory_space=pl.ANY),
                  pl.BlockSpec((tt, d), lambda i, a, b: (i, 0)),
                  pl.BlockSpec((1, 1, d), lambda i, a, b: ((i * tt) // rows_per_mod, 0, 0)),
                  pl.BlockSpec((tt, LANES), lambda i, a, b: (i, 0))],
        out_specs=pl.BlockSpec((tt, d), lambda i, a, b: (i, 0)),
        scratch_shapes=[pltpu.VMEM((2, tt, d), F32), pltpu.VMEM((2, tt, d), F32),
                        pltpu.SemaphoreType.DMA((2, 2))],
    )
    return pl.pallas_call(
        _combine_kernel,
        out_shape=jax.ShapeDtypeStruct((rows, d), F32),
        grid_spec=grid_spec,
        compiler_params=_params(("arbitrary",), 2 * tt * d * 4, 2 * tt * d * 4),
        name="expert_combine",
    )(dest0, dest1, ys, x2d, gate, row_gates)


WEIGHT_DMA_PRIORITY = 1


def _stream_group_weights(sched, copies, cast):
    te_ref, first_ref, grp_ref, ngrp_ref, nxt_ref = sched
    j, r = pl.program_id(0), pl.program_id(1)
    nj = pl.num_programs(0)

    @pl.when(first_ref[r] == 1)
    def _():
        g = grp_ref[r]
        ng = ngrp_ref[0]

        @pl.when((j == 0) & (g == 0))
        def _():
            for cp in copies(te_ref[r], j):
                cp.start(priority=WEIGHT_DMA_PRIORITY)

        for cp in copies(te_ref[r], j):
            cp.wait()
        cast()
        last = g == ng - 1

        @pl.when(jnp.logical_not(last & (j == nj - 1)))
        def _():
            for cp in copies(nxt_ref[r], jnp.where(last, j + 1, j)):
                cp.start(priority=WEIGHT_DMA_PRIORITY)


CAST_ROWS = 256


def _cast_rows(src_ref, dst_ref):
    def body(i, carry):
        rows = pl.ds(pl.multiple_of(i * CAST_ROWS, CAST_ROWS), CAST_ROWS)
        dst_ref[rows, :] = src_ref[rows, :].astype(dst_ref.dtype)
        return carry

    lax.fori_loop(0, src_ref.shape[0] // CAST_ROWS, body, 0)


def _gmm_up_kernel(te_ref, first_ref, grp_ref, ngrp_ref, nxt_ref, nv_ref, x_ref, w_hbm, o_ref,
                   stage_ref, wbf_ref, sem):
    tn = o_ref.shape[1]
    nj = pl.num_programs(0)

    def copies(e, j):
        return [pltpu.make_async_copy(w_hbm.at[e, :, pl.ds(pl.multiple_of((half * nj + j) * tn, tn), tn)],
                                      stage_ref.at[half], sem.at[half]) for half in range(2)]

    def cast():
        _cast_rows(stage_ref.at[0], wbf_ref.at[0])
        _cast_rows(stage_ref.at[1], wbf_ref.at[1])

    _stream_group_weights((te_ref, first_ref, grp_ref, ngrp_ref, nxt_ref), copies, cast)
    valid = pl.program_id(1) < nv_ref[0]

    @pl.when(valid)
    def _():
        x = x_ref[...]
        g_ = jnp.dot(x, wbf_ref[0], preferred_element_type=F32)
        u_ = jnp.dot(x, wbf_ref[1], preferred_element_type=F32)
        o_ref[...] = (g_ * jax.nn.sigmoid(g_) * u_).astype(o_ref.dtype)

    @pl.when(jnp.logical_not(valid))
    def _():
        o_ref[...] = jnp.zeros_like(o_ref)


def _gmm_down_kernel(te_ref, first_ref, grp_ref, ngrp_ref, nxt_ref, nv_ref, h_ref, w_hbm, o_ref,
                     stage_ref, wbf_ref, sem):
    tn = o_ref.shape[1]

    def copies(e, j):
        return [pltpu.make_async_copy(w_hbm.at[e, :, pl.ds(pl.multiple_of(j * tn, tn), tn)], stage_ref, sem)]

    def cast():
        _cast_rows(stage_ref, wbf_ref)

    _stream_group_weights((te_ref, first_ref, grp_ref, ngrp_ref, nxt_ref), copies, cast)
    valid = pl.program_id(1) < nv_ref[0]

    @pl.when(valid)
    def _():
        o_ref[...] = jnp.dot(h_ref[...], wbf_ref[...], preferred_element_type=F32)

    @pl.when(jnp.logical_not(valid))
    def _():
        o_ref[...] = jnp.zeros_like(o_ref)


def _expert_ffn(xs, w_in, w_down, sched, n_valid, *, tr):
    n_rows, d = xs.shape
    d_ff = w_down.shape[1]
    n_tiles = n_rows // tr
    n_sched = len(sched) + 1
    tn1 = d_ff // 4
    nj1 = d_ff // tn1
    small_temps = 4 * 1024 * 1024

    def row_map(*args):
        r, nv = args[1], args[-1]
        return (jnp.minimum(r, nv[0] - 1), 0)

    def out_map(*args):
        return (args[1], args[0])

    up_spec = pltpu.PrefetchScalarGridSpec(
        num_scalar_prefetch=n_sched,
        grid=(nj1, n_tiles),
        in_specs=[pl.BlockSpec((tr, d), row_map), pl.BlockSpec(memory_space=pl.ANY)],
        out_specs=pl.BlockSpec((tr, tn1), out_map),
        scratch_shapes=[pltpu.VMEM((2, d, tn1), F32), pltpu.VMEM((2, d, tn1), BF16),
                        pltpu.SemaphoreType.DMA((2,))],
    )
    stage1 = 2 * d * tn1 * 4
    hidden = pl.pallas_call(
        _gmm_up_kernel,
        out_shape=jax.ShapeDtypeStruct((n_rows, d_ff), BF16),
        grid_spec=up_spec,
        compiler_params=pltpu.CompilerParams(
            dimension_semantics=("arbitrary", "arbitrary"),
            vmem_limit_bytes=stage1 + 2 * d * tn1 * 2 + 2 * (tr * d * 2 + tr * tn1 * 2) + 2 * small_temps),
        name="expert_up",
    )(*sched, n_valid, xs, w_in)

    tn2 = d // 2
    down_spec = pltpu.PrefetchScalarGridSpec(
        num_scalar_prefetch=n_sched,
        grid=(d // tn2, n_tiles),
        in_specs=[pl.BlockSpec((tr, d_ff), row_map), pl.BlockSpec(memory_space=pl.ANY)],
        out_specs=pl.BlockSpec((tr, tn2), out_map),
        scratch_shapes=[pltpu.VMEM((d_ff, tn2), F32), pltpu.VMEM((d_ff, tn2), BF16),
                        pltpu.SemaphoreType.DMA(())],
    )
    stage2 = d_ff * tn2 * 4
    return pl.pallas_call(
        _gmm_down_kernel,
        out_shape=jax.ShapeDtypeStruct((n_rows, d), F32),
        grid_spec=down_spec,
        compiler_params=pltpu.CompilerParams(
            dimension_semantics=("arbitrary", "arbitrary"),
            vmem_limit_bytes=stage2 + d_ff * tn2 * 2 + 2 * (tr * d_ff * 2 + tr * tn2 * 4) + small_temps),
        name="expert_down",
    )(*sched, n_valid, hidden, w_down)


def _moe(x2d, norm_g, shift, scale, gate, w_router, w_in, w_down, *, rows_per_mod):
    rows, d = x2d.shape
    tr = 256
    h2d, ids, gates, counts = _router(x2d, norm_g, shift, scale, w_router, tm=512, rows_per_mod=rows_per_mod)
    counts = counts[0, :N_EXPERTS]
    padded = (counts + tr - 1) // tr * tr
    pad_end = jnp.cumsum(padded)
    pad_start = pad_end - padded
    dest0 = pad_start[ids[:, 0]] + ids[:, 2]
    dest1 = pad_start[ids[:, 1]] + ids[:, 3]
    n_rows = 2 * rows + N_EXPERTS * tr
    n_tiles = n_rows // tr
    tok = jnp.arange(rows, dtype=jnp.int32)
    row_tok = jnp.zeros((n_rows,), jnp.int32).at[jnp.concatenate([dest0, dest1])].set(
        jnp.concatenate([tok, tok]))
    tile_expert = jnp.minimum(
        jnp.sum(jnp.arange(n_tiles, dtype=jnp.int32)[:, None] * tr >= pad_end[None, :], axis=1),
        N_EXPERTS - 1).astype(jnp.int32)
    n_valid = (pad_end[-1:] // tr).astype(jnp.int32)
    tile_id = jnp.arange(n_tiles, dtype=jnp.int32)
    prev_expert = jnp.concatenate([jnp.full((1,), -1, jnp.int32), tile_expert[:-1]])
    first = ((tile_id < n_valid[0]) & (tile_expert != prev_expert)).astype(jnp.int32)
    grp = jnp.cumsum(first).astype(jnp.int32) - 1
    n_grp = jnp.sum(first, keepdims=True).astype(jnp.int32)
    expert_id = jnp.arange(N_EXPERTS, dtype=jnp.int32)
    grp_expert = jnp.sort(jnp.where(counts > 0, expert_id, N_EXPERTS))
    nxt = grp_expert[(grp + 1) % n_grp[0]].astype(jnp.int32)
    sched = (tile_expert, first, grp, n_grp, nxt)

    xs = _dispatch(h2d, row_tok, tr=tr)
    ys = _expert_ffn(xs, w_in, w_down, sched, n_valid, tr=tr)
    return _combine(ys, dest0, dest1, x2d, gate, gates, tt=256, rows_per_mod=rows_per_mod)


def _rope_tables(seq_len):
    t = np.arange(seq_len)
    row = (t // GRID_W).astype(np.float64)[:, None]
    col = (t % GRID_W).astype(np.float64)[:, None]
    inv = ROPE_THETA ** (-np.arange(0, ROT_AXIS, 2, dtype=np.float64) / ROT_AXIS)
    ar, ac = row * inv, col * inv
    zero = np.zeros_like(ar)
    cos = np.concatenate([np.cos(ar), np.cos(ar), np.cos(ac), np.cos(ac)], axis=-1)
    s_next = np.concatenate([-np.sin(ar), zero, -np.sin(ac), zero], axis=-1)
    s_prev = np.concatenate([zero, np.sin(ar), zero, np.sin(ac)], axis=-1)
    return tuple(jnp.asarray(a.astype(np.float32)) for a in (cos, s_next, s_prev))


def _mod_vectors(c, c_ctx, w_mod, b_mod):
    batch, d = c.shape
    cvec = jnp.zeros((8, d), F32).at[:batch].set(c).at[batch].set(c_ctx)
    m = _modulation(jnp.concatenate([cvec, cvec], axis=0), w_mod, b_mod)
    lat = m[:batch].reshape(batch, N_MOD, 1, d)
    ctx = m[batch].reshape(N_MOD, 1, 1, d)
    return [lat[:, k] for k in range(N_MOD)], [ctx[k] for k in range(N_MOD)]


def kernel(x, c, ctx, c_ctx, e_norm1, e_norm2, e_w_mod, e_b_mod, e_w_in, e_w_out, e_a_qnorm, e_a_knorm,
           e_a_sink, e_b_qnorm, e_b_knorm, e_b_lam_q1, e_b_lam_k1, e_b_lam_q2, e_b_lam_k2, e_b_subln,
           e_ffn_w_in, e_ffn_w_down, o_norm1, o_norm2, o_w_mod, o_b_mod, o_w_in, o_w_out, o_c_qnorm,
           o_c_knorm, o_router, o_exp_w_in, o_exp_w_down):
    batch, seq_len, d = x.shape
    n_ctx = ctx.shape[1]
    qk_scale = HEAD_DIM ** -0.5 * LOG2_E
    tables = _rope_tables(seq_len)
    ctx_tables = tuple(t[:batch * n_ctx] for t in tables)
    x2d = x.reshape(batch * seq_len, d)
    xc2d = ctx.reshape(batch * n_ctx, d)
    tm = 1024

    (sh1, sc1, g1, sh2, sc2, g2), (csh1, csc1, cg1, csh2, csc2, cg2) = _mod_vectors(
        c, c_ctx, e_w_mod[0], e_b_mod[0])
    a_q, a_kv = A_HEADS * HEAD_DIM, A_KV_HEADS * HEAD_DIM
    b_qk, b_v = B_HEADS * 2 * HEAD_DIM, B_HEADS * 2 * HEAD_DIM
    ones = lambda n: jnp.ones((n,), F32)
    tile = lambda v, n: jnp.tile(v, n // HEAD_DIM)
    col_gain = jnp.concatenate([tile(e_a_qnorm[0] * qk_scale, a_q), tile(e_a_knorm[0], a_kv), ones(a_kv),
                                tile(e_b_qnorm[0] * qk_scale, b_qk), tile(e_b_knorm[0], b_qk), ones(b_v)])
    kinds = lambda *pairs: jnp.concatenate([jnp.full((n // HEAD_DIM,), k, jnp.int32) for k, n in pairs])
    col_kind = kinds((1, a_q), (1, a_kv), (0, a_kv), (1, b_qk), (1, b_qk), (0, b_v))
    w_in0 = e_w_in[0].astype(BF16)
    z = _in_proj(x2d, e_norm1[0], sh1, sc1, w_in0, col_gain, col_kind, tables,
                 tm=tm, rows_per_mod=seq_len, rope=True)
    zc = _in_proj(xc2d, e_norm1[0], csh1, csc1, w_in0, col_gain, col_kind, ctx_tables,
                  tm=batch * n_ctx, rows_per_mod=batch * n_ctx, rope=False)

    ca_q, ca_k, ca_v = 0, a_q // LANES, (a_q + a_kv) // LANES
    cb_q = (a_q + 2 * a_kv) // LANES
    cb_k, cb_v = cb_q + b_qk // LANES, cb_q + 2 * b_qk // LANES
    lam_init = 0.8 - 0.6 * math.exp(-0.3 * 0)
    dims = dict(batch=batch, seq_len=seq_len, n_ctx=n_ctx)
    attn_a = functools.partial(_attn_a, z, zc, e_a_sink[0], col_q=ca_q, col_k=ca_k, col_v=ca_v, **dims)
    attn_b = functools.partial(_attn_b, z, zc, e_b_lam_q1[0], e_b_lam_k1[0], e_b_lam_q2[0], e_b_lam_k2[0],
                               e_b_subln[0], col_q=cb_q, col_k=cb_k, col_v=cb_v, lam_init=lam_init, **dims)
    w_out0 = e_w_out[0]
    x2d = _out_proj([attn_a(q_from_ctx=False), attn_b(q_from_ctx=False)], w_out0, x2d, g1,
                    tm=tm, rows_per_mod=seq_len)
    xc2d = _out_proj([attn_a(q_from_ctx=True), attn_b(q_from_ctx=True)], w_out0, xc2d, cg1,
                     tm=batch * n_ctx, rows_per_mod=batch * n_ctx)
    ffn_in, ffn_down = e_ffn_w_in[0].astype(BF16), e_ffn_w_down[0].astype(BF16)
    x2d = _dense_ffn(x2d, e_norm2[0], sh2, sc2, g2, ffn_in, ffn_down, tm=512, rows_per_mod=seq_len)
    xc2d = _dense_ffn(xc2d, e_norm2[0], csh2, csc2, cg2, ffn_in, ffn_down, tm=batch * n_ctx,
                      rows_per_mod=batch * n_ctx)

    (sh1, sc1, g1, sh2, sc2, g2), (csh1, csc1, _, _, _, _) = _mod_vectors(c, c_ctx, o_w_mod[0], o_b_mod[0])
    c_q = d
    c_kv = C_KV_HEADS * HEAD_DIM
    col_gain = jnp.concatenate([tile(o_c_qnorm[0] * qk_scale, c_q), tile(o_c_knorm[0], c_kv), ones(c_kv)])
    col_kind = kinds((1, c_q), (1, c_kv), (0, c_kv))
    w_in1 = o_w_in[0].astype(BF16)
    z = _in_proj(x2d, o_norm1[0], sh1, sc1, w_in1, col_gain, col_kind, tables,
                 tm=tm, rows_per_mod=seq_len, rope=True)
    zc = _in_proj(xc2d, o_norm1[0], csh1, csc1, w_in1, col_gain, col_kind, ctx_tables,
                  tm=batch * n_ctx, rows_per_mod=batch * n_ctx, rope=False)
    o = _attn_c(z, zc, n_heads=c_q // HEAD_DIM, col_k=c_q // LANES, col_v=(c_q + c_kv) // LANES, **dims)
    x2d = _out_proj([o], o_w_out[0], x2d, g1, tm=tm, rows_per_mod=seq_len)
    x2d = _moe(x2d, o_norm2[0], sh2, sc2, g2, o_router[0], o_exp_w_in.reshape(o_exp_w_in.shape[1:]),
               o_exp_w_down.reshape(o_exp_w_down.shape[1:]), rows_per_mod=seq_len)
    return x2d.reshape(batch, seq_len, d)
```
